```python
import math
import jax, jax.numpy as jnp
from jax import lax
import numpy as np

D_MODEL = 1024
BATCH = 4
SEQ = 8192
DEPTH = 1

MEM_LEN = 256
M_HEADS = 4
M_DQK = 128
M_DV = 256
CHUNK = 128
CONV_K = 4
D_HEADS = 8
D_HD = 64
D_DV = 2 * D_HD
Q_BLOCK = 128
C_HEADS = 4
C_DQK = 128
C_DV = 256
D_FF = 4 * D_MODEL
N_BRANCH = 3
EPS = 1e-6

M_QK_W = M_HEADS * M_DQK
M_V_W = M_HEADS * M_DV
D_Q_W = D_HEADS * 2 * D_HD
D_V_W = D_HEADS * D_DV
C_Q_W = C_HEADS * C_DQK
C_V_W = C_HEADS * C_DV
IN_SPLIT_SIZES = (M_QK_W, M_QK_W, M_V_W, M_HEADS, M_HEADS, M_V_W, D_Q_W, D_Q_W, D_V_W, C_Q_W)
IN_COLS = sum(IN_SPLIT_SIZES)

kernel_name = 'hybrid_mlstm_diffattn_memxattn_block'


def rms_norm(x, g):
    xf = x.astype(jnp.float32)
    y = xf * lax.rsqrt(jnp.mean(xf * xf, axis=-1, keepdims=True) + EPS)
    return (y * g.astype(jnp.float32)).astype(x.dtype)


def lambda_init(layer):
    return 0.8 - 0.6 * math.exp(-0.3 * layer)


def causal_depthwise_conv(x, w, b):
    c = x.shape[-1]
    y = lax.conv_general_dilated(x, w.astype(x.dtype).reshape(CONV_K, 1, c), window_strides=(1,),
                                 padding=[(CONV_K - 1, 0)], dimension_numbers=('NWC', 'WIO', 'NWC'),
                                 feature_group_count=c)
    return y + b.astype(x.dtype)


def mlstm_chunkwise(q, k, v, ig, fg_pre):
    bsz, seq, nh, dk = q.shape
    dv = v.shape[-1]
    nc = seq // CHUNK
    f32 = jnp.float32

    def to_chunks(t):
        t = t.astype(f32).reshape(bsz, nc, CHUNK, nh, -1)
        return t.transpose(1, 0, 3, 2, 4)

    qc = to_chunks(q * (dk ** -0.5))
    kc = to_chunks(k)
    vc = to_chunks(v)
    igc = to_chunks(ig[..., None])[..., 0]
    lfc = to_chunks(jax.nn.log_sigmoid(fg_pre.astype(f32))[..., None])[..., 0]
    causal = jnp.tril(jnp.ones((CHUNK, CHUNK), dtype=bool))

    def step(carry, xs):
        C, n, m = carry
        qq, kk, vv, ii, lf = xs
        bcum = jnp.cumsum(lf, axis=-1)
        logd = bcum[..., :, None] - bcum[..., None, :] + ii[..., None, :]
        logd = jnp.where(causal, logd, -jnp.inf)
        inter = bcum + m[..., None]
        m_loc = jnp.maximum(inter, jnp.max(logd, axis=-1))
        w_intra = jnp.exp(logd - m_loc[..., None])
        w_inter = jnp.exp(inter - m_loc)
        s = jnp.einsum('bhjd,bhsd->bhjs', qq, kk) * w_intra
        num = jnp.einsum('bhjs,bhsv->bhjv', s, vv) + w_inter[..., None] * jnp.einsum('bhjd,bhdv->bhjv', qq, C)
        den = jnp.sum(s, axis=-1) + w_inter * jnp.einsum('bhjd,bhd->bhj', qq, n)
        h = num / jnp.maximum(jnp.abs(den), jnp.exp(-m_loc))[..., None]
        b_end = bcum[..., -1]
        log_w = b_end[..., None] - bcum + ii
        m_new = jnp.maximum(b_end + m, jnp.max(log_w, axis=-1))
        w_s = jnp.exp(log_w - m_new[..., None])
        decay = jnp.exp(b_end + m - m_new)
        C_new = decay[..., None, None] * C + jnp.einsum('bhs,bhsd,bhsv->bhdv', w_s, kk, vv)
        n_new = decay[..., None] * n + jnp.einsum('bhs,bhsd->bhd', w_s, kk)
        return (C_new, n_new, m_new), h

    init = (jnp.zeros((bsz, nh, dk, dv), f32), jnp.zeros((bsz, nh, dk), f32), jnp.zeros((bsz, nh), f32))
    _, hc = lax.scan(step, init, (qc, kc, vc, igc, lfc))
    return hc.transpose(1, 0, 3, 2, 4).reshape(bsz, seq, nh, dv)


def diff_attention(q, k, v, q_g, k_g, lam, subln_g, lam_init):
    q = rms_norm(q, q_g)
    k = rms_norm(k, k_g)
    bsz, seq, nh, _, d = q.shape
    nb = seq // Q_BLOCK
    qb = q.reshape(bsz, nb, Q_BLOCK, nh, 2, d).transpose(1, 0, 3, 4, 2, 5)
    kt = k.transpose(0, 2, 3, 1, 4)
    vt = v.transpose(0, 2, 1, 3)
    kpos = jnp.arange(seq)
    scale = d ** -0.5

    def block(args):
        qblk, i = args
        s = jnp.einsum('bhcqd,bhckd->bhcqk', qblk, kt).astype(jnp.float32) * scale
        qpos = i * Q_BLOCK + jnp.arange(Q_BLOCK)
        s = jnp.where(kpos[None, :] <= qpos[:, None], s, -jnp.inf)
        p = jax.nn.softmax(s, axis=-1)
        a = p[:, :, 0] - lam * p[:, :, 1]
        return jnp.einsum('bhqk,bhkv->bhqv', a.astype(vt.dtype), vt)

    o = lax.map(block, (qb, jnp.arange(nb)))
    o = o.transpose(1, 0, 3, 2, 4).reshape(bsz, seq, nh, -1)
    o = rms_norm(o, subln_g) * (1.0 - lam_init)
    return o.reshape(bsz, seq, -1)


def memory_cross_attention(q, mem, mem_g, w_kv, q_g, k_g):
    bsz, seq, _ = q.shape
    qh = rms_norm(q.reshape(bsz, seq, C_HEADS, C_DQK), q_g)
    mkv = rms_norm(mem, mem_g) @ w_kv
    mk, mv = jnp.split(mkv, [C_Q_W], axis=-1)
    mk = rms_norm(mk.reshape(bsz, -1, C_HEADS, C_DQK), k_g)
    mv = mv.reshape(bsz, -1, C_HEADS, C_DV)
    s = jnp.einsum('bshd,bmhd->bhsm', qh, mk).astype(jnp.float32) * (C_DQK ** -0.5)
    p = jax.nn.softmax(s, axis=-1)
    o = jnp.einsum('bhsm,bmhv->bshv', p.astype(mv.dtype), mv)
    return o.reshape(bsz, seq, -1)


def setup_inputs(seed: int = 0) -> dict:
    key = jax.random.key(seed)
    ks = iter(jax.random.split(key, 40))

    def nrm(shape, scale):
        return scale * jax.random.normal(next(ks), shape, jnp.float32)

    def gain(shape):
        return 1.0 + nrm(shape, 0.02)

    L = DEPTH
    return {
        'x': nrm((BATCH, SEQ, D_MODEL), 1.0),
        'mem': nrm((BATCH, MEM_LEN, D_MODEL), 1.0),
        'norm_mix_g': gain((L, D_MODEL)),
        'w_in': nrm((L, D_MODEL, IN_COLS), D_MODEL ** -0.5),
        'b_igate': nrm((L, M_HEADS), 0.1),
        'b_fgate': jnp.linspace(3.0, 6.0, M_HEADS, dtype=jnp.float32)[None, :] + nrm((L, M_HEADS), 0.1),
        'conv_w': nrm((L, CONV_K, 2 * M_QK_W), CONV_K ** -0.5),
        'conv_b': nrm((L, 2 * M_QK_W), 0.02),
        'm_norm_g': gain((L, M_V_W)),
        'dq_norm_g': gain((L, D_HD)),
        'dk_norm_g': gain((L, D_HD)),
        'lam_q1': nrm((L, D_HD), 0.1),
        'lam_k1': nrm((L, D_HD), 0.1),
        'lam_q2': nrm((L, D_HD), 0.1),
        'lam_k2': nrm((L, D_HD), 0.1),
        'subln_g': gain((L, D_DV)),
        'cq_norm_g': gain((L, C_DQK)),
        'ck_norm_g': gain((L, C_DQK)),
        'mem_norm_g': gain((L, D_MODEL)),
        'w_mem_kv': nrm((L, D_MODEL, C_Q_W + C_V_W), D_MODEL ** -0.5),
        'w_gate': nrm((L, D_MODEL, N_BRANCH * D_MODEL), D_MODEL ** -0.5),
        'b_gate': nrm((L, N_BRANCH * D_MODEL), 0.02),
        'w_proj_m': nrm((L, M_V_W, D_MODEL), M_V_W ** -0.5),
        'w_proj_d': nrm((L, D_V_W, D_MODEL), D_V_W ** -0.5),
        'w_proj_c': nrm((L, C_V_W, D_MODEL), C_V_W ** -0.5),
        'w_out': nrm((L, D_MODEL, D_MODEL), D_MODEL ** -0.5),
        'norm_mlp_g': gain((L, D_MODEL)),
        'w_up': nrm((L, D_MODEL, D_FF), D_MODEL ** -0.5),
        'w_down': nrm((L, D_FF, D_MODEL), D_FF ** -0.5),
    }


def reference(x, mem, norm_mix_g, w_in, b_igate, b_fgate, conv_w, conv_b, m_norm_g, dq_norm_g, dk_norm_g,
              lam_q1, lam_k1, lam_q2, lam_k2, subln_g, cq_norm_g, ck_norm_g, mem_norm_g, w_mem_kv,
              w_gate, b_gate, w_proj_m, w_proj_d, w_proj_c, w_out, norm_mlp_g, w_up, w_down):
    split_idx = np.cumsum(np.array(IN_SPLIT_SIZES))[:-1].tolist()
    bsz, seq, _ = x.shape
    for l in range(DEPTH):
        h = rms_norm(x, norm_mix_g[l])
        proj = h @ w_in[l]
        mq, mk, mv, mi, mf, mo, dq, dk, dv, cq = jnp.split(proj, split_idx, axis=-1)

        qk = jax.nn.silu(causal_depthwise_conv(jnp.concatenate([mq, mk], axis=-1), conv_w[l], conv_b[l]))
        mq, mk = jnp.split(qk, 2, axis=-1)
        hm = mlstm_chunkwise(mq.reshape(bsz, seq, M_HEADS, M_DQK), mk.reshape(bsz, seq, M_HEADS, M_DQK),
                             mv.reshape(bsz, seq, M_HEADS, M_DV), mi + b_igate[l], mf + b_fgate[l]).astype(x.dtype)
        y_m = jax.nn.sigmoid(mo) * rms_norm(hm, m_norm_g[l].reshape(M_HEADS, M_DV)).reshape(bsz, seq, M_V_W)

        lam_i = lambda_init(l)
        lam = (jnp.exp(jnp.sum(lam_q1[l].astype(jnp.float32) * lam_k1[l].astype(jnp.float32)))
               - jnp.exp(jnp.sum(lam_q2[l].astype(jnp.float32) * lam_k2[l].astype(jnp.float32))) + lam_i)
        y_d = diff_attention(dq.reshape(bsz, seq, D_HEADS, 2, D_HD), dk.reshape(bsz, seq, D_HEADS, 2, D_HD),
                             dv.reshape(bsz, seq, D_HEADS, D_DV), dq_norm_g[l], dk_norm_g[l], lam,
                             subln_g[l], lam_i)

        y_c = memory_cross_attention(cq, mem, mem_norm_g[l], w_mem_kv[l], cq_norm_g[l], ck_norm_g[l])

        g_m, g_d, g_c = jnp.split(jax.nn.sigmoid(h @ w_gate[l] + b_gate[l]), N_BRANCH, axis=-1)
        merged = g_m * (y_m @ w_proj_m[l]) + g_d * (y_d @ w_proj_d[l]) + g_c * (y_c @ w_proj_c[l])
        x = x + merged @ w_out[l]

        h2 = rms_norm(x, norm_mlp_g[l])
        x = x + jnp.square(jax.nn.relu(h2 @ w_up[l])) @ w_down[l]
    return x
```

```python
import functools
import math

import jax
import jax.numpy as jnp
from jax import lax
from jax.experimental import pallas as pl
from jax.experimental.pallas import tpu as pltpu

F32 = jnp.float32
BF16 = jnp.bfloat16

EPS = 1e-6
LOG2E = math.log2(math.e)
NEG_BIG = -1e30

D_MODEL = 1024
MEM_LEN = 256
M_HEADS, M_DQK, M_DV = 4, 128, 256
CHUNK = 128
CONV_K = 4
D_HEADS, D_HD = 8, 64
D_DV = 2 * D_HD
C_HEADS, C_DQK, C_DV = 4, 128, 256
D_FF = 4 * D_MODEL
LANES = 128

M_QK_W = M_HEADS * M_DQK
M_V_W = M_HEADS * M_DV
D_Q_W = D_HEADS * 2 * D_HD
D_V_W = D_HEADS * D_DV
C_Q_W = C_HEADS * C_DQK
C_V_W = C_HEADS * C_DV

VMEM_LIMIT = 56 * 1024 * 1024


def _cparams(sem):
    return pltpu.CompilerParams(dimension_semantics=sem, vmem_limit_bytes=VMEM_LIMIT)


def _ep_identity(acc, aux_ref):
    return acc


def _ep_sigmoid_bias(acc, aux_ref):
    return jax.nn.sigmoid(acc + aux_ref[0:1, :])


def _ep_bias(acc, aux_ref):
    return acc + aux_ref[0:1, :]


def _ep_group_norm(group, acc, aux_ref):
    tn = acc.shape[1]
    r = lax.broadcasted_iota(jnp.int32, (LANES, LANES), 0) // group
    c = lax.broadcasted_iota(jnp.int32, (LANES, LANES), 1) // group
    gmat = (r == c).astype(BF16)
    outs = []
    for s in range(tn // LANES):
        a = acc[:, s * LANES:(s + 1) * LANES]
        ss = jnp.dot((a * a).astype(BF16), gmat, preferred_element_type=F32)
        outs.append(a * lax.rsqrt(ss * (1.0 / group) + EPS))
    y = jnp.concatenate(outs, axis=1) if len(outs) > 1 else outs[0]
    return y * aux_ref[0:1, :]


def _norm_matmul_kernel(x_ref, g_ref, w_ref, aux_ref, o_ref, h_ref, *, epilogue):
    @pl.when(pl.program_id(1) == 0)
    def _():
        x = x_ref[...].astype(F32)
        y = x * lax.rsqrt(jnp.mean(x * x, axis=-1, keepdims=True) + EPS)
        h_ref[...] = (y * g_ref[...]).astype(BF16)

    acc = jnp.dot(h_ref[...], w_ref[...], preferred_element_type=F32)
    o_ref[...] = epilogue(acc, aux_ref).astype(o_ref.dtype)


def _norm_matmul(x, g, w, aux, epilogue, out_dtype, tm, tn, name):
    n, d = x.shape
    cols = w.shape[1]
    tm = min(tm, n)
    tn = min(tn, cols)
    assert n % tm == 0 and cols % tn == 0
    return pl.pallas_call(
        functools.partial(_norm_matmul_kernel, epilogue=epilogue),
        grid=(n // tm, cols // tn),
        in_specs=[
            pl.BlockSpec((tm, d), lambda i, j: (i, 0)),
            pl.BlockSpec((1, d), lambda i, j: (0, 0)),
            pl.BlockSpec((d, tn), lambda i, j: (0, j)),
            pl.BlockSpec((8, tn), lambda i, j: (0, j)),
        ],
        out_specs=pl.BlockSpec((tm, tn), lambda i, j: (i, j)),
        out_shape=jax.ShapeDtypeStruct((n, cols), out_dtype),
        scratch_shapes=[pltpu.VMEM((tm, d), BF16)],
        compiler_params=_cparams(("parallel", "arbitrary")),
        name=name,
    )(x, g.reshape(1, d).astype(F32), w, aux)


def _aux_rows(row):
    return jnp.zeros((8, row.shape[0]), F32).at[0].set(row.astype(F32))


def _mlstm_kernel(qk_ref, v_ref, gate_ref, so_ref, cw_ref, cb_ref, ng_ref, o_ref,
                  xpad_ref, c_ref, n_ref, m_ref):
    L = CHUNK
    dk, dv, nh = M_DQK, M_DV, M_HEADS

    @pl.when(pl.program_id(1) == 0)
    def _():
        xpad_ref[0:8, :] = jnp.zeros((8, 2 * M_QK_W), F32)
        c_ref[...] = jnp.zeros_like(c_ref)
        n_ref[...] = jnp.zeros_like(n_ref)
        m_ref[...] = jnp.zeros_like(m_ref)

    x = qk_ref[...].astype(F32)
    xpad_ref[8:8 + L, :] = x
    y = cb_ref[0:1, :] + cw_ref[CONV_K - 1:CONV_K, :] * x
    for j in range(CONV_K - 1):
        y = y + cw_ref[j:j + 1, :] * xpad_ref[pl.ds(8 - (CONV_K - 1) + j, L), :]
    xpad_ref[0:8, :] = x[L - 8:L, :]
    qk = y * jax.nn.sigmoid(y)

    g = gate_ref[...]
    lf = jnp.minimum(g, 0.0) - jnp.log1p(jnp.exp(-jnp.abs(g)))
    rr = lax.broadcasted_iota(jnp.int32, (L, L), 0)
    cc = lax.broadcasted_iota(jnp.int32, (L, L), 1)
    causal = cc <= rr
    tri = causal.astype(BF16)
    lf_hi = lf.astype(BF16)
    lf_lo = (lf - lf_hi.astype(F32)).astype(BF16)
    bcum = (jnp.dot(tri, lf_hi, preferred_element_type=F32)
            + jnp.dot(tri, lf_lo, preferred_element_type=F32))
    bcum_t = bcum.T
    g_t = g.T

    for h in range(nh):
        bcol = bcum[:, nh + h:nh + h + 1]
        brow = bcum_t[nh + h:nh + h + 1, :]
        irow = g_t[h:h + 1, :]
        icol = g[:, h:h + 1]
        m_old = m_ref[h]
        c_old = c_ref[h]
        n_old = n_ref[h]

        qh = qk[:, h * dk:(h + 1) * dk] * (dk ** -0.5)
        kh = qk[:, M_QK_W + h * dk:M_QK_W + (h + 1) * dk]
        vh = v_ref[:, h * dv:(h + 1) * dv]
        qb = qh.astype(BF16)
        kb = kh.astype(BF16)

        logd = jnp.where(causal, bcol - brow + irow, -jnp.inf)
        inter = bcol + m_old
        m_loc = jnp.maximum(inter, jnp.max(logd, axis=-1, keepdims=True))
        w_intra = jnp.exp(logd - m_loc)
        w_inter = jnp.exp(inter - m_loc)
        s = lax.dot_general(qb, kb, (((1,), (1,)), ((), ())), preferred_element_type=F32) * w_intra
        num = (jnp.dot(s.astype(BF16), vh, preferred_element_type=F32)
               + w_inter * jnp.dot(qb, c_old.astype(BF16), preferred_element_type=F32))
        den = (jnp.sum(s, axis=-1, keepdims=True)
               + w_inter * jnp.sum(qh * n_old, axis=-1, keepdims=True))
        hh = num / jnp.maximum(jnp.abs(den), jnp.exp(-m_loc))

        b_end = bcol[L - 1:L, :]
        log_w = b_end - bcol + icol
        m_new = jnp.maximum(b_end + m_old, jnp.max(log_w, axis=0, keepdims=True))
        w_s = jnp.exp(log_w - m_new)
        decay = jnp.exp(b_end + m_old - m_new)
        kw = kh * w_s
        c_ref[h] = decay * c_old + lax.dot_general(
            kw.astype(BF16), vh, (((0,), (0,)), ((), ())), preferred_element_type=F32)
        n_ref[h] = decay * n_old + jnp.sum(kw, axis=0, keepdims=True)
        m_ref[h] = m_new

        yn = hh * lax.rsqrt(jnp.mean(hh * hh, axis=-1, keepdims=True) + EPS)
        yn = yn * ng_ref[0:1, h * dv:(h + 1) * dv]
        o_ref[:, h * dv:(h + 1) * dv] = (so_ref[:, h * dv:(h + 1) * dv].astype(F32) * yn).astype(o_ref.dtype)


def _mlstm(qk, vproj, v_col, gates, so, so_col, conv_w, conv_b, norm_g, bsz, seq):
    n = bsz * seq
    nc = seq // CHUNK
    L = CHUNK
    return pl.pallas_call(
        _mlstm_kernel,
        grid=(bsz, nc),
        in_specs=[
            pl.BlockSpec((L, 2 * M_QK_W), lambda b, c: (b * nc + c, 0)),
            pl.BlockSpec((L, M_V_W), lambda b, c: (b * nc + c, v_col)),
            pl.BlockSpec((L, LANES), lambda b, c: (b * nc + c, 0)),
            pl.BlockSpec((L, M_V_W), lambda b, c: (b * nc + c, so_col)),
            pl.BlockSpec((CONV_K, 2 * M_QK_W), lambda b, c: (0, 0)),
            pl.BlockSpec((8, 2 * M_QK_W), lambda b, c: (0, 0)),
            pl.BlockSpec((8, M_V_W), lambda b, c: (0, 0)),
        ],
        out_specs=pl.BlockSpec((L, M_V_W), lambda b, c: (b * nc + c, 0)),
        out_shape=jax.ShapeDtypeStruct((n, M_V_W), BF16),
        scratch_shapes=[
            pltpu.VMEM((8 + L, 2 * M_QK_W), F32),
            pltpu.VMEM((M_HEADS, M_DQK, M_DV), F32),
            pltpu.VMEM((M_HEADS, 1, M_DQK), F32),
            pltpu.VMEM((M_HEADS, 1, 1), F32),
        ],
        compiler_params=_cparams(("parallel", "arbitrary")),
        name="mlstm",
    )(qk, vproj, gates, so, conv_w.astype(F32), _aux_rows(conv_b), _aux_rows(norm_g))


def _dattn_kernel(lam_ref, q_ref, k_ref, v_ref, sg_ref, o_ref, vt_ref, acc_ref, *, seq, tq, tk, out_scale):
    qi = pl.program_id(2)
    hd = D_HD
    tv = 256

    @pl.when(qi == 0)
    def _():
        for c in range(seq // tv):
            blk = v_ref[c * tv:(c + 1) * tv, :].astype(F32)
            vt_ref[:, c * tv:(c + 1) * tv] = blk.T.astype(BF16)

    qt = q_ref[...].astype(F32).T
    row = lax.broadcasted_iota(jnp.int32, qt.shape, 0)
    qz = (jnp.where(row < hd, qt, 0.0).astype(BF16), jnp.where(row >= hd, qt, 0.0).astype(BF16))
    acc_ref[...] = jnp.zeros_like(acc_ref)

    def chunk(j, carry, mask):
        start = pl.multiple_of(j * tk, tk)
        kj = k_ref[pl.ds(start, tk), :]
        vtj = vt_ref[:, pl.ds(start, tk)]
        out = []
        for c in range(2):
            m, l = carry[2 * c], carry[2 * c + 1]
            s = jnp.dot(kj, qz[c], preferred_element_type=F32)
            if mask is not None:
                s = jnp.where(mask, s, NEG_BIG)
            m_new = jnp.maximum(m, jnp.max(s, axis=0, keepdims=True))
            p = jnp.exp2(s - m_new)
            alpha = jnp.exp2(m - m_new)
            l_new = alpha * l + jnp.sum(p, axis=0, keepdims=True)
            acc_ref[c] = alpha * acc_ref[c] + jnp.dot(vtj, p.astype(BF16), preferred_element_type=F32)
            out += [m_new, l_new]
        return tuple(out)

    init = (jnp.full((1, tq), NEG_BIG, F32), jnp.zeros((1, tq), F32)) * 2
    n_full = qi * (tq // tk)
    carry = lax.fori_loop(0, n_full, lambda j, c: chunk(j, c, None), init)
    kr = lax.broadcasted_iota(jnp.int32, (tk, tq), 0)
    qc = lax.broadcasted_iota(jnp.int32, (tk, tq), 1)
    for d in range(tq // tk):
        carry = chunk(n_full + d, carry, (kr + d * tk) <= qc)

    m1, l1, m2, l2 = carry
    lam = lam_ref[0]
    ot = acc_ref[0] / l1 - lam * (acc_ref[1] / l2)
    o = ot.T
    y = o * lax.rsqrt(jnp.mean(o * o, axis=-1, keepdims=True) + EPS)
    o_ref[...] = (y * (sg_ref[0:1, :] * out_scale)).astype(o_ref.dtype)


def _diff_attention(lam, qk, v, v_col0, subln_g, bsz, seq, out_scale, tq=512, tk=256):
    n = bsz * seq
    tq = min(tq, seq)
    tk = min(tk, tq)
    nq = seq // tq
    hw = 2 * D_HD
    return pl.pallas_call(
        functools.partial(_dattn_kernel, seq=seq, tq=tq, tk=tk, out_scale=out_scale),
        grid=(bsz, D_HEADS, nq),
        in_specs=[
            pl.BlockSpec(memory_space=pltpu.SMEM),
            pl.BlockSpec((tq, hw), lambda b, h, i: (b * nq + i, h)),
            pl.BlockSpec((seq, hw), lambda b, h, i: (b, D_HEADS + h)),
            pl.BlockSpec((seq, D_DV), lambda b, h, i: (b, v_col0 + h)),
            pl.BlockSpec((8, D_DV), lambda b, h, i: (0, 0)),
        ],
        out_specs=pl.BlockSpec((tq, D_DV), lambda b, h, i: (b * nq + i, h)),
        out_shape=jax.ShapeDtypeStruct((n, D_V_W), BF16),
        scratch_shapes=[
            pltpu.VMEM((D_DV, seq), BF16),
            pltpu.VMEM((2, D_DV, tq), F32),
        ],
        compiler_params=_cparams(("parallel", "parallel", "arbitrary")),
        name="diff_attn",
    )(lam.reshape(1).astype(F32), qk, qk, v, _aux_rows(subln_g))


def _xattn_kernel(q_ref, mk_ref, mv_ref, o_ref):
    for h in range(C_HEADS):
        q = q_ref[:, h * C_DQK:(h + 1) * C_DQK]
        k = mk_ref[:, h * C_DQK:(h + 1) * C_DQK]
        s = lax.dot_general(q, k, (((1,), (1,)), ((), ())), preferred_element_type=F32)
        m = jnp.max(s, axis=-1, keepdims=True)
        p = jnp.exp2(s - m)
        l = jnp.sum(p, axis=-1, keepdims=True)
        o = jnp.dot(p.astype(BF16), mv_ref[:, h * C_DV:(h + 1) * C_DV], preferred_element_type=F32)
        o_ref[:, h * C_DV:(h + 1) * C_DV] = (o / l).astype(o_ref.dtype)


def _cross_attention(q, mk, mv, bsz, seq, mem_len, tq=512):
    n = bsz * seq
    tq = min(tq, seq)
    nq = seq // tq
    return pl.pallas_call(
        _xattn_kernel,
        grid=(bsz, nq),
        in_specs=[
            pl.BlockSpec((tq, C_Q_W), lambda b, i: (b * nq + i, 0)),
            pl.BlockSpec((mem_len, C_Q_W), lambda b, i: (b, 0)),
            pl.BlockSpec((mem_len, C_V_W), lambda b, i: (b, 0)),
        ],
        out_specs=pl.BlockSpec((tq, C_V_W), lambda b, i: (b * nq + i, 0)),
        out_shape=jax.ShapeDtypeStruct((n, C_V_W), BF16),
        compiler_params=_cparams(("parallel", "parallel")),
        name="cross_attn",
    )(q, mk, mv)


def _merge_kernel(x_ref, ym_ref, yd_ref, yc_ref, gm_ref, gd_ref, gc_ref,
                  wm_ref, wd_ref, wc_ref, wo_ref, o_ref):
    merged = gm_ref[...].astype(F32) * jnp.dot(ym_ref[...], wm_ref[...], preferred_element_type=F32)
    merged = merged + gd_ref[...].astype(F32) * jnp.dot(yd_ref[...], wd_ref[...], preferred_element_type=F32)
    merged = merged + gc_ref[...].astype(F32) * jnp.dot(yc_ref[...], wc_ref[...], preferred_element_type=F32)
    o_ref[...] = x_ref[...] + jnp.dot(merged.astype(BF16), wo_ref[...], preferred_element_type=F32)


def _merge(x, ym, yd, yc, gates, g_col0, wm, wd, wc, wo, tm=256):
    n, d = x.shape
    tm = min(tm, n)
    row = lambda i: (i, 0)
    wspec = pl.BlockSpec((d, d), lambda i: (0, 0))
    return pl.pallas_call(
        _merge_kernel,
        grid=(n // tm,),
        in_specs=[
            pl.BlockSpec((tm, d), row),
            pl.BlockSpec((tm, d), row),
            pl.BlockSpec((tm, d), row),
            pl.BlockSpec((tm, d), row),
            pl.BlockSpec((tm, d), lambda i: (i, g_col0)),
            pl.BlockSpec((tm, d), lambda i: (i, g_col0 + 1)),
            pl.BlockSpec((tm, d), lambda i: (i, g_col0 + 2)),
            wspec, wspec, wspec, wspec,
        ],
        out_specs=pl.BlockSpec((tm, d), row),
        out_shape=jax.ShapeDtypeStruct((n, d), F32),
        compiler_params=_cparams(("parallel",)),
        name="merge",
    )(x, ym, yd, yc, gates, gates, gates, wm, wd, wc, wo)


def _mlp_kernel(x_ref, g_ref, wu_ref, wd_ref, o_ref, h_ref, acc_ref):
    f = pl.program_id(1)

    @pl.when(f == 0)
    def _():
        x = x_ref[...]
        y = x * lax.rsqrt(jnp.mean(x * x, axis=-1, keepdims=True) + EPS)
        h_ref[...] = (y * g_ref[...]).astype(BF16)
        acc_ref[...] = jnp.zeros_like(acc_ref)

    u = jnp.maximum(jnp.dot(h_ref[...], wu_ref[...], preferred_element_type=F32), 0.0)
    acc_ref[...] += jnp.dot((u * u).astype(BF16), wd_ref[...], preferred_element_type=F32)

    @pl.when(f == pl.num_programs(1) - 1)
    def _():
        o_ref[...] = x_ref[...] + acc_ref[...]


def _mlp(x, g, wu, wd, tm=1024, tf=512):
    n, d = x.shape
    dff = wu.shape[1]
    tm = min(tm, n)
    return pl.pallas_call(
        _mlp_kernel,
        grid=(n // tm, dff // tf),
        in_specs=[
            pl.BlockSpec((tm, d), lambda i, f: (i, 0)),
            pl.BlockSpec((1, d), lambda i, f: (0, 0)),
            pl.BlockSpec((d, tf), lambda i, f: (0, f)),
            pl.BlockSpec((tf, d), lambda i, f: (f, 0)),
        ],
        out_specs=pl.BlockSpec((tm, d), lambda i, f: (i, 0)),
        out_shape=jax.ShapeDtypeStruct((n, d), F32),
        scratch_shapes=[pltpu.VMEM((tm, d), BF16), pltpu.VMEM((tm, d), F32)],
        compiler_params=_cparams(("parallel", "arbitrary")),
        name="mlp",
    )(x, g.reshape(1, d).astype(F32), wu, wd)


def _lambda_init(layer):
    return 0.8 - 0.6 * math.exp(-0.3 * layer)


def _layer(l, x2, mem2, bsz, seq, mem_len, p):
    split = [0]
    for w in (M_QK_W, M_QK_W, M_V_W, M_HEADS, M_HEADS, M_V_W, D_Q_W, D_Q_W, D_V_W, C_Q_W):
        split.append(split[-1] + w)
    w_in = p['w_in'][l]
    cols = lambda a, b: w_in[:, split[a]:split[b]]
    g_mix = p['norm_mix_g'][l]

    mqk = _norm_matmul(x2, g_mix, cols(0, 2).astype(BF16), jnp.zeros((8, 2 * M_QK_W), F32),
                       _ep_identity, BF16, 1024, 1024, "proj_mqk")
    w_if = jnp.zeros((D_MODEL, LANES), F32).at[:, :2 * M_HEADS].set(cols(3, 5)).astype(BF16)
    b_if = jnp.zeros((LANES,), F32).at[:2 * M_HEADS].set(jnp.concatenate([p['b_igate'][l], p['b_fgate'][l]]))
    gates_if = _norm_matmul(x2, g_mix, w_if, _aux_rows(b_if), _ep_bias, F32, 1024, LANES, "proj_if")
    w_v = jnp.concatenate([cols(2, 3), cols(8, 9)], axis=1).astype(BF16)
    vproj = _norm_matmul(x2, g_mix, w_v, jnp.zeros((8, w_v.shape[1]), F32),
                         _ep_identity, BF16, 1024, 1024, "proj_v")
    w_sg = jnp.concatenate([cols(5, 6), p['w_gate'][l]], axis=1).astype(BF16)
    b_sg = jnp.concatenate([jnp.zeros((M_V_W,), F32), p['b_gate'][l].astype(F32)])
    sgates = _norm_matmul(x2, g_mix, w_sg, _aux_rows(b_sg), _ep_sigmoid_bias, BF16, 1024, 1024, "proj_gates")
    w_dqk = cols(6, 8).astype(BF16)
    q_gain = jnp.tile(p['dq_norm_g'][l].astype(F32), D_Q_W // D_HD) * (D_HD ** -0.5 * LOG2E)
    k_gain = jnp.tile(p['dk_norm_g'][l].astype(F32), D_Q_W // D_HD)
    dqk = _norm_matmul(x2, g_mix, w_dqk, _aux_rows(jnp.concatenate([q_gain, k_gain])),
                       functools.partial(_ep_group_norm, D_HD), BF16, 1024, 1024, "proj_dqk")
    cq_gain = jnp.tile(p['cq_norm_g'][l].astype(F32), C_HEADS) * (C_DQK ** -0.5 * LOG2E)
    cq = _norm_matmul(x2, g_mix, cols(9, 10).astype(BF16), _aux_rows(cq_gain),
                      functools.partial(_ep_group_norm, C_DQK), BF16, 1024, C_Q_W, "proj_cq")
    w_kv = p['w_mem_kv'][l]
    ck_gain = jnp.tile(p['ck_norm_g'][l].astype(F32), C_HEADS)
    mk = _norm_matmul(mem2, p['mem_norm_g'][l], w_kv[:, :C_Q_W].astype(BF16), _aux_rows(ck_gain),
                      functools.partial(_ep_group_norm, C_DQK), BF16, 1024, C_Q_W, "proj_mk")
    mv = _norm_matmul(mem2, p['mem_norm_g'][l], w_kv[:, C_Q_W:].astype(BF16), jnp.zeros((8, C_V_W), F32),
                      _ep_identity, BF16, 1024, C_V_W, "proj_mv")

    y_m = _mlstm(mqk, vproj, 0, gates_if, sgates, 0, p['conv_w'][l], p['conv_b'][l], p['m_norm_g'][l], bsz, seq)

    lam_i = _lambda_init(l)
    lam = (jnp.exp(jnp.sum(p['lam_q1'][l].astype(F32) * p['lam_k1'][l].astype(F32)))
           - jnp.exp(jnp.sum(p['lam_q2'][l].astype(F32) * p['lam_k2'][l].astype(F32))) + lam_i)
    y_d = _diff_attention(lam, dqk, vproj, M_V_W // D_DV, p['subln_g'][l], bsz, seq, 1.0 - lam_i)

    y_c = _cross_attention(cq, mk, mv, bsz, seq, mem_len)

    x2 = _merge(x2, y_m, y_d, y_c, sgates, 1,
                p['w_proj_m'][l].astype(BF16), p['w_proj_d'][l].astype(BF16),
                p['w_proj_c'][l].astype(BF16), p['w_out'][l].astype(BF16))
    return _mlp(x2, p['norm_mlp_g'][l], p['w_up'][l].astype(BF16), p['w_down'][l].astype(BF16))


def kernel(x, mem, norm_mix_g, w_in, b_igate, b_fgate, conv_w, conv_b, m_norm_g, dq_norm_g, dk_norm_g, lam_q1, lam_k1, lam_q2, lam_k2, subln_g, cq_norm_g, ck_norm_g, mem_norm_g, w_mem_kv, w_gate, b_gate, w_proj_m, w_proj_d, w_proj_c, w_out, norm_mlp_g, w_up, w_down):
    p = dict(norm_mix_g=norm_mix_g, w_in=w_in, b_igate=b_igate, b_fgate=b_fgate, conv_w=conv_w, conv_b=conv_b,
             m_norm_g=m_norm_g, dq_norm_g=dq_norm_g, dk_norm_g=dk_norm_g, lam_q1=lam_q1, lam_k1=lam_k1,
             lam_q2=lam_q2, lam_k2=lam_k2, subln_g=subln_g, cq_norm_g=cq_norm_g, ck_norm_g=ck_norm_g,
             mem_norm_g=mem_norm_g, w_mem_kv=w_mem_kv, w_gate=w_gate, b_gate=b_gate, w_proj_m=w_proj_m,
             w_proj_d=w_proj_d, w_proj_c=w_proj_c, w_out=w_out, norm_mlp_g=norm_mlp_g, w_up=w_up, w_down=w_down)
    bsz, seq, d = x.shape
    mem_len = mem.shape[1]
    x2 = x.reshape(bsz * seq, d)
    mem2 = mem.reshape(bsz * mem_len, d)
    for l in range(w_in.shape[0]):
        x2 = _layer(l, x2, mem2, bsz, seq, mem_len, p)
    return x2.reshape(bsz, seq, d)
```

```python
import functools
import math

import jax
import jax.numpy as jnp
from jax import lax
from jax.experimental import pallas as pl
from jax.experimental.pallas import tpu as pltpu

F32 = jnp.float32
BF16 = jnp.bfloat16

EPS = 1e-6
LOG2E = math.log2(math.e)
NEG_BIG = -1e30

D_MODEL = 1024
MEM_LEN = 256
M_HEADS, M_DQK, M_DV = 4, 128, 256
CHUNK = 128
CONV_K = 4
D_HEADS, D_HD = 8, 64
D_DV = 2 * D_HD
C_HEADS, C_DQK, C_DV = 4, 128, 256
D_FF = 4 * D_MODEL
LANES = 128

M_QK_W = M_HEADS * M_DQK
M_V_W = M_HEADS * M_DV
D_Q_W = D_HEADS * 2 * D_HD
D_V_W = D_HEADS * D_DV
C_Q_W = C_HEADS * C_DQK
C_V_W = C_HEADS * C_DV

VMEM_LIMIT = 56 * 1024 * 1024

DATTN_TQ = 1024
DATTN_TK_FAST = 512
DATTN_TK_STABILISED = 256
MAX_UNSTABILISED_SCORE = 60.0


def _cparams(sem):
    return pltpu.CompilerParams(dimension_semantics=sem, vmem_limit_bytes=VMEM_LIMIT)


def _ep_identity(acc, aux_ref):
    return acc


def _ep_sigmoid_bias(acc, aux_ref):
    return jax.nn.sigmoid(acc + aux_ref[0:1, :])


def _ep_bias(acc, aux_ref):
    return acc + aux_ref[0:1, :]


def _ep_group_norm(group, acc, aux_ref):
    tn = acc.shape[1]
    r = lax.broadcasted_iota(jnp.int32, (LANES, LANES), 0) // group
    c = lax.broadcasted_iota(jnp.int32, (LANES, LANES), 1) // group
    gmat = (r == c).astype(BF16)
    outs = []
    for s in range(tn // LANES):
        a = acc[:, s * LANES:(s + 1) * LANES]
        ss = jnp.dot((a * a).astype(BF16), gmat, preferred_element_type=F32)
        outs.append(a * lax.rsqrt(ss * (1.0 / group) + EPS))
    y = jnp.concatenate(outs, axis=1) if len(outs) > 1 else outs[0]
    return y * aux_ref[0:1, :]


def _norm_matmul_kernel(x_ref, g_ref, w_ref, aux_ref, o_ref, h_ref, *, epilogue):
    @pl.when(pl.program_id(1) == 0)
    def _():
        x = x_ref[...].astype(F32)
        y = x * lax.rsqrt(jnp.mean(x * x, axis=-1, keepdims=True) + EPS)
        h_ref[...] = (y * g_ref[...]).astype(BF16)

    acc = jnp.dot(h_ref[...], w_ref[...], preferred_element_type=F32)
    o_ref[...] = epilogue(acc, aux_ref).astype(o_ref.dtype)


def _norm_matmul(x, g, w, aux, epilogue, out_dtype, tm, tn, name):
    n, d = x.shape
    cols = w.shape[1]
    tm = min(tm, n)
    tn = min(tn, cols)
    assert n % tm == 0 and cols % tn == 0
    return pl.pallas_call(
        functools.partial(_norm_matmul_kernel, epilogue=epilogue),
        grid=(n // tm, cols // tn),
        in_specs=[
            pl.BlockSpec((tm, d), lambda i, j: (i, 0)),
            pl.BlockSpec((1, d), lambda i, j: (0, 0)),
            pl.BlockSpec((d, tn), lambda i, j: (0, j)),
            pl.BlockSpec((8, tn), lambda i, j: (0, j)),
        ],
        out_specs=pl.BlockSpec((tm, tn), lambda i, j: (i, j)),
        out_shape=jax.ShapeDtypeStruct((n, cols), out_dtype),
        scratch_shapes=[pltpu.VMEM((tm, d), BF16)],
        compiler_params=_cparams(("parallel", "arbitrary")),
        name=name,
    )(x, g.reshape(1, d).astype(F32), w, aux)


def _aux_rows(row):
    return jnp.zeros((8, row.shape[0]), F32).at[0].set(row.astype(F32))


def _mlstm_kernel(qk_ref, v_ref, gate_ref, so_ref, cw_ref, cb_ref, ng_ref, o_ref,
                  xpad_ref, c_ref, n_ref, m_ref):
    L = CHUNK
    dk, dv, nh = M_DQK, M_DV, M_HEADS

    @pl.when(pl.program_id(1) == 0)
    def _():
        xpad_ref[0:8, :] = jnp.zeros((8, 2 * M_QK_W), F32)
        c_ref[...] = jnp.zeros_like(c_ref)
        n_ref[...] = jnp.zeros_like(n_ref)
        m_ref[...] = jnp.zeros_like(m_ref)

    x = qk_ref[...].astype(F32)
    xpad_ref[8:8 + L, :] = x
    y = cb_ref[0:1, :] + cw_ref[CONV_K - 1:CONV_K, :] * x
    for j in range(CONV_K - 1):
        y = y + cw_ref[j:j + 1, :] * xpad_ref[pl.ds(8 - (CONV_K - 1) + j, L), :]
    xpad_ref[0:8, :] = x[L - 8:L, :]
    qk = y * jax.nn.sigmoid(y)

    g = gate_ref[...]
    lf = jnp.minimum(g, 0.0) - jnp.log1p(jnp.exp(-jnp.abs(g)))
    rr = lax.broadcasted_iota(jnp.int32, (L, L), 0)
    cc = lax.broadcasted_iota(jnp.int32, (L, L), 1)
    causal = cc <= rr
    tri = causal.astype(BF16)
    lf_hi = lf.astype(BF16)
    lf_lo = (lf - lf_hi.astype(F32)).astype(BF16)
    bcum = (jnp.dot(tri, lf_hi, preferred_element_type=F32)
            + jnp.dot(tri, lf_lo, preferred_element_type=F32))
    bcum_t = bcum.T
    g_t = g.T

    for h in range(nh):
        bcol = bcum[:, nh + h:nh + h + 1]
        brow = bcum_t[nh + h:nh + h + 1, :]
        irow = g_t[h:h + 1, :]
        icol = g[:, h:h + 1]
        m_old = m_ref[h]
        c_old = c_ref[h]
        n_old = n_ref[h]

        qh = qk[:, h * dk:(h + 1) * dk] * (dk ** -0.5)
        kh = qk[:, M_QK_W + h * dk:M_QK_W + (h + 1) * dk]
        vh = v_ref[:, h * dv:(h + 1) * dv]
        qb = qh.astype(BF16)
        kb = kh.astype(BF16)

        logd = jnp.where(causal, bcol - brow + irow, -jnp.inf)
        inter = bcol + m_old
        m_loc = jnp.maximum(inter, jnp.max(logd, axis=-1, keepdims=True))
        w_intra = jnp.exp(logd - m_loc)
        w_inter = jnp.exp(inter - m_loc)
        s = lax.dot_general(qb, kb, (((1,), (1,)), ((), ())), preferred_element_type=F32) * w_intra
        num = (jnp.dot(s.astype(BF16), vh, preferred_element_type=F32)
               + w_inter * jnp.dot(qb, c_old.astype(BF16), preferred_element_type=F32))
        den = (jnp.sum(s, axis=-1, keepdims=True)
               + w_inter * jnp.sum(qh * n_old, axis=-1, keepdims=True))
        hh = num / jnp.maximum(jnp.abs(den), jnp.exp(-m_loc))

        b_end = bcol[L - 1:L, :]
        log_w = b_end - bcol + icol
        m_new = jnp.maximum(b_end + m_old, jnp.max(log_w, axis=0, keepdims=True))
        w_s = jnp.exp(log_w - m_new)
        decay = jnp.exp(b_end + m_old - m_new)
        kw = kh * w_s
        c_ref[h] = decay * c_old + lax.dot_general(
            kw.astype(BF16), vh, (((0,), (0,)), ((), ())), preferred_element_type=F32)
        n_ref[h] = decay * n_old + jnp.sum(kw, axis=0, keepdims=True)
        m_ref[h] = m_new

        yn = hh * lax.rsqrt(jnp.mean(hh * hh, axis=-1, keepdims=True) + EPS)
        yn = yn * ng_ref[0:1, h * dv:(h + 1) * dv]
        o_ref[:, h * dv:(h + 1) * dv] = (so_ref[:, h * dv:(h + 1) * dv].astype(F32) * yn).astype(o_ref.dtype)


def _mlstm(qk, vproj, v_col, gates, so, so_col, conv_w, conv_b, norm_g, bsz, seq):
    n = bsz * seq
    nc = seq // CHUNK
    L = CHUNK
    return pl.pallas_call(
        _mlstm_kernel,
        grid=(bsz, nc),
        in_specs=[
            pl.BlockSpec((L, 2 * M_QK_W), lambda b, c: (b * nc + c, 0)),
            pl.BlockSpec((L, M_V_W), lambda b, c: (b * nc + c, v_col)),
            pl.BlockSpec((L, LANES), lambda b, c: (b * nc + c, 0)),
            pl.BlockSpec((L, M_V_W), lambda b, c: (b * nc + c, so_col)),
            pl.BlockSpec((CONV_K, 2 * M_QK_W), lambda b, c: (0, 0)),
            pl.BlockSpec((8, 2 * M_QK_W), lambda b, c: (0, 0)),
            pl.BlockSpec((8, M_V_W), lambda b, c: (0, 0)),
        ],
        out_specs=pl.BlockSpec((L, M_V_W), lambda b, c: (b * nc + c, 0)),
        out_shape=jax.ShapeDtypeStruct((n, M_V_W), BF16),
        scratch_shapes=[
            pltpu.VMEM((8 + L, 2 * M_QK_W), F32),
            pltpu.VMEM((M_HEADS, M_DQK, M_DV), F32),
            pltpu.VMEM((M_HEADS, 1, M_DQK), F32),
            pltpu.VMEM((M_HEADS, 1, 1), F32),
        ],
        compiler_params=_cparams(("parallel", "arbitrary")),
        name="mlstm",
    )(qk, vproj, gates, so, conv_w.astype(F32), _aux_rows(conv_b), _aux_rows(norm_g))


def _dattn_kernel(lam_ref, q_ref, k_ref, v_ref, sg_ref, o_ref, vt_ref, acc_ref, *,
                  seq, tq, tk, out_scale, stabilised):
    qi = pl.program_id(2)
    hd = D_HD
    tv = 256

    @pl.when(qi == 0)
    def _():
        for c in range(seq // tv):
            blk = v_ref[c * tv:(c + 1) * tv, :].astype(F32)
            vt_ref[:, c * tv:(c + 1) * tv] = blk.T.astype(BF16)

    qt = q_ref[...].astype(F32).T
    row = lax.broadcasted_iota(jnp.int32, qt.shape, 0)
    qz = (jnp.where(row < hd, qt, 0.0).astype(BF16), jnp.where(row >= hd, qt, 0.0).astype(BF16))
    acc_ref[...] = jnp.zeros_like(acc_ref)

    def chunk(j, carry, lo):
        start = pl.multiple_of(j * tk, tk)
        kj = k_ref[pl.ds(start, tk), :]
        vtj = vt_ref[:, pl.ds(start, tk)]
        q0 = lo or 0
        out = []
        for c in range(2):
            m, l = carry[2 * c][:, q0:], carry[2 * c + 1][:, q0:]
            s = jnp.dot(kj, qz[c][:, q0:], preferred_element_type=F32)
            if lo is not None:
                kr = lax.broadcasted_iota(jnp.int32, s.shape, 0)
                qc = lax.broadcasted_iota(jnp.int32, s.shape, 1)
                s = jnp.where(kr <= qc, s, NEG_BIG)
            if stabilised:
                m_new = jnp.maximum(m, jnp.max(s, axis=0, keepdims=True))
                p = jnp.exp2(s - m_new)
                alpha = jnp.exp2(m - m_new)
                l_new = alpha * l + jnp.sum(p, axis=0, keepdims=True)
                acc_ref[c, :, q0:] = alpha * acc_ref[c, :, q0:] + jnp.dot(
                    vtj, p.astype(BF16), preferred_element_type=F32)
            else:
                m_new = m
                p = jnp.exp2(s)
                l_new = l + jnp.sum(p, axis=0, keepdims=True)
                acc_ref[c, :, q0:] += jnp.dot(vtj, p.astype(BF16), preferred_element_type=F32)
            if q0:
                m_new = jnp.concatenate([carry[2 * c][:, :q0], m_new], axis=1)
                l_new = jnp.concatenate([carry[2 * c + 1][:, :q0], l_new], axis=1)
            out += [m_new, l_new]
        return tuple(out)

    init = (jnp.full((1, tq), NEG_BIG, F32), jnp.zeros((1, tq), F32)) * 2
    n_full = qi * (tq // tk)
    carry = lax.fori_loop(0, n_full, lambda j, c: chunk(j, c, None), init)
    for d in range(tq // tk):
        carry = chunk(n_full + d, carry, d * tk)

    m1, l1, m2, l2 = carry
    lam = lam_ref[0]
    ot = acc_ref[0] / l1 - lam * (acc_ref[1] / l2)
    o = ot.T
    y = o * lax.rsqrt(jnp.mean(o * o, axis=-1, keepdims=True) + EPS)
    o_ref[...] = (y * (sg_ref[0:1, :] * out_scale)).astype(o_ref.dtype)


def _diff_attention(lam, qk, v, v_col0, subln_g, bsz, seq, out_scale, stabilised, tq, tk):
    n = bsz * seq
    tq = min(tq, seq)
    tk = min(tk, tq)
    nq = seq // tq
    hw = 2 * D_HD
    return pl.pallas_call(
        functools.partial(_dattn_kernel, seq=seq, tq=tq, tk=tk, out_scale=out_scale, stabilised=stabilised),
        grid=(bsz, D_HEADS, nq),
        in_specs=[
            pl.BlockSpec(memory_space=pltpu.SMEM),
            pl.BlockSpec((tq, hw), lambda b, h, i: (b * nq + i, h)),
            pl.BlockSpec((seq, hw), lambda b, h, i: (b, D_HEADS + h)),
            pl.BlockSpec((seq, D_DV), lambda b, h, i: (b, v_col0 + h)),
            pl.BlockSpec((8, D_DV), lambda b, h, i: (0, 0)),
        ],
        out_specs=pl.BlockSpec((tq, D_DV), lambda b, h, i: (b * nq + i, h)),
        out_shape=jax.ShapeDtypeStruct((n, D_V_W), BF16),
        scratch_shapes=[
            pltpu.VMEM((D_DV, seq), BF16),
            pltpu.VMEM((2, D_DV, tq), F32),
        ],
        compiler_params=_cparams(("parallel", "parallel", "arbitrary")),
        name="diff_attn_stabilised" if stabilised else "diff_attn",
    )(lam.reshape(1).astype(F32), qk, qk, v, _aux_rows(subln_g))


def _xattn_kernel(q_ref, mk_ref, mv_ref, o_ref):
    for h in range(C_HEADS):
        q = q_ref[:, h * C_DQK:(h + 1) * C_DQK]
        k = mk_ref[:, h * C_DQK:(h + 1) * C_DQK]
        s = lax.dot_general(q, k, (((1,), (1,)), ((), ())), preferred_element_type=F32)
        m = jnp.max(s, axis=-1, keepdims=True)
        p = jnp.exp2(s - m)
        l = jnp.sum(p, axis=-1, keepdims=True)
        o = jnp.dot(p.astype(BF16), mv_ref[:, h * C_DV:(h + 1) * C_DV], preferred_element_type=F32)
        o_ref[:, h * C_DV:(h + 1) * C_DV] = (o / l).astype(o_ref.dtype)


def _cross_attention(q, mk, mv, bsz, seq, mem_len, tq=512):
    n = bsz * seq
    tq = min(tq, seq)
    nq = seq // tq
    return pl.pallas_call(
        _xattn_kernel,
        grid=(bsz, nq),
        in_specs=[
            pl.BlockSpec((tq, C_Q_W), lambda b, i: (b * nq + i, 0)),
            pl.BlockSpec((mem_len, C_Q_W), lambda b, i: (b, 0)),
            pl.BlockSpec((mem_len, C_V_W), lambda b, i: (b, 0)),
        ],
        out_specs=pl.BlockSpec((tq, C_V_W), lambda b, i: (b * nq + i, 0)),
        out_shape=jax.ShapeDtypeStruct((n, C_V_W), BF16),
        compiler_params=_cparams(("parallel", "parallel")),
        name="cross_attn",
    )(q, mk, mv)


def _merge_kernel(x_ref, ym_ref, yd_ref, yc_ref, gm_ref, gd_ref, gc_ref,
                  wm_ref, wd_ref, wc_ref, wo_ref, o_ref):
    merged = gm_ref[...].astype(F32) * jnp.dot(ym_ref[...], wm_ref[...], preferred_element_type=F32)
    merged = merged + gd_ref[...].astype(F32) * jnp.dot(yd_ref[...], wd_ref[...], preferred_element_type=F32)
    merged = merged + gc_ref[...].astype(F32) * jnp.dot(yc_ref[...], wc_ref[...], preferred_element_type=F32)
    o_ref[...] = x_ref[...] + jnp.dot(merged.astype(BF16), wo_ref[...], preferred_element_type=F32)


def _merge(x, ym, yd, yc, gates, g_col0, wm, wd, wc, wo, tm=256):
    n, d = x.shape
    tm = min(tm, n)
    row = lambda i: (i, 0)
    wspec = pl.BlockSpec((d, d), lambda i: (0, 0))
    return pl.pallas_call(
        _merge_kernel,
        grid=(n // tm,),
        in_specs=[
            pl.BlockSpec((tm, d), row),
            pl.BlockSpec((tm, d), row),
            pl.BlockSpec((tm, d), row),
            pl.BlockSpec((tm, d), row),
            pl.BlockSpec((tm, d), lambda i: (i, g_col0)),
            pl.BlockSpec((tm, d), lambda i: (i, g_col0 + 1)),
            pl.BlockSpec((tm, d), lambda i: (i, g_col0 + 2)),
            wspec, wspec, wspec, wspec,
        ],
        out_specs=pl.BlockSpec((tm, d), row),
        out_shape=jax.ShapeDtypeStruct((n, d), F32),
        compiler_params=_cparams(("parallel",)),
        name="merge",
    )(x, ym, yd, yc, gates, gates, gates, wm, wd, wc, wo)


def _mlp_kernel(x_ref, g_ref, wu_ref, wd_ref, o_ref, h_ref, acc_ref):
    f = pl.program_id(1)

    @pl.when(f == 0)
    def _():
        x = x_ref[...]
        y = x * lax.rsqrt(jnp.mean(x * x, axis=-1, keepdims=True) + EPS)
        h_ref[...] = (y * g_ref[...]).astype(BF16)
        acc_ref[...] = jnp.zeros_like(acc_ref)

    u = jnp.maximum(jnp.dot(h_ref[...], wu_ref[...], preferred_element_type=F32), 0.0)
    acc_ref[...] += jnp.dot((u * u).astype(BF16), wd_ref[...], preferred_element_type=F32)

    @pl.when(f == pl.num_programs(1) - 1)
    def _():
        o_ref[...] = x_ref[...] + acc_ref[...]


def _mlp(x, g, wu, wd, tm=1024, tf=512):
    n, d = x.shape
    dff = wu.shape[1]
    tm = min(tm, n)
    return pl.pallas_call(
        _mlp_kernel,
        grid=(n // tm, dff // tf),
        in_specs=[
            pl.BlockSpec((tm, d), lambda i, f: (i, 0)),
            pl.BlockSpec((1, d), lambda i, f: (0, 0)),
            pl.BlockSpec((d, tf), lambda i, f: (0, f)),
            pl.BlockSpec((tf, d), lambda i, f: (f, 0)),
        ],
        out_specs=pl.BlockSpec((tm, d), lambda i, f: (i, 0)),
        out_shape=jax.ShapeDtypeStruct((n, d), F32),
        scratch_shapes=[pltpu.VMEM((tm, d), BF16), pltpu.VMEM((tm, d), F32)],
        compiler_params=_cparams(("parallel", "arbitrary")),
        name="mlp",
    )(x, g.reshape(1, d).astype(F32), wu, wd)


def _lambda_init(layer):
    return 0.8 - 0.6 * math.exp(-0.3 * layer)


def _layer(l, x2, mem2, bsz, seq, mem_len, p):
    split = [0]
    for w in (M_QK_W, M_QK_W, M_V_W, M_HEADS, M_HEADS, M_V_W, D_Q_W, D_Q_W, D_V_W, C_Q_W):
        split.append(split[-1] + w)
    w_in = p['w_in'][l]
    cols = lambda a, b: w_in[:, split[a]:split[b]]
    g_mix = p['norm_mix_g'][l]

    mqk = _norm_matmul(x2, g_mix, cols(0, 2).astype(BF16), jnp.zeros((8, 2 * M_QK_W), F32),
                       _ep_identity, BF16, 1024, 1024, "proj_mqk")
    w_if = jnp.zeros((D_MODEL, LANES), F32).at[:, :2 * M_HEADS].set(cols(3, 5)).astype(BF16)
    b_if = jnp.zeros((LANES,), F32).at[:2 * M_HEADS].set(jnp.concatenate([p['b_igate'][l], p['b_fgate'][l]]))
    gates_if = _norm_matmul(x2, g_mix, w_if, _aux_rows(b_if), _ep_bias, F32, 1024, LANES, "proj_if")
    w_v = jnp.concatenate([cols(2, 3), cols(8, 9)], axis=1).astype(BF16)
    vproj = _norm_matmul(x2, g_mix, w_v, jnp.zeros((8, w_v.shape[1]), F32),
                         _ep_identity, BF16, 1024, 1024, "proj_v")
    w_sg = jnp.concatenate([cols(5, 6), p['w_gate'][l]], axis=1).astype(BF16)
    b_sg = jnp.concatenate([jnp.zeros((M_V_W,), F32), p['b_gate'][l].astype(F32)])
    sgates = _norm_matmul(x2, g_mix, w_sg, _aux_rows(b_sg), _ep_sigmoid_bias, BF16, 1024, 1024, "proj_gates")
    w_dqk = cols(6, 8).astype(BF16)
    q_gain = jnp.tile(p['dq_norm_g'][l].astype(F32), D_Q_W // D_HD) * (D_HD ** -0.5 * LOG2E)
    k_gain = jnp.tile(p['dk_norm_g'][l].astype(F32), D_Q_W // D_HD)
    dqk = _norm_matmul(x2, g_mix, w_dqk, _aux_rows(jnp.concatenate([q_gain, k_gain])),
                       functools.partial(_ep_group_norm, D_HD), BF16, 1024, 1024, "proj_dqk")
    cq_gain = jnp.tile(p['cq_norm_g'][l].astype(F32), C_HEADS) * (C_DQK ** -0.5 * LOG2E)
    cq = _norm_matmul(x2, g_mix, cols(9, 10).astype(BF16), _aux_rows(cq_gain),
                      functools.partial(_ep_group_norm, C_DQK), BF16, 1024, C_Q_W, "proj_cq")
    w_kv = p['w_mem_kv'][l]
    ck_gain = jnp.tile(p['ck_norm_g'][l].astype(F32), C_HEADS)
    mk = _norm_matmul(mem2, p['mem_norm_g'][l], w_kv[:, :C_Q_W].astype(BF16), _aux_rows(ck_gain),
                      functools.partial(_ep_group_norm, C_DQK), BF16, 1024, C_Q_W, "proj_mk")
    mv = _norm_matmul(mem2, p['mem_norm_g'][l], w_kv[:, C_Q_W:].astype(BF16), jnp.zeros((8, C_V_W), F32),
                      _ep_identity, BF16, 1024, C_V_W, "proj_mv")

    y_m = _mlstm(mqk, vproj, 0, gates_if, sgates, 0, p['conv_w'][l], p['conv_b'][l], p['m_norm_g'][l], bsz, seq)

    lam_i = _lambda_init(l)
    lam = (jnp.exp(jnp.sum(p['lam_q1'][l].astype(F32) * p['lam_k1'][l].astype(F32)))
           - jnp.exp(jnp.sum(p['lam_q2'][l].astype(F32) * p['lam_k2'][l].astype(F32))) + lam_i)
    score_bound = 1.02 * D_HD * jnp.max(jnp.abs(q_gain)) * jnp.max(jnp.abs(k_gain))
    dattn = functools.partial(_diff_attention, lam, dqk, vproj, M_V_W // D_DV, p['subln_g'][l], bsz, seq,
                              1.0 - lam_i)
    y_d = lax.cond(score_bound <= MAX_UNSTABILISED_SCORE,
                   lambda: dattn(False, DATTN_TQ, DATTN_TK_FAST),
                   lambda: dattn(True, DATTN_TQ, DATTN_TK_STABILISED))

    y_c = _cross_attention(cq, mk, mv, bsz, seq, mem_len)

    x2 = _merge(x2, y_m, y_d, y_c, sgates, 1,
                p['w_proj_m'][l].astype(BF16), p['w_proj_d'][l].astype(BF16),
                p['w_proj_c'][l].astype(BF16), p['w_out'][l].astype(BF16))
    return _mlp(x2, p['norm_mlp_g'][l], p['w_up'][l].astype(BF16), p['w_down'][l].astype(BF16))


def kernel(x, mem, norm_mix_g, w_in, b_igate, b_fgate, conv_w, conv_b, m_norm_g, dq_norm_g, dk_norm_g, lam_q1, lam_k1, lam_q2, lam_k2, subln_g, cq_norm_g, ck_norm_g, mem_norm_g, w_mem_kv, w_gate, b_gate, w_proj_m, w_proj_d, w_proj_c, w_out, norm_mlp_g, w_up, w_down):
    p = dict(norm_mix_g=norm_mix_g, w_in=w_in, b_igate=b_igate, b_fgate=b_fgate, conv_w=conv_w, conv_b=conv_b,
             m_norm_g=m_norm_g, dq_norm_g=dq_norm_g, dk_norm_g=dk_norm_g, lam_q1=lam_q1, lam_k1=lam_k1,
             lam_q2=lam_q2, lam_k2=lam_k2, subln_g=subln_g, cq_norm_g=cq_norm_g, ck_norm_g=ck_norm_g,
             mem_norm_g=mem_norm_g, w_mem_kv=w_mem_kv, w_gate=w_gate, b_gate=b_gate, w_proj_m=w_proj_m,
             w_proj_d=w_proj_d, w_proj_c=w_proj_c, w_out=w_out, norm_mlp_g=norm_mlp_g, w_up=w_up, w_down=w_down)
    bsz, seq, d = x.shape
    mem_len = mem.shape[1]
    x2 = x.reshape(bsz * seq, d)
    mem2 = mem.reshape(bsz * mem_len, d)
    for l in range(w_in.shape[0]):
        x2 = _layer(l, x2, mem2, bsz, seq, mem_len, p)
    return x2.reshape(bsz, seq, d)
```

```python
import functools
import math

import jax
import jax.numpy as jnp
from jax import lax
from jax.experimental import pallas as pl
from jax.experimental.pallas import tpu as pltpu

F32 = jnp.float32
BF16 = jnp.bfloat16

EPS = 1e-6
LOG2E = math.log2(math.e)
NEG_BIG = -1e30

D_MODEL = 1024
MEM_LEN = 256
M_HEADS, M_DQK, M_DV = 4, 128, 256
CHUNK = 128
CONV_K = 4
D_HEADS, D_HD = 8, 64
D_DV = 2 * D_HD
C_HEADS, C_DQK, C_DV = 4, 128, 256
D_FF = 4 * D_MODEL
LANES = 128

M_QK_W = M_HEADS * M_DQK
M_V_W = M_HEADS * M_DV
D_Q_W = D_HEADS * 2 * D_HD
D_V_W = D_HEADS * D_DV
C_Q_W = C_HEADS * C_DQK
C_V_W = C_HEADS * C_DV

VMEM_LIMIT = 56 * 1024 * 1024

DATTN_TQ = 1024
DATTN_TK_FAST = 512
DATTN_TK_STABILISED = 256
MAX_UNSTABILISED_SCORE = 60.0


def _cparams(sem):
    return pltpu.CompilerParams(dimension_semantics=sem, vmem_limit_bytes=VMEM_LIMIT)


def _ep_identity(acc, aux_ref):
    return acc


def _ep_sigmoid_bias(acc, aux_ref):
    return jax.nn.sigmoid(acc + aux_ref[0:1, :])


def _ep_bias(acc, aux_ref):
    return acc + aux_ref[0:1, :]


def _ep_group_norm(group, acc, aux_ref):
    tn = acc.shape[1]
    r = lax.broadcasted_iota(jnp.int32, (LANES, LANES), 0) // group
    c = lax.broadcasted_iota(jnp.int32, (LANES, LANES), 1) // group
    gmat = (r == c).astype(BF16)
    outs = []
    for s in range(tn // LANES):
        a = acc[:, s * LANES:(s + 1) * LANES]
        ss = jnp.dot((a * a).astype(BF16), gmat, preferred_element_type=F32)
        outs.append(a * lax.rsqrt(ss * (1.0 / group) + EPS))
    y = jnp.concatenate(outs, axis=1) if len(outs) > 1 else outs[0]
    return y * aux_ref[0:1, :]


def _norm_matmul_kernel(x_ref, g_ref, w_ref, aux_ref, o_ref, h_ref, *, epilogue):
    @pl.when(pl.program_id(1) == 0)
    def _():
        x = x_ref[...].astype(F32)
        y = x * lax.rsqrt(jnp.mean(x * x, axis=-1, keepdims=True) + EPS)
        h_ref[...] = (y * g_ref[...]).astype(BF16)

    acc = jnp.dot(h_ref[...], w_ref[...], preferred_element_type=F32)
    o_ref[...] = epilogue(acc, aux_ref).astype(o_ref.dtype)


def _norm_matmul(x, g, w, aux, epilogue, out_dtype, tm, tn, name):
    n, d = x.shape
    cols = w.shape[1]
    tm = min(tm, n)
    tn = min(tn, cols)
    assert n % tm == 0 and cols % tn == 0
    return pl.pallas_call(
        functools.partial(_norm_matmul_kernel, epilogue=epilogue),
        grid=(n // tm, cols // tn),
        in_specs=[
            pl.BlockSpec((tm, d), lambda i, j: (i, 0)),
            pl.BlockSpec((1, d), lambda i, j: (0, 0)),
            pl.BlockSpec((d, tn), lambda i, j: (0, j)),
            pl.BlockSpec((8, tn), lambda i, j: (0, j)),
        ],
        out_specs=[pl.BlockSpec((tm, tn), lambda i, j: (i, j)),
                   pl.BlockSpec((tm, d), lambda i, j: (i, 0))],
        out_shape=[jax.ShapeDtypeStruct((n, cols), out_dtype), jax.ShapeDtypeStruct((n, d), BF16)],
        compiler_params=_cparams(("parallel", "arbitrary")),
        name=name,
    )(x, g.reshape(1, d).astype(F32), w, aux)


def _matmul_kernel(h_ref, w_ref, aux_ref, o_ref, *, epilogue):
    acc = jnp.dot(h_ref[...], w_ref[...], preferred_element_type=F32)
    o_ref[...] = epilogue(acc, aux_ref).astype(o_ref.dtype)


def _matmul(h, w, aux, epilogue, out_dtype, tm, tn, name):
    n, d = h.shape
    cols = w.shape[1]
    tm = min(tm, n)
    tn = min(tn, cols)
    assert n % tm == 0 and cols % tn == 0
    return pl.pallas_call(
        functools.partial(_matmul_kernel, epilogue=epilogue),
        grid=(n // tm, cols // tn),
        in_specs=[
            pl.BlockSpec((tm, d), lambda i, j: (i, 0)),
            pl.BlockSpec((d, tn), lambda i, j: (0, j)),
            pl.BlockSpec((8, tn), lambda i, j: (0, j)),
        ],
        out_specs=pl.BlockSpec((tm, tn), lambda i, j: (i, j)),
        out_shape=jax.ShapeDtypeStruct((n, cols), out_dtype),
        compiler_params=_cparams(("parallel", "parallel")),
        name=name,
    )(h, w, aux)


def _aux_rows(row):
    return jnp.zeros((8, row.shape[0]), F32).at[0].set(row.astype(F32))


def _mlstm_kernel(qk_ref, v_ref, gate_ref, so_ref, cw_ref, cb_ref, ng_ref, o_ref,
                  xprev_ref, c_ref, n_ref, m_ref):
    L = CHUNK
    dk, dv, nh = M_DQK, M_DV, M_HEADS

    @pl.when(pl.program_id(1) == 0)
    def _():
        xprev_ref[...] = jnp.zeros_like(xprev_ref)
        c_ref[...] = jnp.zeros_like(c_ref)
        n_ref[...] = jnp.zeros_like(n_ref)
        m_ref[...] = jnp.zeros_like(m_ref)

    xb = qk_ref[...]
    xx = jnp.concatenate([xprev_ref[...], xb], axis=0)
    n_sh = CONV_K - 1
    srow = lax.broadcasted_iota(jnp.int32, (n_sh * L, 2 * L), 0)
    scol = lax.broadcasted_iota(jnp.int32, (n_sh * L, 2 * L), 1)
    sel = (scol == (srow % L) + L - 1 - srow // L).astype(BF16)
    shifted = jnp.dot(sel, xx, preferred_element_type=F32)
    xprev_ref[...] = xb
    y = cb_ref[0:1, :] + cw_ref[CONV_K - 1:CONV_K, :] * xb.astype(F32)
    for s in range(n_sh):
        y = y + cw_ref[CONV_K - 2 - s:CONV_K - 1 - s, :] * shifted[s * L:(s + 1) * L, :]
    qk = y * jax.nn.sigmoid(y)

    g = gate_ref[...]
    lf = jnp.minimum(g, 0.0) - jnp.log1p(jnp.exp(-jnp.abs(g)))
    rr = lax.broadcasted_iota(jnp.int32, (L, L), 0)
    cc = lax.broadcasted_iota(jnp.int32, (L, L), 1)
    causal = cc <= rr
    tri = causal.astype(BF16)
    lf_hi = lf.astype(BF16)
    lf_lo = (lf - lf_hi.astype(F32)).astype(BF16)
    bcum = (jnp.dot(tri, lf_hi, preferred_element_type=F32)
            + jnp.dot(tri, lf_lo, preferred_element_type=F32))
    bcum_t = bcum.T
    g_t = g.T

    for h in range(nh):
        bcol = bcum[:, nh + h:nh + h + 1]
        brow = bcum_t[nh + h:nh + h + 1, :]
        irow = g_t[h:h + 1, :]
        icol = g[:, h:h + 1]
        m_old = m_ref[h]
        c_old = c_ref[h]
        n_old = n_ref[h]

        qh = qk[:, h * dk:(h + 1) * dk] * (dk ** -0.5)
        kh = qk[:, M_QK_W + h * dk:M_QK_W + (h + 1) * dk]
        vh = v_ref[:, h * dv:(h + 1) * dv]
        qb = qh.astype(BF16)
        kb = kh.astype(BF16)

        logd = jnp.where(causal, bcol - brow + irow, -jnp.inf)
        inter = bcol + m_old
        m_loc = jnp.maximum(inter, jnp.max(logd, axis=-1, keepdims=True))
        w_intra = jnp.exp(logd - m_loc)
        w_inter = jnp.exp(inter - m_loc)
        s = lax.dot_general(qb, kb, (((1,), (1,)), ((), ())), preferred_element_type=F32) * w_intra
        num = (jnp.dot(s.astype(BF16), vh, preferred_element_type=F32)
               + w_inter * jnp.dot(qb, c_old.astype(BF16), preferred_element_type=F32))
        den = (jnp.sum(s, axis=-1, keepdims=True)
               + w_inter * jnp.sum(qh * n_old, axis=-1, keepdims=True))
        hh = num / jnp.maximum(jnp.abs(den), jnp.exp(-m_loc))

        b_end = bcol[L - 1:L, :]
        log_w = b_end - bcol + icol
        m_new = jnp.maximum(b_end + m_old, jnp.max(log_w, axis=0, keepdims=True))
        w_s = jnp.exp(log_w - m_new)
        decay = jnp.exp(b_end + m_old - m_new)
        kw = kh * w_s
        c_ref[h] = decay * c_old + lax.dot_general(
            kw.astype(BF16), vh, (((0,), (0,)), ((), ())), preferred_element_type=F32)
        n_ref[h] = decay * n_old + jnp.sum(kw, axis=0, keepdims=True)
        m_ref[h] = m_new

        yn = hh * lax.rsqrt(jnp.mean(hh * hh, axis=-1, keepdims=True) + EPS)
        yn = yn * ng_ref[0:1, h * dv:(h + 1) * dv]
        o_ref[:, h * dv:(h + 1) * dv] = (so_ref[:, h * dv:(h + 1) * dv].astype(F32) * yn).astype(o_ref.dtype)


def _mlstm(qk, vproj, v_col, gates, so, so_col, conv_w, conv_b, norm_g, bsz, seq):
    n = bsz * seq
    nc = seq // CHUNK
    L = CHUNK
    return pl.pallas_call(
        _mlstm_kernel,
        grid=(bsz, nc),
        in_specs=[
            pl.BlockSpec((L, 2 * M_QK_W), lambda b, c: (b * nc + c, 0)),
            pl.BlockSpec((L, M_V_W), lambda b, c: (b * nc + c, v_col)),
            pl.BlockSpec((L, LANES), lambda b, c: (b * nc + c, 0)),
            pl.BlockSpec((L, M_V_W), lambda b, c: (b * nc + c, so_col)),
            pl.BlockSpec((CONV_K, 2 * M_QK_W), lambda b, c: (0, 0)),
            pl.BlockSpec((8, 2 * M_QK_W), lambda b, c: (0, 0)),
            pl.BlockSpec((8, M_V_W), lambda b, c: (0, 0)),
        ],
        out_specs=pl.BlockSpec((L, M_V_W), lambda b, c: (b * nc + c, 0)),
        out_shape=jax.ShapeDtypeStruct((n, M_V_W), BF16),
        scratch_shapes=[
            pltpu.VMEM((L, 2 * M_QK_W), BF16),
            pltpu.VMEM((M_HEADS, M_DQK, M_DV), F32),
            pltpu.VMEM((M_HEADS, 1, M_DQK), F32),
            pltpu.VMEM((M_HEADS, 1, 1), F32),
        ],
        compiler_params=_cparams(("parallel", "arbitrary")),
        name="mlstm",
    )(qk, vproj, gates, so, conv_w.astype(F32), _aux_rows(conv_b), _aux_rows(norm_g))


def _dattn_load_tile(q_ref, v_ref, vt_ref, seq):
    tv = 256

    @pl.when(pl.program_id(2) == 0)
    def _():
        for c in range(seq // tv):
            blk = v_ref[c * tv:(c + 1) * tv, :].astype(F32)
            vt_ref[:, c * tv:(c + 1) * tv] = blk.T.astype(BF16)

    qt = q_ref[...].astype(F32).T
    row = lax.broadcasted_iota(jnp.int32, qt.shape, 0)
    return (jnp.where(row < D_HD, qt, 0.0).astype(BF16), jnp.where(row >= D_HD, qt, 0.0).astype(BF16))


def _dattn_finish(lam_ref, sg_ref, o_ref, acc_ref, l1, l2, out_scale):
    ot = acc_ref[0] / l1 - lam_ref[0] * (acc_ref[1] / l2)
    o = ot.T
    y = o * lax.rsqrt(jnp.mean(o * o, axis=-1, keepdims=True) + EPS)
    o_ref[...] = (y * (sg_ref[0:1, :] * out_scale)).astype(o_ref.dtype)


def _causal_mask(s):
    kr = lax.broadcasted_iota(jnp.int32, s.shape, 0)
    qc = lax.broadcasted_iota(jnp.int32, s.shape, 1)
    return jnp.where(kr <= qc, s, NEG_BIG)


def _dattn_kernel(lam_ref, q_ref, k_ref, v_ref, sg_ref, o_ref, vt_ref, acc_ref, qz_ref, l_ref, p_ref, *,
                  seq, tq, tk, out_scale):
    assert tq == 2 * tk
    qi = pl.program_id(2)
    qz = _dattn_load_tile(q_ref, v_ref, vt_ref, seq)
    qz_ref[0] = qz[0]
    qz_ref[1] = qz[1]
    acc_ref[...] = jnp.zeros_like(acc_ref)
    l_ref[...] = jnp.zeros_like(l_ref)

    def scores(j, lo, slot):
        q0 = lo or 0
        kj = k_ref[pl.ds(pl.multiple_of(j * tk, tk), tk), :]
        for c in range(2):
            s = jnp.dot(kj, qz_ref[c, :, q0:], preferred_element_type=F32)
            if lo is not None:
                s = _causal_mask(s)
            p = jnp.exp2(s)
            l_ref[c, :, q0:] += jnp.sum(p, axis=0, keepdims=True)
            p_ref[slot, c, :, q0:] = p.astype(BF16)

    def accumulate(j, lo, slot):
        q0 = lo or 0
        vtj = vt_ref[:, pl.ds(pl.multiple_of(j * tk, tk), tk)]
        for c in range(2):
            acc_ref[c, :, q0:] += jnp.dot(vtj, p_ref[slot, c, :, q0:], preferred_element_type=F32)

    n = 2 * qi

    @pl.when(qi > 0)
    def _():
        scores(0, None, 0)

    def pair(i, carry):
        j = 2 * i
        accumulate(j, None, 0)
        scores(j + 1, None, 1)
        accumulate(j + 1, None, 1)
        scores(j + 2, None, 0)
        return carry

    lax.fori_loop(0, qi - 1, pair, 0)

    @pl.when(qi > 0)
    def _():
        accumulate(n - 2, None, 0)
        scores(n - 1, None, 1)
        accumulate(n - 1, None, 1)
        scores(n, 0, 0)

    @pl.when(qi == 0)
    def _():
        scores(n, 0, 0)

    accumulate(n, 0, 0)
    scores(n + 1, tk, 1)
    accumulate(n + 1, tk, 1)
    _dattn_finish(lam_ref, sg_ref, o_ref, acc_ref, l_ref[0], l_ref[1], out_scale)


def _dattn_stabilised_kernel(lam_ref, q_ref, k_ref, v_ref, sg_ref, o_ref, vt_ref, acc_ref, *,
                             seq, tq, tk, out_scale):
    qi = pl.program_id(2)
    qz = _dattn_load_tile(q_ref, v_ref, vt_ref, seq)
    acc_ref[...] = jnp.zeros_like(acc_ref)

    def chunk(j, carry, lo):
        start = pl.multiple_of(j * tk, tk)
        kj = k_ref[pl.ds(start, tk), :]
        vtj = vt_ref[:, pl.ds(start, tk)]
        q0 = lo or 0
        out = []
        for c in range(2):
            m, l = carry[2 * c][:, q0:], carry[2 * c + 1][:, q0:]
            s = jnp.dot(kj, qz[c][:, q0:], preferred_element_type=F32)
            if lo is not None:
                s = _causal_mask(s)
            m_new = jnp.maximum(m, jnp.max(s, axis=0, keepdims=True))
            p = jnp.exp2(s - m_new)
            alpha = jnp.exp2(m - m_new)
            l_new = alpha * l + jnp.sum(p, axis=0, keepdims=True)
            acc_ref[c, :, q0:] = alpha * acc_ref[c, :, q0:] + jnp.dot(
                vtj, p.astype(BF16), preferred_element_type=F32)
            if q0:
                m_new = jnp.concatenate([carry[2 * c][:, :q0], m_new], axis=1)
                l_new = jnp.concatenate([carry[2 * c + 1][:, :q0], l_new], axis=1)
            out += [m_new, l_new]
        return tuple(out)

    init = (jnp.full((1, tq), NEG_BIG, F32), jnp.zeros((1, tq), F32)) * 2
    n_full = qi * (tq // tk)
    carry = lax.fori_loop(0, n_full, lambda j, c: chunk(j, c, None), init)
    for d in range(tq // tk):
        carry = chunk(n_full + d, carry, d * tk)

    _dattn_finish(lam_ref, sg_ref, o_ref, acc_ref, carry[1], carry[3], out_scale)


def _diff_attention(lam, qk, v, v_col0, subln_g, bsz, seq, out_scale, stabilised, tq, tk):
    n = bsz * seq
    tq = min(tq, seq)
    tk = min(tk, tq)
    nq = seq // tq
    hw = 2 * D_HD
    scratch = [pltpu.VMEM((D_DV, seq), BF16), pltpu.VMEM((2, D_DV, tq), F32)]
    if stabilised:
        body = _dattn_stabilised_kernel
    else:
        body = _dattn_kernel
        scratch += [pltpu.VMEM((2, hw, tq), BF16), pltpu.VMEM((2, 1, tq), F32), pltpu.VMEM((2, 2, tk, tq), BF16)]
    return pl.pallas_call(
        functools.partial(body, seq=seq, tq=tq, tk=tk, out_scale=out_scale),
        grid=(bsz, D_HEADS, nq),
        in_specs=[
            pl.BlockSpec(memory_space=pltpu.SMEM),
            pl.BlockSpec((tq, hw), lambda b, h, i: (b * nq + i, h)),
            pl.BlockSpec((seq, hw), lambda b, h, i: (b, D_HEADS + h)),
            pl.BlockSpec((seq, D_DV), lambda b, h, i: (b, v_col0 + h)),
            pl.BlockSpec((8, D_DV), lambda b, h, i: (0, 0)),
        ],
        out_specs=pl.BlockSpec((tq, D_DV), lambda b, h, i: (b * nq + i, h)),
        out_shape=jax.ShapeDtypeStruct((n, D_V_W), BF16),
        scratch_shapes=scratch,
        compiler_params=_cparams(("parallel", "parallel", "arbitrary")),
        name="diff_attn_stabilised" if stabilised else "diff_attn",
    )(lam.reshape(1).astype(F32), qk, qk, v, _aux_rows(subln_g))


def _xattn_kernel(q_ref, mk_ref, mv_ref, o_ref):
    for h in range(C_HEADS):
        q = q_ref[:, h * C_DQK:(h + 1) * C_DQK]
        k = mk_ref[:, h * C_DQK:(h + 1) * C_DQK]
        s = lax.dot_general(q, k, (((1,), (1,)), ((), ())), preferred_element_type=F32)
        m = jnp.max(s, axis=-1, keepdims=True)
        p = jnp.exp2(s - m)
        l = jnp.sum(p, axis=-1, keepdims=True)
        o = jnp.dot(p.astype(BF16), mv_ref[:, h * C_DV:(h + 1) * C_DV], preferred_element_type=F32)
        o_ref[:, h * C_DV:(h + 1) * C_DV] = (o / l).astype(o_ref.dtype)


def _cross_attention(q, mk, mv, bsz, seq, mem_len, tq=512):
    n = bsz * seq
    tq = min(tq, seq)
    nq = seq // tq
    return pl.pallas_call(
        _xattn_kernel,
        grid=(bsz, nq),
        in_specs=[
            pl.BlockSpec((tq, C_Q_W), lambda b, i: (b * nq + i, 0)),
            pl.BlockSpec((mem_len, C_Q_W), lambda b, i: (b, 0)),
            pl.BlockSpec((mem_len, C_V_W), lambda b, i: (b, 0)),
        ],
        out_specs=pl.BlockSpec((tq, C_V_W), lambda b, i: (b * nq + i, 0)),
        out_shape=jax.ShapeDtypeStruct((n, C_V_W), BF16),
        compiler_params=_cparams(("parallel", "parallel")),
        name="cross_attn",
    )(q, mk, mv)


def _merge_kernel(x_ref, ym_ref, yd_ref, yc_ref, gm_ref, gd_ref, gc_ref,
                  wm_ref, wd_ref, wc_ref, wo_ref, o_ref):
    merged = gm_ref[...].astype(F32) * jnp.dot(ym_ref[...], wm_ref[...], preferred_element_type=F32)
    merged = merged + gd_ref[...].astype(F32) * jnp.dot(yd_ref[...], wd_ref[...], preferred_element_type=F32)
    merged = merged + gc_ref[...].astype(F32) * jnp.dot(yc_ref[...], wc_ref[...], preferred_element_type=F32)
    o_ref[...] = x_ref[...] + jnp.dot(merged.astype(BF16), wo_ref[...], preferred_element_type=F32)


def _merge(x, ym, yd, yc, gates, g_col0, wm, wd, wc, wo, tm=256):
    n, d = x.shape
    tm = min(tm, n)
    row = lambda i: (i, 0)
    wspec = pl.BlockSpec((d, d), lambda i: (0, 0))
    return pl.pallas_call(
        _merge_kernel,
        grid=(n // tm,),
        in_specs=[
            pl.BlockSpec((tm, d), row),
            pl.BlockSpec((tm, d), row),
            pl.BlockSpec((tm, d), row),
            pl.BlockSpec((tm, d), row),
            pl.BlockSpec((tm, d), lambda i: (i, g_col0)),
            pl.BlockSpec((tm, d), lambda i: (i, g_col0 + 1)),
            pl.BlockSpec((tm, d), lambda i: (i, g_col0 + 2)),
            wspec, wspec, wspec, wspec,
        ],
        out_specs=pl.BlockSpec((tm, d), row),
        out_shape=jax.ShapeDtypeStruct((n, d), F32),
        compiler_params=_cparams(("parallel",)),
        name="merge",
    )(x, ym, yd, yc, gates, gates, gates, wm, wd, wc, wo)


def _mlp_kernel(x_ref, g_ref, wu_ref, wd_ref, o_ref, h_ref, acc_ref):
    f = pl.program_id(1)

    @pl.when(f == 0)
    def _():
        x = x_ref[...]
        y = x * lax.rsqrt(jnp.mean(x * x, axis=-1, keepdims=True) + EPS)
        h_ref[...] = (y * g_ref[...]).astype(BF16)
        acc_ref[...] = jnp.zeros_like(acc_ref)

    u = jnp.maximum(jnp.dot(h_ref[...], wu_ref[...], preferred_element_type=F32), 0.0)
    acc_ref[...] += jnp.dot((u * u).astype(BF16), wd_ref[...], preferred_element_type=F32)

    @pl.when(f == pl.num_programs(1) - 1)
    def _():
        o_ref[...] = x_ref[...] + acc_ref[...]


def _mlp(x, g, wu, wd, tm=1024, tf=512):
    n, d = x.shape
    dff = wu.shape[1]
    tm = min(tm, n)
    return pl.pallas_call(
        _mlp_kernel,
        grid=(n // tm, dff // tf),
        in_specs=[
            pl.BlockSpec((tm, d), lambda i, f: (i, 0)),
            pl.BlockSpec((1, d), lambda i, f: (0, 0)),
            pl.BlockSpec((d, tf), lambda i, f: (0, f)),
            pl.BlockSpec((tf, d), lambda i, f: (f, 0)),
        ],
        out_specs=pl.BlockSpec((tm, d), lambda i, f: (i, 0)),
        out_shape=jax.ShapeDtypeStruct((n, d), F32),
        scratch_shapes=[pltpu.VMEM((tm, d), BF16), pltpu.VMEM((tm, d), F32)],
        compiler_params=_cparams(("parallel", "arbitrary")),
        name="mlp",
    )(x, g.reshape(1, d).astype(F32), wu, wd)


def _lambda_init(layer):
    return 0.8 - 0.6 * math.exp(-0.3 * layer)


def _layer(l, x2, mem2, bsz, seq, mem_len, p):
    split = [0]
    for w in (M_QK_W, M_QK_W, M_V_W, M_HEADS, M_HEADS, M_V_W, D_Q_W, D_Q_W, D_V_W, C_Q_W):
        split.append(split[-1] + w)
    w_in = p['w_in'][l]
    cols = lambda a, b: w_in[:, split[a]:split[b]]
    g_mix = p['norm_mix_g'][l]

    mqk, h = _norm_matmul(x2, g_mix, cols(0, 2).astype(BF16), jnp.zeros((8, 2 * M_QK_W), F32),
                          _ep_identity, BF16, 1024, 1024, "proj_mqk")
    w_if = jnp.zeros((D_MODEL, LANES), F32).at[:, :2 * M_HEADS].set(cols(3, 5)).astype(BF16)
    b_if = jnp.zeros((LANES,), F32).at[:2 * M_HEADS].set(jnp.concatenate([p['b_igate'][l], p['b_fgate'][l]]))
    gates_if = _matmul(h, w_if, _aux_rows(b_if), _ep_bias, F32, 2048, LANES, "proj_if")
    w_v = jnp.concatenate([cols(2, 3), cols(8, 9)], axis=1).astype(BF16)
    vproj = _matmul(h, w_v, jnp.zeros((8, w_v.shape[1]), F32), _ep_identity, BF16, 1024, 1024, "proj_v")
    w_sg = jnp.concatenate([cols(5, 6), p['w_gate'][l]], axis=1).astype(BF16)
    b_sg = jnp.concatenate([jnp.zeros((M_V_W,), F32), p['b_gate'][l].astype(F32)])
    sgates = _matmul(h, w_sg, _aux_rows(b_sg), _ep_sigmoid_bias, BF16, 1024, 1024, "proj_gates")
    w_dqk = cols(6, 8).astype(BF16)
    q_gain = jnp.tile(p['dq_norm_g'][l].astype(F32), D_Q_W // D_HD) * (D_HD ** -0.5 * LOG2E)
    k_gain = jnp.tile(p['dk_norm_g'][l].astype(F32), D_Q_W // D_HD)
    dqk = _matmul(h, w_dqk, _aux_rows(jnp.concatenate([q_gain, k_gain])),
                  functools.partial(_ep_group_norm, D_HD), BF16, 1024, 1024, "proj_dqk")
    cq_gain = jnp.tile(p['cq_norm_g'][l].astype(F32), C_HEADS) * (C_DQK ** -0.5 * LOG2E)
    cq = _matmul(h, cols(9, 10).astype(BF16), _aux_rows(cq_gain),
                 functools.partial(_ep_group_norm, C_DQK), BF16, 2048, C_Q_W, "proj_cq")
    w_kv = p['w_mem_kv'][l]
    ck_gain = jnp.tile(p['ck_norm_g'][l].astype(F32), C_HEADS)
    mk, hmem = _norm_matmul(mem2, p['mem_norm_g'][l], w_kv[:, :C_Q_W].astype(BF16), _aux_rows(ck_gain),
                            functools.partial(_ep_group_norm, C_DQK), BF16, 1024, C_Q_W, "proj_mk")
    mv = _matmul(hmem, w_kv[:, C_Q_W:].astype(BF16), jnp.zeros((8, C_V_W), F32),
                 _ep_identity, BF16, 1024, C_V_W, "proj_mv")

    y_m = _mlstm(mqk, vproj, 0, gates_if, sgates, 0, p['conv_w'][l], p['conv_b'][l], p['m_norm_g'][l], bsz, seq)

    lam_i = _lambda_init(l)
    lam = (jnp.exp(jnp.sum(p['lam_q1'][l].astype(F32) * p['lam_k1'][l].astype(F32)))
           - jnp.exp(jnp.sum(p['lam_q2'][l].astype(F32) * p['lam_k2'][l].astype(F32))) + lam_i)
    score_bound = 1.02 * D_HD * jnp.max(jnp.abs(q_gain)) * jnp.max(jnp.abs(k_gain))
    dattn = functools.partial(_diff_attention, lam, dqk, vproj, M_V_W // D_DV, p['subln_g'][l], bsz, seq,
                              1.0 - lam_i)
    y_d = lax.cond(score_bound <= MAX_UNSTABILISED_SCORE,
                   lambda: dattn(False, DATTN_TQ, DATTN_TK_FAST),
                   lambda: dattn(True, DATTN_TQ, DATTN_TK_STABILISED))

    y_c = _cross_attention(cq, mk, mv, bsz, seq, mem_len)

    x2 = _merge(x2, y_m, y_d, y_c, sgates, 1,
                p['w_proj_m'][l].astype(BF16), p['w_proj_d'][l].astype(BF16),
                p['w_proj_c'][l].astype(BF16), p['w_out'][l].astype(BF16))
    return _mlp(x2, p['norm_mlp_g'][l], p['w_up'][l].astype(BF16), p['w_down'][l].astype(BF16))


def kernel(x, mem, norm_mix_g, w_in, b_igate, b_fgate, conv_w, conv_b, m_norm_g, dq_norm_g, dk_norm_g, lam_q1, lam_k1, lam_q2, lam_k2, subln_g, cq_norm_g, ck_norm_g, mem_norm_g, w_mem_kv, w_gate, b_gate, w_proj_m, w_proj_d, w_proj_c, w_out, norm_mlp_g, w_up, w_down):
    p = dict(norm_mix_g=norm_mix_g, w_in=w_in, b_igate=b_igate, b_fgate=b_fgate, conv_w=conv_w, conv_b=conv_b,
             m_norm_g=m_norm_g, dq_norm_g=dq_norm_g, dk_norm_g=dk_norm_g, lam_q1=lam_q1, lam_k1=lam_k1,
             lam_q2=lam_q2, lam_k2=lam_k2, subln_g=subln_g, cq_norm_g=cq_norm_g, ck_norm_g=ck_norm_g,
             mem_norm_g=mem_norm_g, w_mem_kv=w_mem_kv, w_gate=w_gate, b_gate=b_gate, w_proj_m=w_proj_m,
             w_proj_d=w_proj_d, w_proj_c=w_proj_c, w_out=w_out, norm_mlp_g=norm_mlp_g, w_up=w_up, w_down=w_down)
    bsz, seq, d = x.shape
    mem_len = mem.shape[1]
    x2 = x.reshape(bsz * seq, d)
    mem2 = mem.reshape(bsz * mem_len, d)
    for l in range(w_in.shape[0]):
        x2 = _layer(l, x2, mem2, bsz, seq, mem_len, p)
    return x2.reshape(bsz, seq, d)
```

```python
import functools
import math

import jax
import jax.numpy as jnp
from jax import lax
from jax.experimental import pallas as pl
from jax.experimental.pallas import tpu as pltpu

F32 = jnp.float32
BF16 = jnp.bfloat16

EPS = 1e-6
LOG2E = math.log2(math.e)
NEG_BIG = -1e30

D_MODEL = 1024
MEM_LEN = 256
M_HEADS, M_DQK, M_DV = 4, 128, 256
CHUNK = 128
CONV_K = 4
D_HEADS, D_HD = 8, 64
D_DV = 2 * D_HD
C_HEADS, C_DQK, C_DV = 4, 128, 256
D_FF = 4 * D_MODEL
LANES = 128
MXU_DIM = 256

M_QK_W = M_HEADS * M_DQK
M_V_W = M_HEADS * M_DV
D_Q_W = D_HEADS * 2 * D_HD
D_V_W = D_HEADS * D_DV
C_Q_W = C_HEADS * C_DQK
C_V_W = C_HEADS * C_DV

VMEM_LIMIT = 56 * 1024 * 1024

PROJ_TM = 2048
PROJ_NORM_TM = 1024
PROJ_TN = 1024
MERGE_TM = 512
MLP_TM = 1024
MLP_TF = 1024
DATTN_TQ = 1024
DATTN_TK_FAST = 512
DATTN_TK_STABILISED = 256
MAX_UNSTABILISED_SCORE = 60.0


def _cparams(sem):
    return pltpu.CompilerParams(dimension_semantics=sem, vmem_limit_bytes=VMEM_LIMIT)


def _ep_identity(acc, aux_ref):
    return acc


def _ep_sigmoid_of_double(acc, aux_ref):
    return 0.5 * jnp.tanh(acc + aux_ref[0:1, :]) + 0.5


def _ep_bias(acc, aux_ref):
    return acc + aux_ref[0:1, :]


def _ep_group_norm(group, acc, aux_ref):
    tn = acc.shape[1]
    cw = MXU_DIM
    r = lax.broadcasted_iota(jnp.int32, (cw, cw), 0) // group
    c = lax.broadcasted_iota(jnp.int32, (cw, cw), 1) // group
    gmat = (r == c).astype(BF16)
    outs = []
    for s in range(tn // cw):
        a = acc[:, s * cw:(s + 1) * cw]
        ss = jnp.dot((a * a).astype(BF16), gmat, preferred_element_type=F32)
        outs.append(a * lax.rsqrt(ss * (1.0 / group) + EPS))
    y = jnp.concatenate(outs, axis=1) if len(outs) > 1 else outs[0]
    return y * aux_ref[0:1, :]


def _norm_matmul_kernel(x_ref, g_ref, w_ref, aux_ref, o_ref, h_ref, *, epilogue):
    @pl.when(pl.program_id(1) == 0)
    def _():
        x = x_ref[...].astype(F32)
        y = x * lax.rsqrt(jnp.mean(x * x, axis=-1, keepdims=True) + EPS)
        h_ref[...] = (y * g_ref[...]).astype(BF16)

    acc = jnp.dot(h_ref[...], w_ref[...], preferred_element_type=F32)
    o_ref[...] = epilogue(acc, aux_ref).astype(o_ref.dtype)


def _norm_matmul(x, g, w, aux, epilogue, out_dtype, tm, tn, name):
    n, d = x.shape
    cols = w.shape[1]
    tm = min(tm, n)
    tn = min(tn, cols)
    assert n % tm == 0 and cols % tn == 0
    return pl.pallas_call(
        functools.partial(_norm_matmul_kernel, epilogue=epilogue),
        grid=(n // tm, cols // tn),
        in_specs=[
            pl.BlockSpec((tm, d), lambda i, j: (i, 0)),
            pl.BlockSpec((1, d), lambda i, j: (0, 0)),
            pl.BlockSpec((d, tn), lambda i, j: (0, j)),
            pl.BlockSpec((8, tn), lambda i, j: (0, j)),
        ],
        out_specs=[pl.BlockSpec((tm, tn), lambda i, j: (i, j)),
                   pl.BlockSpec((tm, d), lambda i, j: (i, 0))],
        out_shape=[jax.ShapeDtypeStruct((n, cols), out_dtype), jax.ShapeDtypeStruct((n, d), BF16)],
        compiler_params=_cparams(("parallel", "arbitrary")),
        name=name,
    )(x, g.reshape(1, d).astype(F32), w, aux)


def _matmul_kernel(h_ref, w_ref, aux_ref, o_ref, *, epilogue):
    acc = jnp.dot(h_ref[...], w_ref[...], preferred_element_type=F32)
    o_ref[...] = epilogue(acc, aux_ref).astype(o_ref.dtype)


def _matmul(h, w, aux, epilogue, out_dtype, tm, tn, name):
    n, d = h.shape
    cols = w.shape[1]
    tm = min(tm, n)
    tn = min(tn, cols)
    assert n % tm == 0 and cols % tn == 0
    return pl.pallas_call(
        functools.partial(_matmul_kernel, epilogue=epilogue),
        grid=(n // tm, cols // tn),
        in_specs=[
            pl.BlockSpec((tm, d), lambda i, j: (i, 0)),
            pl.BlockSpec((d, tn), lambda i, j: (0, j)),
            pl.BlockSpec((8, tn), lambda i, j: (0, j)),
        ],
        out_specs=pl.BlockSpec((tm, tn), lambda i, j: (i, j)),
        out_shape=jax.ShapeDtypeStruct((n, cols), out_dtype),
        compiler_params=_cparams(("parallel", "parallel")),
        name=name,
    )(h, w, aux)


def _aux_rows(row):
    return jnp.zeros((8, row.shape[0]), F32).at[0].set(row.astype(F32))


def _mlstm_kernel(qk_ref, v_ref, gate_ref, so_ref, cw_ref, cb_ref, ng_ref, o_ref,
                  xprev_ref, c_ref, n_ref, m_ref):
    L = CHUNK
    dk, dv, nh = M_DQK, M_DV, M_HEADS

    @pl.when(pl.program_id(1) == 0)
    def _():
        xprev_ref[...] = jnp.zeros_like(xprev_ref)
        c_ref[...] = jnp.zeros_like(c_ref)
        n_ref[...] = jnp.zeros_like(n_ref)
        m_ref[...] = jnp.zeros_like(m_ref)

    xb = qk_ref[...]
    xx = jnp.concatenate([xprev_ref[...], xb], axis=0)
    n_sh = CONV_K - 1
    srow = lax.broadcasted_iota(jnp.int32, (n_sh * L, 2 * L), 0)
    scol = lax.broadcasted_iota(jnp.int32, (n_sh * L, 2 * L), 1)
    sel = (scol == (srow % L) + L - 1 - srow // L).astype(BF16)
    shifted = jnp.dot(sel, xx, preferred_element_type=F32)
    xprev_ref[...] = xb
    y = cb_ref[0:1, :] + cw_ref[CONV_K - 1:CONV_K, :] * xb.astype(F32)
    for s in range(n_sh):
        y = y + cw_ref[CONV_K - 2 - s:CONV_K - 1 - s, :] * shifted[s * L:(s + 1) * L, :]
    qk = y * (0.5 * jnp.tanh(0.5 * y) + 0.5)

    g = gate_ref[...]
    lf = jnp.minimum(g, 0.0) - jnp.log1p(jnp.exp(-jnp.abs(g)))
    rr = lax.broadcasted_iota(jnp.int32, (L, L), 0)
    cc = lax.broadcasted_iota(jnp.int32, (L, L), 1)
    causal = cc <= rr
    tri = causal.astype(BF16)
    lf_hi = lf.astype(BF16)
    lf_lo = (lf - lf_hi.astype(F32)).astype(BF16)
    bcum = (jnp.dot(tri, lf_hi, preferred_element_type=F32)
            + jnp.dot(tri, lf_lo, preferred_element_type=F32))
    bcum_t = bcum.T
    g_t = g.T

    for h in range(nh):
        bcol = bcum[:, nh + h:nh + h + 1]
        brow = bcum_t[nh + h:nh + h + 1, :]
        irow = g_t[h:h + 1, :]
        icol = g[:, h:h + 1]
        m_old = m_ref[h]
        c_old = c_ref[h]
        n_old = n_ref[h]

        qh = qk[:, h * dk:(h + 1) * dk] * (dk ** -0.5)
        kh = qk[:, M_QK_W + h * dk:M_QK_W + (h + 1) * dk]
        vh = v_ref[:, h * dv:(h + 1) * dv]
        qb = qh.astype(BF16)
        kb = kh.astype(BF16)

        logd = jnp.where(causal, bcol - brow + irow, -jnp.inf)
        inter = bcol + m_old
        m_loc = jnp.maximum(inter, jnp.max(logd, axis=-1, keepdims=True))
        w_intra = jnp.exp(logd - m_loc)
        w_inter = jnp.exp(inter - m_loc)
        s = lax.dot_general(qb, kb, (((1,), (1,)), ((), ())), preferred_element_type=F32) * w_intra
        num = (jnp.dot(s.astype(BF16), vh, preferred_element_type=F32)
               + w_inter * jnp.dot(qb, c_old.astype(BF16), preferred_element_type=F32))
        den = (jnp.sum(s, axis=-1, keepdims=True)
               + w_inter * jnp.sum(qh * n_old, axis=-1, keepdims=True))
        hh = num / jnp.maximum(jnp.abs(den), jnp.exp(-m_loc))

        b_end = bcol[L - 1:L, :]
        log_w = b_end - bcol + icol
        m_new = jnp.maximum(b_end + m_old, jnp.max(log_w, axis=0, keepdims=True))
        w_s = jnp.exp(log_w - m_new)
        decay = jnp.exp(b_end + m_old - m_new)
        kw = kh * w_s
        c_ref[h] = decay * c_old + lax.dot_general(
            kw.astype(BF16), vh, (((0,), (0,)), ((), ())), preferred_element_type=F32)
        n_ref[h] = decay * n_old + jnp.sum(kw, axis=0, keepdims=True)
        m_ref[h] = m_new

        yn = hh * lax.rsqrt(jnp.mean(hh * hh, axis=-1, keepdims=True) + EPS)
        yn = yn * ng_ref[0:1, h * dv:(h + 1) * dv]
        o_ref[:, h * dv:(h + 1) * dv] = (so_ref[:, h * dv:(h + 1) * dv].astype(F32) * yn).astype(o_ref.dtype)


def _mlstm(qk, vproj, v_col, gates, so, so_col, conv_w, conv_b, norm_g, bsz, seq):
    n = bsz * seq
    nc = seq // CHUNK
    L = CHUNK
    return pl.pallas_call(
        _mlstm_kernel,
        grid=(bsz, nc),
        in_specs=[
            pl.BlockSpec((L, 2 * M_QK_W), lambda b, c: (b * nc + c, 0)),
            pl.BlockSpec((L, M_V_W), lambda b, c: (b * nc + c, v_col)),
            pl.BlockSpec((L, LANES), lambda b, c: (b * nc + c, 0)),
            pl.BlockSpec((L, M_V_W), lambda b, c: (b * nc + c, so_col)),
            pl.BlockSpec((CONV_K, 2 * M_QK_W), lambda b, c: (0, 0)),
            pl.BlockSpec((8, 2 * M_QK_W), lambda b, c: (0, 0)),
            pl.BlockSpec((8, M_V_W), lambda b, c: (0, 0)),
        ],
        out_specs=pl.BlockSpec((L, M_V_W), lambda b, c: (b * nc + c, 0)),
        out_shape=jax.ShapeDtypeStruct((n, M_V_W), BF16),
        scratch_shapes=[
            pltpu.VMEM((L, 2 * M_QK_W), BF16),
            pltpu.VMEM((M_HEADS, M_DQK, M_DV), F32),
            pltpu.VMEM((M_HEADS, 1, M_DQK), F32),
            pltpu.VMEM((M_HEADS, 1, 1), F32),
        ],
        compiler_params=_cparams(("parallel", "arbitrary")),
        name="mlstm",
    )(qk, vproj, gates, so, conv_w.astype(F32), _aux_rows(conv_b), _aux_rows(norm_g))


def _dattn_load_tile(q_ref, v_ref, vt_ref, seq):
    tv = 256

    @pl.when(pl.program_id(2) == 0)
    def _():
        for c in range(seq // tv):
            blk = v_ref[c * tv:(c + 1) * tv, :].astype(F32)
            vt_ref[:, c * tv:(c + 1) * tv] = blk.T.astype(BF16)

    qt = q_ref[...].astype(F32).T
    row = lax.broadcasted_iota(jnp.int32, qt.shape, 0)
    return (jnp.where(row < D_HD, qt, 0.0).astype(BF16), jnp.where(row >= D_HD, qt, 0.0).astype(BF16))


def _dattn_finish(lam_ref, sg_ref, o_ref, acc_ref, l1, l2, out_scale):
    ot = acc_ref[0] / l1 - lam_ref[0] * (acc_ref[1] / l2)
    o = ot.T
    y = o * lax.rsqrt(jnp.mean(o * o, axis=-1, keepdims=True) + EPS)
    o_ref[...] = (y * (sg_ref[0:1, :] * out_scale)).astype(o_ref.dtype)


def _causal_mask(s):
    kr = lax.broadcasted_iota(jnp.int32, s.shape, 0)
    qc = lax.broadcasted_iota(jnp.int32, s.shape, 1)
    return jnp.where(kr <= qc, s, NEG_BIG)


def _dattn_kernel(lam_ref, q_ref, k_ref, v_ref, sg_ref, o_ref, vt_ref, acc_ref, qz_ref, l_ref, p_ref, *,
                  seq, tq, tk, out_scale):
    assert tq == 2 * tk
    qi = pl.program_id(2)
    qz = _dattn_load_tile(q_ref, v_ref, vt_ref, seq)
    qz_ref[0] = qz[0]
    qz_ref[1] = qz[1]
    acc_ref[...] = jnp.zeros_like(acc_ref)
    l_ref[...] = jnp.zeros_like(l_ref)

    def scores(j, lo, slot):
        q0 = lo or 0
        kj = k_ref[pl.ds(pl.multiple_of(j * tk, tk), tk), :]
        for c in range(2):
            s = jnp.dot(kj, qz_ref[c, :, q0:], preferred_element_type=F32)
            if lo is not None:
                s = _causal_mask(s)
            p = jnp.exp2(s)
            l_ref[c, :, q0:] += jnp.sum(p, axis=0, keepdims=True)
            p_ref[slot, c, :, q0:] = p.astype(BF16)

    def accumulate(j, lo, slot):
        q0 = lo or 0
        vtj = vt_ref[:, pl.ds(pl.multiple_of(j * tk, tk), tk)]
        for c in range(2):
            acc_ref[c, :, q0:] += jnp.dot(vtj, p_ref[slot, c, :, q0:], preferred_element_type=F32)

    n = 2 * qi

    @pl.when(qi > 0)
    def _():
        scores(0, None, 0)

    def pair(i, carry):
        j = 2 * i
        accumulate(j, None, 0)
        scores(j + 1, None, 1)
        accumulate(j + 1, None, 1)
        scores(j + 2, None, 0)
        return carry

    lax.fori_loop(0, qi - 1, pair, 0)

    @pl.when(qi > 0)
    def _():
        accumulate(n - 2, None, 0)
        scores(n - 1, None, 1)
        accumulate(n - 1, None, 1)
        scores(n, 0, 0)

    @pl.when(qi == 0)
    def _():
        scores(n, 0, 0)

    accumulate(n, 0, 0)
    scores(n + 1, tk, 1)
    accumulate(n + 1, tk, 1)
    _dattn_finish(lam_ref, sg_ref, o_ref, acc_ref, l_ref[0], l_ref[1], out_scale)


def _dattn_stabilised_kernel(lam_ref, q_ref, k_ref, v_ref, sg_ref, o_ref, vt_ref, acc_ref, *,
                             seq, tq, tk, out_scale):
    qi = pl.program_id(2)
    qz = _dattn_load_tile(q_ref, v_ref, vt_ref, seq)
    acc_ref[...] = jnp.zeros_like(acc_ref)

    def chunk(j, carry, lo):
        start = pl.multiple_of(j * tk, tk)
        kj = k_ref[pl.ds(start, tk), :]
        vtj = vt_ref[:, pl.ds(start, tk)]
        q0 = lo or 0
        out = []
        for c in range(2):
            m, l = carry[2 * c][:, q0:], carry[2 * c + 1][:, q0:]
            s = jnp.dot(kj, qz[c][:, q0:], preferred_element_type=F32)
            if lo is not None:
                s = _causal_mask(s)
            m_new = jnp.maximum(m, jnp.max(s, axis=0, keepdims=True))
            p = jnp.exp2(s - m_new)
            alpha = jnp.exp2(m - m_new)
            l_new = alpha * l + jnp.sum(p, axis=0, keepdims=True)
            acc_ref[c, :, q0:] = alpha * acc_ref[c, :, q0:] + jnp.dot(
                vtj, p.astype(BF16), preferred_element_type=F32)
            if q0:
                m_new = jnp.concatenate([carry[2 * c][:, :q0], m_new], axis=1)
                l_new = jnp.concatenate([carry[2 * c + 1][:, :q0], l_new], axis=1)
            out += [m_new, l_new]
        return tuple(out)

    init = (jnp.full((1, tq), NEG_BIG, F32), jnp.zeros((1, tq), F32)) * 2
    n_full = qi * (tq // tk)
    carry = lax.fori_loop(0, n_full, lambda j, c: chunk(j, c, None), init)
    for d in range(tq // tk):
        carry = chunk(n_full + d, carry, d * tk)

    _dattn_finish(lam_ref, sg_ref, o_ref, acc_ref, carry[1], carry[3], out_scale)


def _diff_attention(lam, qk, v, v_col0, subln_g, bsz, seq, out_scale, stabilised, tq, tk):
    n = bsz * seq
    tq = min(tq, seq)
    tk = min(tk, tq)
    nq = seq // tq
    hw = 2 * D_HD
    scratch = [pltpu.VMEM((D_DV, seq), BF16), pltpu.VMEM((2, D_DV, tq), F32)]
    if stabilised:
        body = _dattn_stabilised_kernel
    else:
        body = _dattn_kernel
        scratch += [pltpu.VMEM((2, hw, tq), BF16), pltpu.VMEM((2, 1, tq), F32), pltpu.VMEM((2, 2, tk, tq), BF16)]
    return pl.pallas_call(
        functools.partial(body, seq=seq, tq=tq, tk=tk, out_scale=out_scale),
        grid=(bsz, D_HEADS, nq),
        in_specs=[
            pl.BlockSpec(memory_space=pltpu.SMEM),
            pl.BlockSpec((tq, hw), lambda b, h, i: (b * nq + i, h)),
            pl.BlockSpec((seq, hw), lambda b, h, i: (b, D_HEADS + h)),
            pl.BlockSpec((seq, D_DV), lambda b, h, i: (b, v_col0 + h)),
            pl.BlockSpec((8, D_DV), lambda b, h, i: (0, 0)),
        ],
        out_specs=pl.BlockSpec((tq, D_DV), lambda b, h, i: (b * nq + i, h)),
        out_shape=jax.ShapeDtypeStruct((n, D_V_W), BF16),
        scratch_shapes=scratch,
        compiler_params=_cparams(("parallel", "parallel", "arbitrary")),
        name="diff_attn_stabilised" if stabilised else "diff_attn",
    )(lam.reshape(1).astype(F32), qk, qk, v, _aux_rows(subln_g))


def _xattn_kernel(q_ref, mk_ref, mv_ref, o_ref):
    for h in range(C_HEADS):
        q = q_ref[:, h * C_DQK:(h + 1) * C_DQK]
        k = mk_ref[:, h * C_DQK:(h + 1) * C_DQK]
        s = lax.dot_general(q, k, (((1,), (1,)), ((), ())), preferred_element_type=F32)
        m = jnp.max(s, axis=-1, keepdims=True)
        p = jnp.exp2(s - m)
        l = jnp.sum(p, axis=-1, keepdims=True)
        o = jnp.dot(p.astype(BF16), mv_ref[:, h * C_DV:(h + 1) * C_DV], preferred_element_type=F32)
        o_ref[:, h * C_DV:(h + 1) * C_DV] = (o / l).astype(o_ref.dtype)


def _cross_attention(q, mk, mv, bsz, seq, mem_len, tq=512):
    n = bsz * seq
    tq = min(tq, seq)
    nq = seq // tq
    return pl.pallas_call(
        _xattn_kernel,
        grid=(bsz, nq),
        in_specs=[
            pl.BlockSpec((tq, C_Q_W), lambda b, i: (b * nq + i, 0)),
            pl.BlockSpec((mem_len, C_Q_W), lambda b, i: (b, 0)),
            pl.BlockSpec((mem_len, C_V_W), lambda b, i: (b, 0)),
        ],
        out_specs=pl.BlockSpec((tq, C_V_W), lambda b, i: (b * nq + i, 0)),
        out_shape=jax.ShapeDtypeStruct((n, C_V_W), BF16),
        compiler_params=_cparams(("parallel", "parallel")),
        name="cross_attn",
    )(q, mk, mv)


def _merge_kernel(x_ref, ym_ref, yd_ref, yc_ref, gm_ref, gd_ref, gc_ref,
                  wm_ref, wd_ref, wc_ref, wo_ref, g_ref, o_ref, h_ref):
    merged = gm_ref[...].astype(F32) * jnp.dot(ym_ref[...], wm_ref[...], preferred_element_type=F32)
    merged = merged + gd_ref[...].astype(F32) * jnp.dot(yd_ref[...], wd_ref[...], preferred_element_type=F32)
    merged = merged + gc_ref[...].astype(F32) * jnp.dot(yc_ref[...], wc_ref[...], preferred_element_type=F32)
    x1 = x_ref[...] + jnp.dot(merged.astype(BF16), wo_ref[...], preferred_element_type=F32)
    o_ref[...] = x1
    y = x1 * lax.rsqrt(jnp.mean(x1 * x1, axis=-1, keepdims=True) + EPS)
    h_ref[...] = (y * g_ref[...]).astype(BF16)


def _merge(x, ym, yd, yc, gates, g_col0, wm, wd, wc, wo, g_next, tm):
    n, d = x.shape
    tm = min(tm, n)
    row = lambda i: (i, 0)
    wspec = pl.BlockSpec((d, d), lambda i: (0, 0), pipeline_mode=pl.Buffered(1))
    return pl.pallas_call(
        _merge_kernel,
        grid=(n // tm,),
        in_specs=[
            pl.BlockSpec((tm, d), row),
            pl.BlockSpec((tm, d), row),
            pl.BlockSpec((tm, d), row),
            pl.BlockSpec((tm, d), row),
            pl.BlockSpec((tm, d), lambda i: (i, g_col0)),
            pl.BlockSpec((tm, d), lambda i: (i, g_col0 + 1)),
            pl.BlockSpec((tm, d), lambda i: (i, g_col0 + 2)),
            wspec, wspec, wspec, wspec,
            pl.BlockSpec((1, d), lambda i: (0, 0)),
        ],
        out_specs=[pl.BlockSpec((tm, d), row), pl.BlockSpec((tm, d), row)],
        out_shape=[jax.ShapeDtypeStruct((n, d), F32), jax.ShapeDtypeStruct((n, d), BF16)],
        compiler_params=_cparams(("parallel",)),
        name="merge",
    )(x, ym, yd, yc, gates, gates, gates, wm, wd, wc, wo, g_next.reshape(1, d).astype(F32))


def _mlp_kernel(x_ref, h_ref, wu_ref, wd_ref, o_ref):
    @pl.when(pl.program_id(1) == 0)
    def _():
        o_ref[...] = x_ref[...]

    u = jnp.maximum(jnp.dot(h_ref[...], wu_ref[...], preferred_element_type=F32), 0.0)
    o_ref[...] += jnp.dot((u * u).astype(BF16), wd_ref[...], preferred_element_type=F32)


def _mlp(x, h, wu, wd, tm, tf):
    n, d = x.shape
    dff = wu.shape[1]
    tm = min(tm, n)
    return pl.pallas_call(
        _mlp_kernel,
        grid=(n // tm, dff // tf),
        in_specs=[
            pl.BlockSpec((tm, d), lambda i, f: (i, 0)),
            pl.BlockSpec((tm, d), lambda i, f: (i, 0)),
            pl.BlockSpec((d, tf), lambda i, f: (0, f)),
            pl.BlockSpec((tf, d), lambda i, f: (f, 0)),
        ],
        out_specs=pl.BlockSpec((tm, d), lambda i, f: (i, 0)),
        out_shape=jax.ShapeDtypeStruct((n, d), F32),
        compiler_params=_cparams(("parallel", "arbitrary")),
        name="mlp",
    )(x, h, wu, wd)


def _lambda_init(layer):
    return 0.8 - 0.6 * math.exp(-0.3 * layer)


def _layer(l, x2, mem2, bsz, seq, mem_len, p):
    split = [0]
    for w in (M_QK_W, M_QK_W, M_V_W, M_HEADS, M_HEADS, M_V_W, D_Q_W, D_Q_W, D_V_W, C_Q_W):
        split.append(split[-1] + w)
    w_in = p['w_in'][l]
    cols = lambda a, b: w_in[:, split[a]:split[b]]
    g_mix = p['norm_mix_g'][l]

    mqk, h = _norm_matmul(x2, g_mix, cols(0, 2).astype(BF16), jnp.zeros((8, 2 * M_QK_W), F32),
                          _ep_identity, BF16, 1024, 1024, "proj_mqk")
    w_if = jnp.zeros((D_MODEL, LANES), F32).at[:, :2 * M_HEADS].set(cols(3, 5)).astype(BF16)
    b_if = jnp.zeros((LANES,), F32).at[:2 * M_HEADS].set(jnp.concatenate([p['b_igate'][l], p['b_fgate'][l]]))
    gates_if = _matmul(h, w_if, _aux_rows(b_if), _ep_bias, F32, 2048, LANES, "proj_if")
    w_v = jnp.concatenate([cols(2, 3), cols(8, 9)], axis=1).astype(BF16)
    vproj = _matmul(h, w_v, jnp.zeros((8, w_v.shape[1]), F32), _ep_identity, BF16, PROJ_TM, PROJ_TN, "proj_v")
    w_sg = (0.5 * jnp.concatenate([cols(5, 6), p['w_gate'][l]], axis=1)).astype(BF16)
    b_sg = 0.5 * jnp.concatenate([jnp.zeros((M_V_W,), F32), p['b_gate'][l].astype(F32)])
    sgates = _matmul(h, w_sg, _aux_rows(b_sg), _ep_sigmoid_of_double, BF16, PROJ_TM, PROJ_TN, "proj_gates")
    w_dqk = cols(6, 8).astype(BF16)
    q_gain = jnp.tile(p['dq_norm_g'][l].astype(F32), D_Q_W // D_HD) * (D_HD ** -0.5 * LOG2E)
    k_gain = jnp.tile(p['dk_norm_g'][l].astype(F32), D_Q_W // D_HD)
    dqk = _matmul(h, w_dqk, _aux_rows(jnp.concatenate([q_gain, k_gain])),
                  functools.partial(_ep_group_norm, D_HD), BF16, PROJ_NORM_TM, PROJ_TN, "proj_dqk")
    cq_gain = jnp.tile(p['cq_norm_g'][l].astype(F32), C_HEADS) * (C_DQK ** -0.5 * LOG2E)
    cq = _matmul(h, cols(9, 10).astype(BF16), _aux_rows(cq_gain),
                 functools.partial(_ep_group_norm, C_DQK), BF16, PROJ_NORM_TM, C_Q_W, "proj_cq")
    w_kv = p['w_mem_kv'][l]
    ck_gain = jnp.tile(p['ck_norm_g'][l].astype(F32), C_HEADS)
    mk, hmem = _norm_matmul(mem2, p['mem_norm_g'][l], w_kv[:, :C_Q_W].astype(BF16), _aux_rows(ck_gain),
                            functools.partial(_ep_group_norm, C_DQK), BF16, 1024, C_Q_W, "proj_mk")
    mv = _matmul(hmem, w_kv[:, C_Q_W:].astype(BF16), jnp.zeros((8, C_V_W), F32),
                 _ep_identity, BF16, 1024, C_V_W, "proj_mv")

    y_m = _mlstm(mqk, vproj, 0, gates_if, sgates, 0, p['conv_w'][l], p['conv_b'][l], p['m_norm_g'][l], bsz, seq)

    lam_i = _lambda_init(l)
    lam = (jnp.exp(jnp.sum(p['lam_q1'][l].astype(F32) * p['lam_k1'][l].astype(F32)))
           - jnp.exp(jnp.sum(p['lam_q2'][l].astype(F32) * p['lam_k2'][l].astype(F32))) + lam_i)
    score_bound = 1.02 * D_HD * jnp.max(jnp.abs(q_gain)) * jnp.max(jnp.abs(k_gain))
    dattn = functools.partial(_diff_attention, lam, dqk, vproj, M_V_W // D_DV, p['subln_g'][l], bsz, seq,
                              1.0 - lam_i)
    y_d = lax.cond(score_bound <= MAX_UNSTABILISED_SCORE,
                   lambda: dattn(False, DATTN_TQ, DATTN_TK_FAST),
                   lambda: dattn(True, DATTN_TQ, DATTN_TK_STABILISED))

    y_c = _cross_attention(cq, mk, mv, bsz, seq, mem_len)

    x2, h2 = _merge(x2, y_m, y_d, y_c, sgates, 1,
                    p['w_proj_m'][l].astype(BF16), p['w_proj_d'][l].astype(BF16),
                    p['w_proj_c'][l].astype(BF16), p['w_out'][l].astype(BF16), p['norm_mlp_g'][l], MERGE_TM)
    return _mlp(x2, h2, p['w_up'][l].astype(BF16), p['w_down'][l].astype(BF16), MLP_TM, MLP_TF)


def kernel(x, mem, norm_mix_g, w_in, b_igate, b_fgate, conv_w, conv_b, m_norm_g, dq_norm_g, dk_norm_g, lam_q1, lam_k1, lam_q2, lam_k2, subln_g, cq_norm_g, ck_norm_g, mem_norm_g, w_mem_kv, w_gate, b_gate, w_proj_m, w_proj_d, w_proj_c, w_out, norm_mlp_g, w_up, w_down):
    p = dict(norm_mix_g=norm_mix_g, w_in=w_in, b_igate=b_igate, b_fgate=b_fgate, conv_w=conv_w, conv_b=conv_b,
             m_norm_g=m_norm_g, dq_norm_g=dq_norm_g, dk_norm_g=dk_norm_g, lam_q1=lam_q1, lam_k1=lam_k1,
             lam_q2=lam_q2, lam_k2=lam_k2, subln_g=subln_g, cq_norm_g=cq_norm_g, ck_norm_g=ck_norm_g,
             mem_norm_g=mem_norm_g, w_mem_kv=w_mem_kv, w_gate=w_gate, b_gate=b_gate, w_proj_m=w_proj_m,
             w_proj_d=w_proj_d, w_proj_c=w_proj_c, w_out=w_out, norm_mlp_g=norm_mlp_g, w_up=w_up, w_down=w_down)
    bsz, seq, d = x.shape
    mem_len = mem.shape[1]
    x2 = x.reshape(bsz * seq, d)
    mem2 = mem.reshape(bsz * mem_len, d)
    for l in range(w_in.shape[0]):
        x2 = _layer(l, x2, mem2, bsz, seq, mem_len, p)
    return x2.reshape(bsz, seq, d)
```

```python
import functools
import math

import jax
import jax.numpy as jnp
from jax import lax
from jax.experimental import pallas as pl
from jax.experimental.pallas import tpu as pltpu

F32 = jnp.float32
BF16 = jnp.bfloat16

EPS = 1e-6
LOG2E = math.log2(math.e)
NEG_BIG = -1e30

D_MODEL = 1024
MEM_LEN = 256
M_HEADS, M_DQK, M_DV = 4, 128, 256
CHUNK = 128
CONV_K = 4
D_HEADS, D_HD = 8, 64
D_DV = 2 * D_HD
C_HEADS, C_DQK, C_DV = 4, 128, 256
D_FF = 4 * D_MODEL
LANES = 128
MXU_DIM = 256

M_QK_W = M_HEADS * M_DQK
M_V_W = M_HEADS * M_DV
D_Q_W = D_HEADS * 2 * D_HD
D_V_W = D_HEADS * D_DV
C_Q_W = C_HEADS * C_DQK
C_V_W = C_HEADS * C_DV

VMEM_LIMIT = 56 * 1024 * 1024

PROJ_TM = 2048
PROJ_NORM_TM = 1024
PROJ_TN = 1024
MERGE_TM = 512
MLP_TM = 1024
MLP_TF = 1024
DATTN_TQ_FAST = 2048
DATTN_TQ_STABILISED = 1024
DATTN_TK_FAST = 512
DATTN_TK_STABILISED = 256
MAX_UNSTABILISED_SCORE = 60.0


def _cparams(sem):
    return pltpu.CompilerParams(dimension_semantics=sem, vmem_limit_bytes=VMEM_LIMIT)


def _ep_identity(acc, aux_ref):
    return acc


def _ep_sigmoid_of_double(acc, aux_ref):
    return 0.5 * jnp.tanh(acc + aux_ref[0:1, :]) + 0.5


def _ep_bias(acc, aux_ref):
    return acc + aux_ref[0:1, :]


def _ep_group_norm(group, acc, aux_ref):
    tn = acc.shape[1]
    cw = MXU_DIM
    r = lax.broadcasted_iota(jnp.int32, (cw, cw), 0) // group
    c = lax.broadcasted_iota(jnp.int32, (cw, cw), 1) // group
    gmat = (r == c).astype(BF16)
    outs = []
    for s in range(tn // cw):
        a = acc[:, s * cw:(s + 1) * cw]
        ss = jnp.dot((a * a).astype(BF16), gmat, preferred_element_type=F32)
        outs.append(a * lax.rsqrt(ss * (1.0 / group) + EPS))
    y = jnp.concatenate(outs, axis=1) if len(outs) > 1 else outs[0]
    return y * aux_ref[0:1, :]


def _norm_matmul_kernel(x_ref, g_ref, w_ref, aux_ref, o_ref, h_ref, *, epilogue):
    @pl.when(pl.program_id(1) == 0)
    def _():
        x = x_ref[...].astype(F32)
        y = x * lax.rsqrt(jnp.mean(x * x, axis=-1, keepdims=True) + EPS)
        h_ref[...] = (y * g_ref[...]).astype(BF16)

    acc = jnp.dot(h_ref[...], w_ref[...], preferred_element_type=F32)
    o_ref[...] = epilogue(acc, aux_ref).astype(o_ref.dtype)


def _norm_matmul(x, g, w, aux, epilogue, out_dtype, tm, tn, name):
    n, d = x.shape
    cols = w.shape[1]
    tm = min(tm, n)
    tn = min(tn, cols)
    assert n % tm == 0 and cols % tn == 0
    return pl.pallas_call(
        functools.partial(_norm_matmul_kernel, epilogue=epilogue),
        grid=(n // tm, cols // tn),
        in_specs=[
            pl.BlockSpec((tm, d), lambda i, j: (i, 0)),
            pl.BlockSpec((1, d), lambda i, j: (0, 0)),
            pl.BlockSpec((d, tn), lambda i, j: (0, j)),
            pl.BlockSpec((8, tn), lambda i, j: (0, j)),
        ],
        out_specs=[pl.BlockSpec((tm, tn), lambda i, j: (i, j)),
                   pl.BlockSpec((tm, d), lambda i, j: (i, 0))],
        out_shape=[jax.ShapeDtypeStruct((n, cols), out_dtype), jax.ShapeDtypeStruct((n, d), BF16)],
        compiler_params=_cparams(("parallel", "arbitrary")),
        name=name,
    )(x, g.reshape(1, d).astype(F32), w, aux)


def _matmul_kernel(h_ref, w_ref, aux_ref, o_ref, *, epilogue):
    acc = jnp.dot(h_ref[...], w_ref[...], preferred_element_type=F32)
    o_ref[...] = epilogue(acc, aux_ref).astype(o_ref.dtype)


def _matmul(h, w, aux, epilogue, out_dtype, tm, tn, name):
    n, d = h.shape
    cols = w.shape[1]
    tm = min(tm, n)
    tn = min(tn, cols)
    assert n % tm == 0 and cols % tn == 0
    return pl.pallas_call(
        functools.partial(_matmul_kernel, epilogue=epilogue),
        grid=(n // tm, cols // tn),
        in_specs=[
            pl.BlockSpec((tm, d), lambda i, j: (i, 0)),
            pl.BlockSpec((d, tn), lambda i, j: (0, j)),
            pl.BlockSpec((8, tn), lambda i, j: (0, j)),
        ],
        out_specs=pl.BlockSpec((tm, tn), lambda i, j: (i, j)),
        out_shape=jax.ShapeDtypeStruct((n, cols), out_dtype),
        compiler_params=_cparams(("parallel", "parallel")),
        name=name,
    )(h, w, aux)


def _aux_rows(row):
    return jnp.zeros((8, row.shape[0]), F32).at[0].set(row.astype(F32))


def _mlstm_kernel(qk_ref, v_ref, gate_ref, so_ref, cw_ref, cb_ref, ng_ref, o_ref,
                  xprev_ref, c_ref, n_ref, m_ref):
    L = CHUNK
    dk, dv, nh = M_DQK, M_DV, M_HEADS

    @pl.when(pl.program_id(1) == 0)
    def _():
        xprev_ref[...] = jnp.zeros_like(xprev_ref)
        c_ref[...] = jnp.zeros_like(c_ref)
        n_ref[...] = jnp.zeros_like(n_ref)
        m_ref[...] = jnp.zeros_like(m_ref)

    xb = qk_ref[...]
    xx = jnp.concatenate([xprev_ref[...], xb], axis=0)
    n_sh = CONV_K - 1
    srow = lax.broadcasted_iota(jnp.int32, (n_sh * L, 2 * L), 0)
    scol = lax.broadcasted_iota(jnp.int32, (n_sh * L, 2 * L), 1)
    sel = (scol == (srow % L) + L - 1 - srow // L).astype(BF16)
    shifted = jnp.dot(sel, xx, preferred_element_type=F32)
    xprev_ref[...] = xb
    y = cb_ref[0:1, :] + cw_ref[CONV_K - 1:CONV_K, :] * xb.astype(F32)
    for s in range(n_sh):
        y = y + cw_ref[CONV_K - 2 - s:CONV_K - 1 - s, :] * shifted[s * L:(s + 1) * L, :]
    qk = y * (0.5 * jnp.tanh(0.5 * y) + 0.5)

    g = gate_ref[...]
    lf = jnp.minimum(g, 0.0) - jnp.log1p(jnp.exp(-jnp.abs(g)))
    rr = lax.broadcasted_iota(jnp.int32, (L, L), 0)
    cc = lax.broadcasted_iota(jnp.int32, (L, L), 1)
    causal = cc <= rr
    tri = causal.astype(BF16)
    lf_hi = lf.astype(BF16)
    lf_lo = (lf - lf_hi.astype(F32)).astype(BF16)
    bcum = (jnp.dot(tri, lf_hi, preferred_element_type=F32)
            + jnp.dot(tri, lf_lo, preferred_element_type=F32))
    bcum_t = bcum.T
    g_t = g.T

    for h in range(nh):
        bcol = bcum[:, nh + h:nh + h + 1]
        brow = bcum_t[nh + h:nh + h + 1, :]
        irow = g_t[h:h + 1, :]
        icol = g[:, h:h + 1]
        m_old = m_ref[h]
        c_old = c_ref[h]
        n_old = n_ref[h]

        qh = qk[:, h * dk:(h + 1) * dk] * (dk ** -0.5)
        kh = qk[:, M_QK_W + h * dk:M_QK_W + (h + 1) * dk]
        vh = v_ref[:, h * dv:(h + 1) * dv]
        qb = qh.astype(BF16)
        kb = kh.astype(BF16)

        logd = jnp.where(causal, bcol - brow + irow, -jnp.inf)
        inter = bcol + m_old
        m_loc = jnp.maximum(inter, jnp.max(logd, axis=-1, keepdims=True))
        w_intra = jnp.exp(logd - m_loc)
        w_inter = jnp.exp(inter - m_loc)
        s = lax.dot_general(qb, kb, (((1,), (1,)), ((), ())), preferred_element_type=F32) * w_intra
        num = (jnp.dot(s.astype(BF16), vh, preferred_element_type=F32)
               + w_inter * jnp.dot(qb, c_old.astype(BF16), preferred_element_type=F32))
        den = (jnp.sum(s, axis=-1, keepdims=True)
               + w_inter * jnp.sum(qh * n_old, axis=-1, keepdims=True))
        hh = num / jnp.maximum(jnp.abs(den), jnp.exp(-m_loc))

        b_end = bcol[L - 1:L, :]
        log_w = b_end - bcol + icol
        m_new = jnp.maximum(b_end + m_old, jnp.max(log_w, axis=0, keepdims=True))
        w_s = jnp.exp(log_w - m_new)
        decay = jnp.exp(b_end + m_old - m_new)
        kw = kh * w_s
        c_ref[h] = decay * c_old + lax.dot_general(
            kw.astype(BF16), vh, (((0,), (0,)), ((), ())), preferred_element_type=F32)
        n_ref[h] = decay * n_old + jnp.sum(kw, axis=0, keepdims=True)
        m_ref[h] = m_new

        yn = hh * lax.rsqrt(jnp.mean(hh * hh, axis=-1, keepdims=True) + EPS)
        yn = yn * ng_ref[0:1, h * dv:(h + 1) * dv]
        o_ref[:, h * dv:(h + 1) * dv] = (so_ref[:, h * dv:(h + 1) * dv].astype(F32) * yn).astype(o_ref.dtype)


def _mlstm(qk, vproj, v_col, gates, so, so_col, conv_w, conv_b, norm_g, bsz, seq):
    n = bsz * seq
    nc = seq // CHUNK
    L = CHUNK
    return pl.pallas_call(
        _mlstm_kernel,
        grid=(bsz, nc),
        in_specs=[
            pl.BlockSpec((L, 2 * M_QK_W), lambda b, c: (b * nc + c, 0)),
            pl.BlockSpec((L, M_V_W), lambda b, c: (b * nc + c, v_col)),
            pl.BlockSpec((L, LANES), lambda b, c: (b * nc + c, 0)),
            pl.BlockSpec((L, M_V_W), lambda b, c: (b * nc + c, so_col)),
            pl.BlockSpec((CONV_K, 2 * M_QK_W), lambda b, c: (0, 0)),
            pl.BlockSpec((8, 2 * M_QK_W), lambda b, c: (0, 0)),
            pl.BlockSpec((8, M_V_W), lambda b, c: (0, 0)),
        ],
        out_specs=pl.BlockSpec((L, M_V_W), lambda b, c: (b * nc + c, 0)),
        out_shape=jax.ShapeDtypeStruct((n, M_V_W), BF16),
        scratch_shapes=[
            pltpu.VMEM((L, 2 * M_QK_W), BF16),
            pltpu.VMEM((M_HEADS, M_DQK, M_DV), F32),
            pltpu.VMEM((M_HEADS, 1, M_DQK), F32),
            pltpu.VMEM((M_HEADS, 1, 1), F32),
        ],
        compiler_params=_cparams(("parallel", "arbitrary")),
        name="mlstm",
    )(qk, vproj, gates, so, conv_w.astype(F32), _aux_rows(conv_b), _aux_rows(norm_g))


def _dattn_load_tile(q_ref, v_ref, vt_ref, seq):
    tv = 256

    @pl.when(pl.program_id(2) == 0)
    def _():
        for c in range(seq // tv):
            blk = v_ref[c * tv:(c + 1) * tv, :].astype(F32)
            vt_ref[:, c * tv:(c + 1) * tv] = blk.T.astype(BF16)

    qt = q_ref[...].astype(F32).T
    row = lax.broadcasted_iota(jnp.int32, qt.shape, 0)
    return (jnp.where(row < D_HD, qt, 0.0).astype(BF16), jnp.where(row >= D_HD, qt, 0.0).astype(BF16))


def _dattn_finish(lam_ref, sg_ref, o_ref, acc_ref, l1, l2, out_scale):
    ot = acc_ref[0] * (1.0 / l1) - acc_ref[1] * (lam_ref[0] / l2)
    yt = ot * lax.rsqrt(jnp.mean(ot * ot, axis=0, keepdims=True) + EPS)
    o_ref[...] = (yt.T * (sg_ref[0:1, :] * out_scale)).astype(o_ref.dtype)


def _causal_mask(s):
    kr = lax.broadcasted_iota(jnp.int32, s.shape, 0)
    qc = lax.broadcasted_iota(jnp.int32, s.shape, 1)
    return jnp.where(kr <= qc, s, NEG_BIG)


def _dattn_kernel(lam_ref, q_ref, k_ref, v_ref, sg_ref, o_ref, vt_ref, acc_ref, qz_ref, l_ref, p_ref, *,
                  seq, tq, tk, out_scale):
    n_diag = tq // tk
    assert n_diag * tk == tq and n_diag % 2 == 0
    qi = pl.program_id(2)
    qz = _dattn_load_tile(q_ref, v_ref, vt_ref, seq)
    qz_ref[0] = qz[0]
    qz_ref[1] = qz[1]
    acc_ref[...] = jnp.zeros_like(acc_ref)
    l_ref[...] = jnp.zeros_like(l_ref)

    def scores(j, lo, slot):
        q0 = lo or 0
        kj = k_ref[pl.ds(pl.multiple_of(j * tk, tk), tk), :]
        for c in range(2):
            s = jnp.dot(kj, qz_ref[c, :, q0:], preferred_element_type=F32)
            if lo is not None:
                s = _causal_mask(s)
            p = jnp.exp2(s)
            l_ref[c, :, q0:] += jnp.sum(p, axis=0, keepdims=True)
            p_ref[slot, c, :, q0:] = p.astype(BF16)

    def accumulate(j, lo, slot):
        q0 = lo or 0
        vtj = vt_ref[:, pl.ds(pl.multiple_of(j * tk, tk), tk)]
        for c in range(2):
            acc_ref[c, :, q0:] += jnp.dot(vtj, p_ref[slot, c, :, q0:], preferred_element_type=F32)

    n = n_diag * qi

    @pl.when(qi > 0)
    def _():
        scores(0, None, 0)

    def pair(i, carry):
        j = 2 * i
        accumulate(j, None, 0)
        scores(j + 1, None, 1)
        accumulate(j + 1, None, 1)
        scores(j + 2, None, 0)
        return carry

    lax.fori_loop(0, n // 2 - 1, pair, 0)

    @pl.when(qi > 0)
    def _():
        accumulate(n - 2, None, 0)
        scores(n - 1, None, 1)
        accumulate(n - 1, None, 1)
        scores(n, 0, 0)

    @pl.when(qi == 0)
    def _():
        scores(n, 0, 0)

    for d in range(n_diag):
        accumulate(n + d, d * tk, d % 2)
        if d + 1 < n_diag:
            scores(n + d + 1, (d + 1) * tk, (d + 1) % 2)
    _dattn_finish(lam_ref, sg_ref, o_ref, acc_ref, l_ref[0], l_ref[1], out_scale)


def _dattn_stabilised_kernel(lam_ref, q_ref, k_ref, v_ref, sg_ref, o_ref, vt_ref, acc_ref, *,
                             seq, tq, tk, out_scale):
    qi = pl.program_id(2)
    qz = _dattn_load_tile(q_ref, v_ref, vt_ref, seq)
    acc_ref[...] = jnp.zeros_like(acc_ref)

    def chunk(j, carry, lo):
        start = pl.multiple_of(j * tk, tk)
        kj = k_ref[pl.ds(start, tk), :]
        vtj = vt_ref[:, pl.ds(start, tk)]
        q0 = lo or 0
        out = []
        for c in range(2):
            m, l = carry[2 * c][:, q0:], carry[2 * c + 1][:, q0:]
            s = jnp.dot(kj, qz[c][:, q0:], preferred_element_type=F32)
            if lo is not None:
                s = _causal_mask(s)
            m_new = jnp.maximum(m, jnp.max(s, axis=0, keepdims=True))
            p = jnp.exp2(s - m_new)
            alpha = jnp.exp2(m - m_new)
            l_new = alpha * l + jnp.sum(p, axis=0, keepdims=True)
            acc_ref[c, :, q0:] = alpha * acc_ref[c, :, q0:] + jnp.dot(
                vtj, p.astype(BF16), preferred_element_type=F32)
            if q0:
                m_new = jnp.concatenate([carry[2 * c][:, :q0], m_new], axis=1)
                l_new = jnp.concatenate([carry[2 * c + 1][:, :q0], l_new], axis=1)
            out += [m_new, l_new]
        return tuple(out)

    init = (jnp.full((1, tq), NEG_BIG, F32), jnp.zeros((1, tq), F32)) * 2
    n_full = qi * (tq // tk)
    carry = lax.fori_loop(0, n_full, lambda j, c: chunk(j, c, None), init)
    for d in range(tq // tk):
        carry = chunk(n_full + d, carry, d * tk)

    _dattn_finish(lam_ref, sg_ref, o_ref, acc_ref, carry[1], carry[3], out_scale)


def _diff_attention(lam, qk, v, v_col0, subln_g, bsz, seq, out_scale, stabilised, tq, tk):
    n = bsz * seq
    tq = min(tq, seq)
    tk = min(tk, tq)
    nq = seq // tq
    hw = 2 * D_HD
    scratch = [pltpu.VMEM((D_DV, seq), BF16), pltpu.VMEM((2, D_DV, tq), F32)]
    if stabilised:
        body = _dattn_stabilised_kernel
    else:
        body = _dattn_kernel
        scratch += [pltpu.VMEM((2, hw, tq), BF16), pltpu.VMEM((2, 1, tq), F32), pltpu.VMEM((2, 2, tk, tq), BF16)]
    return pl.pallas_call(
        functools.partial(body, seq=seq, tq=tq, tk=tk, out_scale=out_scale),
        grid=(bsz, D_HEADS, nq),
        in_specs=[
            pl.BlockSpec(memory_space=pltpu.SMEM),
            pl.BlockSpec((tq, hw), lambda b, h, i: (b * nq + i, h)),
            pl.BlockSpec((seq, hw), lambda b, h, i: (b, D_HEADS + h)),
            pl.BlockSpec((seq, D_DV), lambda b, h, i: (b, v_col0 + h)),
            pl.BlockSpec((8, D_DV), lambda b, h, i: (0, 0)),
        ],
        out_specs=pl.BlockSpec((tq, D_DV), lambda b, h, i: (b * nq + i, h)),
        out_shape=jax.ShapeDtypeStruct((n, D_V_W), BF16),
        scratch_shapes=scratch,
        compiler_params=_cparams(("parallel", "parallel", "arbitrary")),
        name="diff_attn_stabilised" if stabilised else "diff_attn",
    )(lam.reshape(1).astype(F32), qk, qk, v, _aux_rows(subln_g))


def _xattn_kernel(q_ref, mk_ref, mv_ref, o_ref):
    for h in range(C_HEADS):
        q = q_ref[:, h * C_DQK:(h + 1) * C_DQK]
        k = mk_ref[:, h * C_DQK:(h + 1) * C_DQK]
        s = lax.dot_general(q, k, (((1,), (1,)), ((), ())), preferred_element_type=F32)
        m = jnp.max(s, axis=-1, keepdims=True)
        p = jnp.exp2(s - m)
        l = jnp.sum(p, axis=-1, keepdims=True)
        o = jnp.dot(p.astype(BF16), mv_ref[:, h * C_DV:(h + 1) * C_DV], preferred_element_type=F32)
        o_ref[:, h * C_DV:(h + 1) * C_DV] = (o / l).astype(o_ref.dtype)


def _cross_attention(q, mk, mv, bsz, seq, mem_len, tq=512):
    n = bsz * seq
    tq = min(tq, seq)
    nq = seq // tq
    return pl.pallas_call(
        _xattn_kernel,
        grid=(bsz, nq),
        in_specs=[
            pl.BlockSpec((tq, C_Q_W), lambda b, i: (b * nq + i, 0)),
            pl.BlockSpec((mem_len, C_Q_W), lambda b, i: (b, 0)),
            pl.BlockSpec((mem_len, C_V_W), lambda b, i: (b, 0)),
        ],
        out_specs=pl.BlockSpec((tq, C_V_W), lambda b, i: (b * nq + i, 0)),
        out_shape=jax.ShapeDtypeStruct((n, C_V_W), BF16),
        compiler_params=_cparams(("parallel", "parallel")),
        name="cross_attn",
    )(q, mk, mv)


def _merge_kernel(x_ref, ym_ref, yd_ref, yc_ref, gm_ref, gd_ref, gc_ref,
                  wm_ref, wd_ref, wc_ref, wo_ref, g_ref, o_ref, h_ref):
    merged = gm_ref[...].astype(F32) * jnp.dot(ym_ref[...], wm_ref[...], preferred_element_type=F32)
    merged = merged + gd_ref[...].astype(F32) * jnp.dot(yd_ref[...], wd_ref[...], preferred_element_type=F32)
    merged = merged + gc_ref[...].astype(F32) * jnp.dot(yc_ref[...], wc_ref[...], preferred_element_type=F32)
    x1 = x_ref[...] + jnp.dot(merged.astype(BF16), wo_ref[...], preferred_element_type=F32)
    o_ref[...] = x1
    y = x1 * lax.rsqrt(jnp.mean(x1 * x1, axis=-1, keepdims=True) + EPS)
    h_ref[...] = (y * g_ref[...]).astype(BF16)


def _merge(x, ym, yd, yc, gates, g_col0, wm, wd, wc, wo, g_next, tm):
    n, d = x.shape
    tm = min(tm, n)
    row = lambda i: (i, 0)
    wspec = pl.BlockSpec((d, d), lambda i: (0, 0), pipeline_mode=pl.Buffered(1))
    return pl.pallas_call(
        _merge_kernel,
        grid=(n // tm,),
        in_specs=[
            pl.BlockSpec((tm, d), row),
            pl.BlockSpec((tm, d), row),
            pl.BlockSpec((tm, d), row),
            pl.BlockSpec((tm, d), row),
            pl.BlockSpec((tm, d), lambda i: (i, g_col0)),
            pl.BlockSpec((tm, d), lambda i: (i, g_col0 + 1)),
            pl.BlockSpec((tm, d), lambda i: (i, g_col0 + 2)),
            wspec, wspec, wspec, wspec,
            pl.BlockSpec((1, d), lambda i: (0, 0)),
        ],
        out_specs=[pl.BlockSpec((tm, d), row), pl.BlockSpec((tm, d), row)],
        out_shape=[jax.ShapeDtypeStruct((n, d), F32), jax.ShapeDtypeStruct((n, d), BF16)],
        compiler_params=_cparams(("parallel",)),
        name="merge",
    )(x, ym, yd, yc, gates, gates, gates, wm, wd, wc, wo, g_next.reshape(1, d).astype(F32))


def _mlp_kernel(x_ref, h_ref, wu_ref, wd_ref, o_ref):
    @pl.when(pl.program_id(1) == 0)
    def _():
        o_ref[...] = x_ref[...]

    u = jnp.maximum(jnp.dot(h_ref[...], wu_ref[...], preferred_element_type=F32), 0.0)
    o_ref[...] += jnp.dot((u * u).astype(BF16), wd_ref[...], preferred_element_type=F32)


def _mlp(x, h, wu, wd, tm, tf):
    n, d = x.shape
    dff = wu.shape[1]
    tm = min(tm, n)
    return pl.pallas_call(
        _mlp_kernel,
        grid=(n // tm, dff // tf),
        in_specs=[
            pl.BlockSpec((tm, d), lambda i, f: (i, 0)),
            pl.BlockSpec((tm, d), lambda i, f: (i, 0)),
            pl.BlockSpec((d, tf), lambda i, f: (0, f)),
            pl.BlockSpec((tf, d), lambda i, f: (f, 0)),
        ],
        out_specs=pl.BlockSpec((tm, d), lambda i, f: (i, 0)),
        out_shape=jax.ShapeDtypeStruct((n, d), F32),
        compiler_params=_cparams(("parallel", "arbitrary")),
        name="mlp",
    )(x, h, wu, wd)


def _lambda_init(layer):
    return 0.8 - 0.6 * math.exp(-0.3 * layer)


def _layer(l, x2, mem2, bsz, seq, mem_len, p):
    split = [0]
    for w in (M_QK_W, M_QK_W, M_V_W, M_HEADS, M_HEADS, M_V_W, D_Q_W, D_Q_W, D_V_W, C_Q_W):
        split.append(split[-1] + w)
    w_in = p['w_in'][l]
    cols = lambda a, b: w_in[:, split[a]:split[b]]
    g_mix = p['norm_mix_g'][l]

    mqk, h = _norm_matmul(x2, g_mix, cols(0, 2).astype(BF16), jnp.zeros((8, 2 * M_QK_W), F32),
                          _ep_identity, BF16, 1024, 1024, "proj_mqk")
    w_if = jnp.zeros((D_MODEL, LANES), F32).at[:, :2 * M_HEADS].set(cols(3, 5)).astype(BF16)
    b_if = jnp.zeros((LANES,), F32).at[:2 * M_HEADS].set(jnp.concatenate([p['b_igate'][l], p['b_fgate'][l]]))
    gates_if = _matmul(h, w_if, _aux_rows(b_if), _ep_bias, F32, 2048, LANES, "proj_if")
    w_v = jnp.concatenate([cols(2, 3), cols(8, 9)], axis=1).astype(BF16)
    vproj = _matmul(h, w_v, jnp.zeros((8, w_v.shape[1]), F32), _ep_identity, BF16, PROJ_TM, PROJ_TN, "proj_v")
    w_sg = (0.5 * jnp.concatenate([cols(5, 6), p['w_gate'][l]], axis=1)).astype(BF16)
    b_sg = 0.5 * jnp.concatenate([jnp.zeros((M_V_W,), F32), p['b_gate'][l].astype(F32)])
    sgates = _matmul(h, w_sg, _aux_rows(b_sg), _ep_sigmoid_of_double, BF16, PROJ_TM, PROJ_TN, "proj_gates")
    w_dqk = cols(6, 8).astype(BF16)
    q_gain = jnp.tile(p['dq_norm_g'][l].astype(F32), D_Q_W // D_HD) * (D_HD ** -0.5 * LOG2E)
    k_gain = jnp.tile(p['dk_norm_g'][l].astype(F32), D_Q_W // D_HD)
    dqk = _matmul(h, w_dqk, _aux_rows(jnp.concatenate([q_gain, k_gain])),
                  functools.partial(_ep_group_norm, D_HD), BF16, PROJ_NORM_TM, PROJ_TN, "proj_dqk")
    cq_gain = jnp.tile(p['cq_norm_g'][l].astype(F32), C_HEADS) * (C_DQK ** -0.5 * LOG2E)
    cq = _matmul(h, cols(9, 10).astype(BF16), _aux_rows(cq_gain),
                 functools.partial(_ep_group_norm, C_DQK), BF16, PROJ_NORM_TM, C_Q_W, "proj_cq")
    w_kv = p['w_mem_kv'][l]
    ck_gain = jnp.tile(p['ck_norm_g'][l].astype(F32), C_HEADS)
    mk, hmem = _norm_matmul(mem2, p['mem_norm_g'][l], w_kv[:, :C_Q_W].astype(BF16), _aux_rows(ck_gain),
                            functools.partial(_ep_group_norm, C_DQK), BF16, 1024, C_Q_W, "proj_mk")
    mv = _matmul(hmem, w_kv[:, C_Q_W:].astype(BF16), jnp.zeros((8, C_V_W), F32),
                 _ep_identity, BF16, 1024, C_V_W, "proj_mv")

    y_m = _mlstm(mqk, vproj, 0, gates_if, sgates, 0, p['conv_w'][l], p['conv_b'][l], p['m_norm_g'][l], bsz, seq)

    lam_i = _lambda_init(l)
    lam = (jnp.exp(jnp.sum(p['lam_q1'][l].astype(F32) * p['lam_k1'][l].astype(F32)))
           - jnp.exp(jnp.sum(p['lam_q2'][l].astype(F32) * p['lam_k2'][l].astype(F32))) + lam_i)
    score_bound = 1.02 * D_HD * jnp.max(jnp.abs(q_gain)) * jnp.max(jnp.abs(k_gain))
    dattn = functools.partial(_diff_attention, lam, dqk, vproj, M_V_W // D_DV, p['subln_g'][l], bsz, seq,
                              1.0 - lam_i)
    y_d = lax.cond(score_bound <= MAX_UNSTABILISED_SCORE,
                   lambda: dattn(False, DATTN_TQ_FAST, DATTN_TK_FAST),
                   lambda: dattn(True, DATTN_TQ_STABILISED, DATTN_TK_STABILISED))

    y_c = _cross_attention(cq, mk, mv, bsz, seq, mem_len)

    x2, h2 = _merge(x2, y_m, y_d, y_c, sgates, 1,
                    p['w_proj_m'][l].astype(BF16), p['w_proj_d'][l].astype(BF16),
                    p['w_proj_c'][l].astype(BF16), p['w_out'][l].astype(BF16), p['norm_mlp_g'][l], MERGE_TM)
    return _mlp(x2, h2, p['w_up'][l].astype(BF16), p['w_down'][l].astype(BF16), MLP_TM, MLP_TF)


def kernel(x, mem, norm_mix_g, w_in, b_igate, b_fgate, conv_w, conv_b, m_norm_g, dq_norm_g, dk_norm_g, lam_q1, lam_k1, lam_q2, lam_k2, subln_g, cq_norm_g, ck_norm_g, mem_norm_g, w_mem_kv, w_gate, b_gate, w_proj_m, w_proj_d, w_proj_c, w_out, norm_mlp_g, w_up, w_down):
    p = dict(norm_mix_g=norm_mix_g, w_in=w_in, b_igate=b_igate, b_fgate=b_fgate, conv_w=conv_w, conv_b=conv_b,
             m_norm_g=m_norm_g, dq_norm_g=dq_norm_g, dk_norm_g=dk_norm_g, lam_q1=lam_q1, lam_k1=lam_k1,
             lam_q2=lam_q2, lam_k2=lam_k2, subln_g=subln_g, cq_norm_g=cq_norm_g, ck_norm_g=ck_norm_g,
             mem_norm_g=mem_norm_g, w_mem_kv=w_mem_kv, w_gate=w_gate, b_gate=b_gate, w_proj_m=w_proj_m,
             w_proj_d=w_proj_d, w_proj_c=w_proj_c, w_out=w_out, norm_mlp_g=norm_mlp_g, w_up=w_up, w_down=w_down)
    bsz, seq, d = x.shape
    mem_len = mem.shape[1]
    x2 = x.reshape(bsz * seq, d)
    mem2 = mem.reshape(bsz * mem_len, d)
    for l in range(w_in.shape[0]):
        x2 = _layer(l, x2, mem2, bsz, seq, mem_len, p)
    return x2.reshape(bsz, seq, d)
```

```python
import functools
import math

import jax
import jax.numpy as jnp
from jax import lax
from jax.experimental import pallas as pl
from jax.experimental.pallas import tpu as pltpu

F32 = jnp.float32
BF16 = jnp.bfloat16

EPS = 1e-6
LOG2E = math.log2(math.e)
NEG_BIG = -1e30

D_MODEL = 1024
MEM_LEN = 256
M_HEADS, M_DQK, M_DV = 4, 128, 256
CHUNK = 128
CONV_K = 4
D_HEADS, D_HD = 8, 64
D_DV = 2 * D_HD
C_HEADS, C_DQK, C_DV = 4, 128, 256
D_FF = 4 * D_MODEL
N_BRANCH = 3
LANES = 128
MXU_DIM = 256

M_QK_W = M_HEADS * M_DQK
M_V_W = M_HEADS * M_DV
D_Q_W = D_HEADS * 2 * D_HD
D_V_W = D_HEADS * D_DV
C_Q_W = C_HEADS * C_DQK
C_V_W = C_HEADS * C_DV

VMEM_LIMIT = 56 * 1024 * 1024

PROJ_TM = 2048
PROJ_NORM_TM = 1024
PROJ_TN = 1024
MERGE_TM = 512
MLP_TM = 1024
MLP_TF = 1024
DATTN_TQ_FAST = 4096
DATTN_TQ_STABILISED = 1024
DATTN_TK_FAST = 512
DATTN_TK_STABILISED = 256
MAX_UNSTABILISED_SCORE = 60.0


def _cparams(sem):
    return pltpu.CompilerParams(dimension_semantics=sem, vmem_limit_bytes=VMEM_LIMIT)


def _ep_identity(acc, aux_ref):
    return acc


def _ep_sigmoid_of_double(acc, aux_ref):
    return 0.5 * jnp.tanh(acc + aux_ref[0:1, :]) + 0.5


def _ep_bias(acc, aux_ref):
    return acc + aux_ref[0:1, :]


def _ep_group_norm(group, acc, aux_ref):
    tn = acc.shape[1]
    cw = MXU_DIM
    r = lax.broadcasted_iota(jnp.int32, (cw, cw), 0) // group
    c = lax.broadcasted_iota(jnp.int32, (cw, cw), 1) // group
    gmat = (r == c).astype(BF16)
    outs = []
    for s in range(tn // cw):
        a = acc[:, s * cw:(s + 1) * cw]
        ss = jnp.dot((a * a).astype(BF16), gmat, preferred_element_type=F32)
        outs.append(a * lax.rsqrt(ss * (1.0 / group) + EPS))
    y = jnp.concatenate(outs, axis=1) if len(outs) > 1 else outs[0]
    return y * aux_ref[0:1, :]


def _norm_matmul_kernel(x_ref, g_ref, w_ref, aux_ref, o_ref, h_ref, *, epilogue):
    @pl.when(pl.program_id(1) == 0)
    def _():
        x = x_ref[...].astype(F32)
        y = x * lax.rsqrt(jnp.mean(x * x, axis=-1, keepdims=True) + EPS)
        h_ref[...] = (y * g_ref[...]).astype(BF16)

    acc = jnp.dot(h_ref[...], w_ref[...], preferred_element_type=F32)
    o_ref[...] = epilogue(acc, aux_ref).astype(o_ref.dtype)


def _norm_matmul(x, g, w, aux, window, epilogue, out_dtype, tm, tn, name):
    n, d = x.shape
    c0, cols = window
    tm = min(tm, n)
    tn = min(tn, cols)
    assert n % tm == 0 and cols % tn == 0 and c0 % tn == 0
    j0 = c0 // tn
    return pl.pallas_call(
        functools.partial(_norm_matmul_kernel, epilogue=epilogue),
        grid=(n // tm, cols // tn),
        in_specs=[
            pl.BlockSpec((tm, d), lambda i, j: (i, 0)),
            pl.BlockSpec((1, d), lambda i, j: (0, 0)),
            pl.BlockSpec((d, tn), lambda i, j: (0, j0 + j)),
            pl.BlockSpec((8, tn), lambda i, j: (0, j0 + j)),
        ],
        out_specs=[pl.BlockSpec((tm, tn), lambda i, j: (i, j)),
                   pl.BlockSpec((tm, d), lambda i, j: (i, 0))],
        out_shape=[jax.ShapeDtypeStruct((n, cols), out_dtype), jax.ShapeDtypeStruct((n, d), BF16)],
        compiler_params=_cparams(("parallel", "arbitrary")),
        name=name,
    )(x, g.reshape(1, d).astype(F32), w, aux)


def _matmul_kernel(h_ref, w_ref, aux_ref, o_ref, *, epilogue):
    acc = jnp.dot(h_ref[...], w_ref[...], preferred_element_type=F32)
    o_ref[...] = epilogue(acc, aux_ref).astype(o_ref.dtype)


def _matmul(h, w, aux, window, epilogue, out_dtype, tm, tn, name):
    n, d = h.shape
    c0, cols = window
    tm = min(tm, n)
    tn = min(tn, cols)
    assert n % tm == 0 and cols % tn == 0 and c0 % tn == 0
    j0 = c0 // tn
    return pl.pallas_call(
        functools.partial(_matmul_kernel, epilogue=epilogue),
        grid=(n // tm, cols // tn),
        in_specs=[
            pl.BlockSpec((tm, d), lambda i, j: (i, 0)),
            pl.BlockSpec((d, tn), lambda i, j: (0, j0 + j)),
            pl.BlockSpec((8, tn), lambda i, j: (0, j0 + j)),
        ],
        out_specs=pl.BlockSpec((tm, tn), lambda i, j: (i, j)),
        out_shape=jax.ShapeDtypeStruct((n, cols), out_dtype),
        compiler_params=_cparams(("parallel", "parallel")),
        name=name,
    )(h, w, aux)


def _pack_w_in_kernel(w_ref, wg_ref, o_ref, *, split):
    w = w_ref[...]
    col = lambda a, b: w[:, split[a]:split[b]]
    parts = []
    for h in range(M_HEADS):
        parts += [col(0, 1)[:, h * M_DQK:(h + 1) * M_DQK], col(1, 2)[:, h * M_DQK:(h + 1) * M_DQK]]
    parts += [col(2, 3), col(8, 9), 0.5 * col(5, 6), 0.5 * wg_ref[...], col(6, 8), col(9, 10), col(3, 5),
              jnp.zeros((w.shape[0], LANES - 2 * M_HEADS), w.dtype)]
    o_ref[...] = jnp.concatenate(parts, axis=1).astype(BF16)


def _pack_w_in(w_in, w_gate, split, tr=256):
    d, cin = w_in.shape
    cout = cin - 2 * M_HEADS + LANES + w_gate.shape[1]
    return pl.pallas_call(
        functools.partial(_pack_w_in_kernel, split=split),
        grid=(d // tr,),
        in_specs=[pl.BlockSpec((tr, cin), lambda i: (i, 0)),
                  pl.BlockSpec((tr, w_gate.shape[1]), lambda i: (i, 0))],
        out_specs=pl.BlockSpec((tr, cout), lambda i: (i, 0)),
        out_shape=jax.ShapeDtypeStruct((d, cout), BF16),
        compiler_params=_cparams(("parallel",)),
        name="pack_w_in",
    )(w_in, w_gate)


def _aux_rows(row):
    return jnp.zeros((8, row.shape[0]), F32).at[0].set(row.astype(F32))


def _mlstm_kernel(qk_ref, v_ref, gate_ref, so_ref, cw_ref, cb_ref, ng_ref, o_ref,
                  xprev_ref, c_ref, n_ref, m_ref):
    L = CHUNK
    dk, dv, nh = M_DQK, M_DV, M_HEADS

    @pl.when(pl.program_id(1) == 0)
    def _():
        xprev_ref[...] = jnp.zeros_like(xprev_ref)
        c_ref[...] = jnp.zeros_like(c_ref)
        n_ref[...] = jnp.zeros_like(n_ref)
        m_ref[...] = jnp.zeros_like(m_ref)

    n_sh = CONV_K - 1
    srow = lax.broadcasted_iota(jnp.int32, (n_sh * L, 2 * L), 0)
    scol = lax.broadcasted_iota(jnp.int32, (n_sh * L, 2 * L), 1)
    sel = (scol == (srow % L) + L - 1 - srow // L).astype(BF16)
    xb = qk_ref[...]
    xx = jnp.concatenate([xprev_ref[...], xb], axis=0)
    shifted = jnp.dot(sel, xx, preferred_element_type=F32)
    xprev_ref[...] = xb
    z = cb_ref[0:1, :] + cw_ref[CONV_K - 1:CONV_K, :] * xb.astype(F32)
    for s in range(n_sh):
        z = z + cw_ref[CONV_K - 2 - s:CONV_K - 1 - s, :] * shifted[s * L:(s + 1) * L, :]
    act = z + z * jnp.tanh(z)

    g = gate_ref[...]
    lf = jnp.minimum(g, 0.0) - jnp.log1p(jnp.exp(-jnp.abs(g)))
    rr = lax.broadcasted_iota(jnp.int32, (L, L), 0)
    cc = lax.broadcasted_iota(jnp.int32, (L, L), 1)
    causal = cc <= rr
    tri = causal.astype(BF16)
    lf_hi = lf.astype(BF16)
    lf_lo = (lf - lf_hi.astype(F32)).astype(BF16)
    bcum = (jnp.dot(tri, lf_hi, preferred_element_type=F32)
            + jnp.dot(tri, lf_lo, preferred_element_type=F32))
    bcum_t = bcum.T
    g_t = g.T

    heads = range(nh)
    bcol = [bcum[:, nh + h:nh + h + 1] for h in heads]
    brow = [bcum_t[nh + h:nh + h + 1, :] for h in heads]
    irow = [g_t[h:h + 1, :] for h in heads]
    icol = [g[:, h:h + 1] for h in heads]
    m_old = [m_ref[h] for h in heads]
    c_old = [c_ref[h] for h in heads]
    n_old = [n_ref[h] for h in heads]
    qh = [act[:, 2 * dk * h:2 * dk * h + dk] * (dk ** -0.5) for h in heads]
    kh = [act[:, 2 * dk * h + dk:2 * dk * (h + 1)] for h in heads]
    vh = [v_ref[:, h * dv:(h + 1) * dv] for h in heads]
    qb = [q.astype(BF16) for q in qh]
    kb = [k.astype(BF16) for k in kh]

    qk_t = [lax.dot_general(qb[h], kb[h], (((1,), (1,)), ((), ())), preferred_element_type=F32) for h in heads]
    q_c = [jnp.dot(qb[h], c_old[h].astype(BF16), preferred_element_type=F32) for h in heads]
    logd = [jnp.where(causal, bcol[h] - brow[h] + irow[h], -jnp.inf) for h in heads]
    inter = [bcol[h] + m_old[h] for h in heads]
    m_loc = [jnp.maximum(inter[h], jnp.max(logd[h], axis=-1, keepdims=True)) for h in heads]
    w_inter = [jnp.exp(inter[h] - m_loc[h]) for h in heads]
    s = [qk_t[h] * jnp.exp(logd[h] - m_loc[h]) for h in heads]
    num = [jnp.dot(s[h].astype(BF16), vh[h], preferred_element_type=F32) + w_inter[h] * q_c[h] for h in heads]
    den = [jnp.sum(s[h], axis=-1, keepdims=True) + w_inter[h] * jnp.sum(qh[h] * n_old[h], axis=-1, keepdims=True)
           for h in heads]

    b_end = [bcol[h][L - 1:L, :] for h in heads]
    log_w = [b_end[h] - bcol[h] + icol[h] for h in heads]
    m_new = [jnp.maximum(b_end[h] + m_old[h], jnp.max(log_w[h], axis=0, keepdims=True)) for h in heads]
    kw = [kh[h] * jnp.exp(log_w[h] - m_new[h]) for h in heads]
    decay = [jnp.exp(b_end[h] + m_old[h] - m_new[h]) for h in heads]
    for h in heads:
        c_ref[h] = decay[h] * c_old[h] + lax.dot_general(
            kw[h].astype(BF16), vh[h], (((0,), (0,)), ((), ())), preferred_element_type=F32)
        n_ref[h] = decay[h] * n_old[h] + jnp.sum(kw[h], axis=0, keepdims=True)
        m_ref[h] = m_new[h]

    for h in heads:
        d = jnp.maximum(jnp.abs(den[h]), jnp.exp(-m_loc[h]))
        row_scale = lax.rsqrt(jnp.mean(num[h] * num[h], axis=-1, keepdims=True) + EPS * (d * d))
        yn = num[h] * row_scale * ng_ref[0:1, h * dv:(h + 1) * dv]
        o_ref[:, h * dv:(h + 1) * dv] = (so_ref[:, h * dv:(h + 1) * dv].astype(F32) * yn).astype(o_ref.dtype)


def _mlstm(qk, vproj, v_col, gates, so, so_col, conv_w, conv_b, norm_g, bsz, seq):
    n = bsz * seq
    nc = seq // CHUNK
    L = CHUNK
    return pl.pallas_call(
        _mlstm_kernel,
        grid=(bsz, nc),
        in_specs=[
            pl.BlockSpec((L, 2 * M_QK_W), lambda b, c: (b * nc + c, 0)),
            pl.BlockSpec((L, M_V_W), lambda b, c: (b * nc + c, v_col)),
            pl.BlockSpec((L, LANES), lambda b, c: (b * nc + c, 0)),
            pl.BlockSpec((L, M_V_W), lambda b, c: (b * nc + c, so_col)),
            pl.BlockSpec((CONV_K, 2 * M_QK_W), lambda b, c: (0, 0)),
            pl.BlockSpec((8, 2 * M_QK_W), lambda b, c: (0, 0)),
            pl.BlockSpec((8, M_V_W), lambda b, c: (0, 0)),
        ],
        out_specs=pl.BlockSpec((L, M_V_W), lambda b, c: (b * nc + c, 0)),
        out_shape=jax.ShapeDtypeStruct((n, M_V_W), BF16),
        scratch_shapes=[
            pltpu.VMEM((L, 2 * M_QK_W), BF16),
            pltpu.VMEM((M_HEADS, M_DQK, M_DV), F32),
            pltpu.VMEM((M_HEADS, 1, M_DQK), F32),
            pltpu.VMEM((M_HEADS, 1, 1), F32),
        ],
        compiler_params=_cparams(("parallel", "arbitrary")),
        name="mlstm",
    )(qk, vproj, gates, so, conv_w.astype(F32), _aux_rows(conv_b), _aux_rows(norm_g))


def _dattn_load_tile(q_ref, v_ref, vt_ref, seq):
    tv = 256

    @pl.when(pl.program_id(2) == 0)
    def _():
        for c in range(seq // tv):
            blk = v_ref[c * tv:(c + 1) * tv, :].astype(F32)
            vt_ref[:, c * tv:(c + 1) * tv] = blk.T.astype(BF16)

    qt = q_ref[...].astype(F32).T
    row = lax.broadcasted_iota(jnp.int32, qt.shape, 0)
    return (jnp.where(row < D_HD, qt, 0.0).astype(BF16), jnp.where(row >= D_HD, qt, 0.0).astype(BF16))


def _dattn_finish(lam_ref, sg_ref, o_ref, acc_ref, l1, l2, out_scale):
    ot = acc_ref[0] * (1.0 / l1) - acc_ref[1] * (lam_ref[0] / l2)
    yt = ot * lax.rsqrt(jnp.mean(ot * ot, axis=0, keepdims=True) + EPS)
    o_ref[...] = (yt.T * (sg_ref[0:1, :] * out_scale)).astype(o_ref.dtype)


def _causal_mask(s):
    kr = lax.broadcasted_iota(jnp.int32, s.shape, 0)
    qc = lax.broadcasted_iota(jnp.int32, s.shape, 1)
    return jnp.where(kr <= qc, s, NEG_BIG)


def _dattn_kernel(lam_ref, q_ref, k_ref, v_ref, sg_ref, o_ref, vt_ref, acc_ref, qz_ref, l_ref, p_ref, *,
                  seq, tq, tk, out_scale):
    n_diag = tq // tk
    assert n_diag * tk == tq and n_diag % 2 == 0
    qi = pl.program_id(2)
    qz = _dattn_load_tile(q_ref, v_ref, vt_ref, seq)
    qz_ref[0] = qz[0]
    qz_ref[1] = qz[1]
    acc_ref[...] = jnp.zeros_like(acc_ref)
    l_ref[...] = jnp.zeros_like(l_ref)

    def scores(j, lo, slot):
        q0 = lo or 0
        kj = k_ref[pl.ds(pl.multiple_of(j * tk, tk), tk), :]
        for c in range(2):
            s = jnp.dot(kj, qz_ref[c, :, q0:], preferred_element_type=F32)
            if lo is not None:
                s = _causal_mask(s)
            p = jnp.exp2(s)
            l_ref[c, :, q0:] += jnp.sum(p, axis=0, keepdims=True)
            p_ref[slot, c, :, q0:] = p.astype(BF16)

    def accumulate(j, lo, slot):
        q0 = lo or 0
        vtj = vt_ref[:, pl.ds(pl.multiple_of(j * tk, tk), tk)]
        for c in range(2):
            acc_ref[c, :, q0:] += jnp.dot(vtj, p_ref[slot, c, :, q0:], preferred_element_type=F32)

    n = n_diag * qi

    @pl.when(qi > 0)
    def _():
        scores(0, None, 0)

    def pair(i, carry):
        j = 2 * i
        accumulate(j, None, 0)
        scores(j + 1, None, 1)
        accumulate(j + 1, None, 1)
        scores(j + 2, None, 0)
        return carry

    lax.fori_loop(0, n // 2 - 1, pair, 0)

    @pl.when(qi > 0)
    def _():
        accumulate(n - 2, None, 0)
        scores(n - 1, None, 1)
        accumulate(n - 1, None, 1)
        scores(n, 0, 0)

    @pl.when(qi == 0)
    def _():
        scores(n, 0, 0)

    for d in range(n_diag):
        accumulate(n + d, d * tk, d % 2)
        if d + 1 < n_diag:
            scores(n + d + 1, (d + 1) * tk, (d + 1) % 2)
    _dattn_finish(lam_ref, sg_ref, o_ref, acc_ref, l_ref[0], l_ref[1], out_scale)


def _dattn_stabilised_kernel(lam_ref, q_ref, k_ref, v_ref, sg_ref, o_ref, vt_ref, acc_ref, *,
                             seq, tq, tk, out_scale):
    qi = pl.program_id(2)
    qz = _dattn_load_tile(q_ref, v_ref, vt_ref, seq)
    acc_ref[...] = jnp.zeros_like(acc_ref)

    def chunk(j, carry, lo):
        start = pl.multiple_of(j * tk, tk)
        kj = k_ref[pl.ds(start, tk), :]
        vtj = vt_ref[:, pl.ds(start, tk)]
        q0 = lo or 0
        out = []
        for c in range(2):
            m, l = carry[2 * c][:, q0:], carry[2 * c + 1][:, q0:]
            s = jnp.dot(kj, qz[c][:, q0:], preferred_element_type=F32)
            if lo is not None:
                s = _causal_mask(s)
            m_new = jnp.maximum(m, jnp.max(s, axis=0, keepdims=True))
            p = jnp.exp2(s - m_new)
            alpha = jnp.exp2(m - m_new)
            l_new = alpha * l + jnp.sum(p, axis=0, keepdims=True)
            acc_ref[c, :, q0:] = alpha * acc_ref[c, :, q0:] + jnp.dot(
                vtj, p.astype(BF16), preferred_element_type=F32)
            if q0:
                m_new = jnp.concatenate([carry[2 * c][:, :q0], m_new], axis=1)
                l_new = jnp.concatenate([carry[2 * c + 1][:, :q0], l_new], axis=1)
            out += [m_new, l_new]
        return tuple(out)

    init = (jnp.full((1, tq), NEG_BIG, F32), jnp.zeros((1, tq), F32)) * 2
    n_full = qi * (tq // tk)
    carry = lax.fori_loop(0, n_full, lambda j, c: chunk(j, c, None), init)
    for d in range(tq // tk):
        carry = chunk(n_full + d, carry, d * tk)

    _dattn_finish(lam_ref, sg_ref, o_ref, acc_ref, carry[1], carry[3], out_scale)


def _diff_attention(lam, qk, v, v_col0, subln_g, bsz, seq, out_scale, stabilised, tq, tk):
    n = bsz * seq
    tq = min(tq, seq)
    tk = min(tk, tq)
    nq = seq // tq
    hw = 2 * D_HD
    scratch = [pltpu.VMEM((D_DV, seq), BF16), pltpu.VMEM((2, D_DV, tq), F32)]
    if stabilised:
        body = _dattn_stabilised_kernel
    else:
        body = _dattn_kernel
        scratch += [pltpu.VMEM((2, hw, tq), BF16), pltpu.VMEM((2, 1, tq), F32), pltpu.VMEM((2, 2, tk, tq), BF16)]
    return pl.pallas_call(
        functools.partial(body, seq=seq, tq=tq, tk=tk, out_scale=out_scale),
        grid=(bsz, D_HEADS, nq),
        in_specs=[
            pl.BlockSpec(memory_space=pltpu.SMEM),
            pl.BlockSpec((tq, hw), lambda b, h, i: (b * nq + i, h)),
            pl.BlockSpec((seq, hw), lambda b, h, i: (b, D_HEADS + h)),
            pl.BlockSpec((seq, D_DV), lambda b, h, i: (b, v_col0 + h)),
            pl.BlockSpec((8, D_DV), lambda b, h, i: (0, 0)),
        ],
        out_specs=pl.BlockSpec((tq, D_DV), lambda b, h, i: (b * nq + i, h)),
        out_shape=jax.ShapeDtypeStruct((n, D_V_W), BF16),
        scratch_shapes=scratch,
        compiler_params=_cparams(("parallel", "parallel", "arbitrary")),
        name="diff_attn_stabilised" if stabilised else "diff_attn",
    )(lam.reshape(1).astype(F32), qk, qk, v, _aux_rows(subln_g))


def _xattn_kernel(q_ref, mk_ref, mv_ref, o_ref):
    for h in range(C_HEADS):
        q = q_ref[:, h * C_DQK:(h + 1) * C_DQK]
        k = mk_ref[:, h * C_DQK:(h + 1) * C_DQK]
        s = lax.dot_general(q, k, (((1,), (1,)), ((), ())), preferred_element_type=F32)
        m = jnp.max(s, axis=-1, keepdims=True)
        p = jnp.exp2(s - m)
        l = jnp.sum(p, axis=-1, keepdims=True)
        o = jnp.dot(p.astype(BF16), mv_ref[:, h * C_DV:(h + 1) * C_DV], preferred_element_type=F32)
        o_ref[:, h * C_DV:(h + 1) * C_DV] = (o / l).astype(o_ref.dtype)


def _cross_attention(q, mk, mv, bsz, seq, mem_len, tq=2048):
    n = bsz * seq
    tq = min(tq, seq)
    nq = seq // tq
    return pl.pallas_call(
        _xattn_kernel,
        grid=(bsz, nq),
        in_specs=[
            pl.BlockSpec((tq, C_Q_W), lambda b, i: (b * nq + i, 0)),
            pl.BlockSpec((mem_len, C_Q_W), lambda b, i: (b, 0)),
            pl.BlockSpec((mem_len, C_V_W), lambda b, i: (b, 0)),
        ],
        out_specs=pl.BlockSpec((tq, C_V_W), lambda b, i: (b * nq + i, 0)),
        out_shape=jax.ShapeDtypeStruct((n, C_V_W), BF16),
        compiler_params=_cparams(("parallel", "parallel")),
        name="cross_attn",
    )(q, mk, mv)


def _merge_kernel(x_ref, ym_ref, yd_ref, yc_ref, gm_ref, gd_ref, gc_ref,
                  wm_ref, wd_ref, wc_ref, wo_ref, g_ref, o_ref, h_ref):
    merged = gm_ref[...].astype(F32) * jnp.dot(ym_ref[...], wm_ref[...], preferred_element_type=F32)
    merged = merged + gd_ref[...].astype(F32) * jnp.dot(yd_ref[...], wd_ref[...], preferred_element_type=F32)
    merged = merged + gc_ref[...].astype(F32) * jnp.dot(yc_ref[...], wc_ref[...], preferred_element_type=F32)
    x1 = x_ref[...] + jnp.dot(merged.astype(BF16), wo_ref[...], preferred_element_type=F32)
    o_ref[...] = x1
    y = x1 * lax.rsqrt(jnp.mean(x1 * x1, axis=-1, keepdims=True) + EPS)
    h_ref[...] = (y * g_ref[...]).astype(BF16)


def _merge(x, ym, yd, yc, gates, g_col0, wm, wd, wc, wo, g_next, tm):
    n, d = x.shape
    tm = min(tm, n)
    row = lambda i: (i, 0)
    wspec = pl.BlockSpec((d, d), lambda i: (0, 0), pipeline_mode=pl.Buffered(1))
    return pl.pallas_call(
        _merge_kernel,
        grid=(n // tm,),
        in_specs=[
            pl.BlockSpec((tm, d), row),
            pl.BlockSpec((tm, d), row),
            pl.BlockSpec((tm, d), row),
            pl.BlockSpec((tm, d), row),
            pl.BlockSpec((tm, d), lambda i: (i, g_col0)),
            pl.BlockSpec((tm, d), lambda i: (i, g_col0 + 1)),
            pl.BlockSpec((tm, d), lambda i: (i, g_col0 + 2)),
            wspec, wspec, wspec, wspec,
            pl.BlockSpec((1, d), lambda i: (0, 0)),
        ],
        out_specs=[pl.BlockSpec((tm, d), row), pl.BlockSpec((tm, d), row)],
        out_shape=[jax.ShapeDtypeStruct((n, d), F32), jax.ShapeDtypeStruct((n, d), BF16)],
        compiler_params=_cparams(("parallel",)),
        name="merge",
    )(x, ym, yd, yc, gates, gates, gates, wm, wd, wc, wo, g_next.reshape(1, d).astype(F32))


def _mlp_kernel(x_ref, h_ref, wu_ref, wd_ref, o_ref):
    @pl.when(pl.program_id(1) == 0)
    def _():
        o_ref[...] = x_ref[...]

    u = jnp.maximum(jnp.dot(h_ref[...], wu_ref[...], preferred_element_type=F32), 0.0)
    o_ref[...] += jnp.dot((u * u).astype(BF16), wd_ref[...], preferred_element_type=F32)


def _mlp(x, h, wu, wd, tm, tf):
    n, d = x.shape
    dff = wu.shape[1]
    tm = min(tm, n)
    return pl.pallas_call(
        _mlp_kernel,
        grid=(n // tm, dff // tf),
        in_specs=[
            pl.BlockSpec((tm, d), lambda i, f: (i, 0)),
            pl.BlockSpec((tm, d), lambda i, f: (i, 0)),
            pl.BlockSpec((d, tf), lambda i, f: (0, f)),
            pl.BlockSpec((tf, d), lambda i, f: (f, 0)),
        ],
        out_specs=pl.BlockSpec((tm, d), lambda i, f: (i, 0)),
        out_shape=jax.ShapeDtypeStruct((n, d), F32),
        compiler_params=_cparams(("parallel", "arbitrary")),
        name="mlp",
    )(x, h, wu, wd)


def _lambda_init(layer):
    return 0.8 - 0.6 * math.exp(-0.3 * layer)


def _layer(l, x2, mem2, bsz, seq, mem_len, p):
    split = [0]
    for w in (M_QK_W, M_QK_W, M_V_W, M_HEADS, M_HEADS, M_V_W, D_Q_W, D_Q_W, D_V_W, C_Q_W):
        split.append(split[-1] + w)
    w_in = p['w_in'][l]
    g_mix = p['norm_mix_g'][l]

    def head_major(a):
        lead = a.shape[:-1]
        return a.reshape(lead + (2, M_HEADS, M_DQK)).swapaxes(-3, -2).reshape(lead + (2 * M_QK_W,))

    conv_w_half = 0.5 * head_major(p['conv_w'][l].astype(F32))
    conv_b_half = 0.5 * head_major(p['conv_b'][l].astype(F32))

    q_gain = jnp.tile(p['dq_norm_g'][l].astype(F32), D_Q_W // D_HD) * (D_HD ** -0.5 * LOG2E)
    k_gain = jnp.tile(p['dk_norm_g'][l].astype(F32), D_Q_W // D_HD)
    cq_gain = jnp.tile(p['cq_norm_g'][l].astype(F32), C_HEADS) * (C_DQK ** -0.5 * LOG2E)
    if_pad = LANES - 2 * M_HEADS
    w_all = _pack_w_in(w_in, p['w_gate'][l], tuple(split))
    aux_all = _aux_rows(jnp.concatenate(
        [jnp.zeros((2 * M_QK_W + M_V_W + D_V_W + M_V_W,), F32), 0.5 * p['b_gate'][l].astype(F32), q_gain, k_gain,
         cq_gain, p['b_igate'][l].astype(F32), p['b_fgate'][l].astype(F32), jnp.zeros((if_pad,), F32)]))
    win = {}
    c0 = 0
    for name, width in (("mqk", 2 * M_QK_W), ("v", M_V_W + D_V_W), ("gates", M_V_W + N_BRANCH * D_MODEL),
                        ("dqk", 2 * D_Q_W), ("cq", C_Q_W), ("if", LANES)):
        win[name] = (c0, width)
        c0 += width

    mqk, h = _norm_matmul(x2, g_mix, w_all, aux_all, win["mqk"], _ep_identity, BF16, 1024, 1024, "proj_mqk")
    gates_if = _matmul(h, w_all, aux_all, win["if"], _ep_bias, F32, 2048, LANES, "proj_if")
    vproj = _matmul(h, w_all, aux_all, win["v"], _ep_identity, BF16, PROJ_TM, PROJ_TN, "proj_v")
    sgates = _matmul(h, w_all, aux_all, win["gates"], _ep_sigmoid_of_double, BF16, PROJ_TM, PROJ_TN, "proj_gates")
    dqk = _matmul(h, w_all, aux_all, win["dqk"], functools.partial(_ep_group_norm, D_HD), BF16,
                  PROJ_NORM_TM, PROJ_TN, "proj_dqk")
    cq = _matmul(h, w_all, aux_all, win["cq"], functools.partial(_ep_group_norm, C_DQK), BF16,
                 PROJ_NORM_TM, C_Q_W, "proj_cq")
    w_kv = p['w_mem_kv'][l].astype(BF16)
    aux_kv = _aux_rows(jnp.concatenate([jnp.tile(p['ck_norm_g'][l].astype(F32), C_HEADS), jnp.zeros((C_V_W,), F32)]))
    mk, hmem = _norm_matmul(mem2, p['mem_norm_g'][l], w_kv, aux_kv, (0, C_Q_W),
                            functools.partial(_ep_group_norm, C_DQK), BF16, 1024, C_Q_W, "proj_mk")
    mv = _matmul(hmem, w_kv, aux_kv, (C_Q_W, C_V_W), _ep_identity, BF16, 1024, C_Q_W, "proj_mv")

    y_m = _mlstm(mqk, vproj, 0, gates_if, sgates, 0, conv_w_half, conv_b_half, p['m_norm_g'][l], bsz, seq)

    lam_i = _lambda_init(l)
    lam = (jnp.exp(jnp.sum(p['lam_q1'][l].astype(F32) * p['lam_k1'][l].astype(F32)))
           - jnp.exp(jnp.sum(p['lam_q2'][l].astype(F32) * p['lam_k2'][l].astype(F32))) + lam_i)
    score_bound = 1.02 * D_HD * jnp.max(jnp.abs(q_gain)) * jnp.max(jnp.abs(k_gain))
    dattn = functools.partial(_diff_attention, lam, dqk, vproj, M_V_W // D_DV, p['subln_g'][l], bsz, seq,
                              1.0 - lam_i)
    y_d = lax.cond(score_bound <= MAX_UNSTABILISED_SCORE,
                   lambda: dattn(False, DATTN_TQ_FAST, DATTN_TK_FAST),
                   lambda: dattn(True, DATTN_TQ_STABILISED, DATTN_TK_STABILISED))

    y_c = _cross_attention(cq, mk, mv, bsz, seq, mem_len)

    x2, h2 = _merge(x2, y_m, y_d, y_c, sgates, 1,
                    p['w_proj_m'][l].astype(BF16), p['w_proj_d'][l].astype(BF16),
                    p['w_proj_c'][l].astype(BF16), p['w_out'][l].astype(BF16), p['norm_mlp_g'][l], MERGE_TM)
    return _mlp(x2, h2, p['w_up'][l].astype(BF16), p['w_down'][l].astype(BF16), MLP_TM, MLP_TF)


def kernel(x, mem, norm_mix_g, w_in, b_igate, b_fgate, conv_w, conv_b, m_norm_g, dq_norm_g, dk_norm_g, lam_q1, lam_k1, lam_q2, lam_k2, subln_g, cq_norm_g, ck_norm_g, mem_norm_g, w_mem_kv, w_gate, b_gate, w_proj_m, w_proj_d, w_proj_c, w_out, norm_mlp_g, w_up, w_down):
    p = dict(norm_mix_g=norm_mix_g, w_in=w_in, b_igate=b_igate, b_fgate=b_fgate, conv_w=conv_w, conv_b=conv_b,
             m_norm_g=m_norm_g, dq_norm_g=dq_norm_g, dk_norm_g=dk_norm_g, lam_q1=lam_q1, lam_k1=lam_k1,
             lam_q2=lam_q2, lam_k2=lam_k2, subln_g=subln_g, cq_norm_g=cq_norm_g, ck_norm_g=ck_norm_g,
             mem_norm_g=mem_norm_g, w_mem_kv=w_mem_kv, w_gate=w_gate, b_gate=b_gate, w_proj_m=w_proj_m,
             w_proj_d=w_proj_d, w_proj_c=w_proj_c, w_out=w_out, norm_mlp_g=norm_mlp_g, w_up=w_up, w_down=w_down)
    bsz, seq, d = x.shape
    mem_len = mem.shape[1]
    x2 = x.reshape(bsz * seq, d)
    mem2 = mem.reshape(bsz * mem_len, d)
    for l in range(w_in.shape[0]):
        x2 = _layer(l, x2, mem2, bsz, seq, mem_len, p)
    return x2.reshape(bsz, seq, d)
```

```python
import functools
import math

import jax
import jax.numpy as jnp
from jax import lax
from jax.experimental import pallas as pl
from jax.experimental.pallas import tpu as pltpu

F32 = jnp.float32
BF16 = jnp.bfloat16

EPS = 1e-6
LOG2E = math.log2(math.e)
NEG_BIG = -1e30

D_MODEL = 1024
MEM_LEN = 256
M_HEADS, M_DQK, M_DV = 4, 128, 256
CHUNK = 128
CONV_K = 4
D_HEADS, D_HD = 8, 64
D_DV = 2 * D_HD
C_HEADS, C_DQK, C_DV = 4, 128, 256
D_FF = 4 * D_MODEL
N_BRANCH = 3
LANES = 128
MXU_DIM = 256

M_QK_W = M_HEADS * M_DQK
M_V_W = M_HEADS * M_DV
D_Q_W = D_HEADS * 2 * D_HD
D_V_W = D_HEADS * D_DV
C_Q_W = C_HEADS * C_DQK
C_V_W = C_HEADS * C_DV

VMEM_LIMIT = 56 * 1024 * 1024

PROJ_TM = 2048
PROJ_WIDE_TM = 1024
PROJ_NORM_TM = 1024
PROJ_TN = 1024
MERGE_TM = 512
MLP_TM = 1024
MLP_TF = 1024
MLSTM_NB = 4
DATTN_TQ_FAST = 4096
DATTN_TQ_STABILISED = 1024
DATTN_TK_FAST = 512
DATTN_TK_STABILISED = 256
MAX_UNSTABILISED_SCORE = 60.0


def _cparams(sem):
    return pltpu.CompilerParams(dimension_semantics=sem, vmem_limit_bytes=VMEM_LIMIT)


def _ep_identity(acc, aux_ref):
    return acc


def _ep_sigmoid_of_double(acc, aux_ref):
    return 0.5 * jnp.tanh(acc + aux_ref[0:1, :]) + 0.5


def _ep_bias(acc, aux_ref):
    return acc + aux_ref[0:1, :]


def _ep_group_norm(group, acc, aux_ref):
    tn = acc.shape[1]
    cw = MXU_DIM
    r = lax.broadcasted_iota(jnp.int32, (cw, cw), 0) // group
    c = lax.broadcasted_iota(jnp.int32, (cw, cw), 1) // group
    gmat = (r == c).astype(BF16)
    outs = []
    for s in range(tn // cw):
        a = acc[:, s * cw:(s + 1) * cw]
        ss = jnp.dot((a * a).astype(BF16), gmat, preferred_element_type=F32)
        outs.append(a * lax.rsqrt(ss * (1.0 / group) + EPS))
    y = jnp.concatenate(outs, axis=1) if len(outs) > 1 else outs[0]
    return y * aux_ref[0:1, :]


def _column_tiles(h, w_ref, aux_ref, o_ref, epilogue, tn):
    for j in range(w_ref.shape[1] // tn):
        cs = slice(j * tn, (j + 1) * tn)
        acc = jnp.dot(h, w_ref[:, cs], preferred_element_type=F32)
        o_ref[:, cs] = epilogue(acc, aux_ref.at[:, cs]).astype(o_ref.dtype)


def _norm_matmul_kernel(x_ref, g_ref, w_ref, aux_ref, o_ref, h_ref, *, epilogue, tn):
    x = x_ref[...].astype(F32)
    y = x * lax.rsqrt(jnp.mean(x * x, axis=-1, keepdims=True) + EPS)
    h = (y * g_ref[...]).astype(BF16)
    h_ref[...] = h
    _column_tiles(h, w_ref, aux_ref, o_ref, epilogue, tn)


def _matmul_kernel(h_ref, w_ref, aux_ref, o_ref, *, epilogue, tn):
    _column_tiles(h_ref[...], w_ref, aux_ref, o_ref, epilogue, tn)


def _window_specs(d, window, tn):
    c0, cols = window
    assert cols % tn == 0 and c0 % cols == 0
    jw = c0 // cols
    return [pl.BlockSpec((d, cols), lambda i: (0, jw), pipeline_mode=pl.Buffered(1)),
            pl.BlockSpec((8, cols), lambda i: (0, jw), pipeline_mode=pl.Buffered(1))]


def _norm_matmul(x, g, w, aux, window, epilogue, out_dtype, tm, tn, name):
    n, d = x.shape
    cols = window[1]
    tm = min(tm, n)
    tn = min(tn, cols)
    assert n % tm == 0
    return pl.pallas_call(
        functools.partial(_norm_matmul_kernel, epilogue=epilogue, tn=tn),
        grid=(n // tm,),
        in_specs=[pl.BlockSpec((tm, d), lambda i: (i, 0)), pl.BlockSpec((1, d), lambda i: (0, 0))]
        + _window_specs(d, window, tn),
        out_specs=[pl.BlockSpec((tm, cols), lambda i: (i, 0)), pl.BlockSpec((tm, d), lambda i: (i, 0))],
        out_shape=[jax.ShapeDtypeStruct((n, cols), out_dtype), jax.ShapeDtypeStruct((n, d), BF16)],
        compiler_params=_cparams(("parallel",)),
        name=name,
    )(x, g.reshape(1, d).astype(F32), w, aux)


def _matmul(h, w, aux, window, epilogue, out_dtype, tm, tn, name):
    n, d = h.shape
    cols = window[1]
    tm = min(tm, n)
    tn = min(tn, cols)
    assert n % tm == 0
    return pl.pallas_call(
        functools.partial(_matmul_kernel, epilogue=epilogue, tn=tn),
        grid=(n // tm,),
        in_specs=[pl.BlockSpec((tm, d), lambda i: (i, 0))] + _window_specs(d, window, tn),
        out_specs=pl.BlockSpec((tm, cols), lambda i: (i, 0)),
        out_shape=jax.ShapeDtypeStruct((n, cols), out_dtype),
        compiler_params=_cparams(("parallel",)),
        name=name,
    )(h, w, aux)


def _pack_w_in_kernel(w_ref, wg_ref, o_ref, *, split):
    w = w_ref[...]
    col = lambda a, b: w[:, split[a]:split[b]]
    parts = [0.5 * col(5, 6), 0.5 * wg_ref[...], col(6, 8), col(2, 3), col(8, 9)]
    for h in range(M_HEADS):
        parts += [col(0, 1)[:, h * M_DQK:(h + 1) * M_DQK], col(1, 2)[:, h * M_DQK:(h + 1) * M_DQK]]
    parts += [col(9, 10), col(3, 5), jnp.zeros((w.shape[0], LANES - 2 * M_HEADS), w.dtype)]
    o_ref[...] = jnp.concatenate(parts, axis=1).astype(BF16)


def _pack_w_in(w_in, w_gate, split, tr=256):
    d, cin = w_in.shape
    cout = cin - 2 * M_HEADS + LANES + w_gate.shape[1]
    return pl.pallas_call(
        functools.partial(_pack_w_in_kernel, split=split),
        grid=(d // tr,),
        in_specs=[pl.BlockSpec((tr, cin), lambda i: (i, 0)),
                  pl.BlockSpec((tr, w_gate.shape[1]), lambda i: (i, 0))],
        out_specs=pl.BlockSpec((tr, cout), lambda i: (i, 0)),
        out_shape=jax.ShapeDtypeStruct((d, cout), BF16),
        compiler_params=_cparams(("parallel",)),
        name="pack_w_in",
    )(w_in, w_gate)


def _aux_rows(row):
    return jnp.zeros((8, row.shape[0]), F32).at[0].set(row.astype(F32))


def _mlstm_kernel(qk_ref, v_ref, gate_ref, so_ref, cw_ref, cb_ref, ng_ref, o_ref,
                  xprev_ref, c_ref, n_ref, m_ref):
    L = CHUNK
    dk, dv, nh = M_DQK, M_DV, M_HEADS
    nb = qk_ref.shape[0]

    @pl.when(pl.program_id(1) == 0)
    def _():
        xprev_ref[...] = jnp.zeros_like(xprev_ref)
        c_ref[...] = jnp.zeros_like(c_ref)
        n_ref[...] = jnp.zeros_like(n_ref)
        m_ref[...] = jnp.zeros_like(m_ref)

    n_sh = CONV_K - 1
    srow = lax.broadcasted_iota(jnp.int32, (n_sh * L, 2 * L), 0)
    scol = lax.broadcasted_iota(jnp.int32, (n_sh * L, 2 * L), 1)
    sel = (scol == (srow % L) + L - 1 - srow // L).astype(BF16)
    rr = lax.broadcasted_iota(jnp.int32, (L, L), 0)
    cc = lax.broadcasted_iota(jnp.int32, (L, L), 1)
    causal = cc <= rr
    tri = causal.astype(BF16)

    act, g, g_t, bcum, bcum_t = [], [], [], [], []
    for bb in range(nb):
        xb = qk_ref[bb]
        xx = jnp.concatenate([xprev_ref[bb], xb], axis=0)
        shifted = jnp.dot(sel, xx, preferred_element_type=F32)
        xprev_ref[bb] = xb
        z = cb_ref[0:1, :] + cw_ref[CONV_K - 1:CONV_K, :] * xb.astype(F32)
        for s in range(n_sh):
            z = z + cw_ref[CONV_K - 2 - s:CONV_K - 1 - s, :] * shifted[s * L:(s + 1) * L, :]
        act.append(z + z * jnp.tanh(z))

        gg = gate_ref[bb]
        lf = jnp.minimum(gg, 0.0) - jnp.log1p(jnp.exp(-jnp.abs(gg)))
        lf_hi = lf.astype(BF16)
        lf_lo = (lf - lf_hi.astype(F32)).astype(BF16)
        bc = (jnp.dot(tri, lf_hi, preferred_element_type=F32)
              + jnp.dot(tri, lf_lo, preferred_element_type=F32))
        g.append(gg)
        g_t.append(gg.T)
        bcum.append(bc)
        bcum_t.append(bc.T)

    streams = [(bb, h) for bb in range(nb) for h in range(nh)]
    heads = range(len(streams))
    bcol = [bcum[bb][:, nh + h:nh + h + 1] for bb, h in streams]
    brow = [bcum_t[bb][nh + h:nh + h + 1, :] for bb, h in streams]
    irow = [g_t[bb][h:h + 1, :] for bb, h in streams]
    icol = [g[bb][:, h:h + 1] for bb, h in streams]
    m_old = [m_ref[i] for i in heads]
    c_old = [c_ref[i] for i in heads]
    n_old = [n_ref[i] for i in heads]
    qh = [act[bb][:, 2 * dk * h:2 * dk * h + dk] * (dk ** -0.5) for bb, h in streams]
    kh = [act[bb][:, 2 * dk * h + dk:2 * dk * (h + 1)] for bb, h in streams]
    vh = [v_ref[bb, :, h * dv:(h + 1) * dv] for bb, h in streams]
    qb = [q.astype(BF16) for q in qh]
    kb = [k.astype(BF16) for k in kh]

    qk_t = [lax.dot_general(qb[h], kb[h], (((1,), (1,)), ((), ())), preferred_element_type=F32) for h in heads]
    q_c = [jnp.dot(qb[h], c_old[h].astype(BF16), preferred_element_type=F32) for h in heads]
    logd = [jnp.where(causal, bcol[h] - brow[h] + irow[h], -jnp.inf) for h in heads]
    inter = [bcol[h] + m_old[h] for h in heads]
    m_loc = [jnp.maximum(inter[h], jnp.max(logd[h], axis=-1, keepdims=True)) for h in heads]
    w_inter = [jnp.exp(inter[h] - m_loc[h]) for h in heads]
    s = [qk_t[h] * jnp.exp(logd[h] - m_loc[h]) for h in heads]
    num = [jnp.dot(s[h].astype(BF16), vh[h], preferred_element_type=F32) + w_inter[h] * q_c[h] for h in heads]
    den = [jnp.sum(s[h], axis=-1, keepdims=True) + w_inter[h] * jnp.sum(qh[h] * n_old[h], axis=-1, keepdims=True)
           for h in heads]

    b_end = [bcol[h][L - 1:L, :] for h in heads]
    log_w = [b_end[h] - bcol[h] + icol[h] for h in heads]
    m_new = [jnp.maximum(b_end[h] + m_old[h], jnp.max(log_w[h], axis=0, keepdims=True)) for h in heads]
    kw = [kh[h] * jnp.exp(log_w[h] - m_new[h]) for h in heads]
    decay = [jnp.exp(b_end[h] + m_old[h] - m_new[h]) for h in heads]
    for h in heads:
        c_ref[h] = decay[h] * c_old[h] + lax.dot_general(
            kw[h].astype(BF16), vh[h], (((0,), (0,)), ((), ())), preferred_element_type=F32)
        n_ref[h] = decay[h] * n_old[h] + jnp.sum(kw[h], axis=0, keepdims=True)
        m_ref[h] = m_new[h]

    for i, (bb, h) in enumerate(streams):
        d = jnp.maximum(jnp.abs(den[i]), jnp.exp(-m_loc[i]))
        row_scale = lax.rsqrt(jnp.mean(num[i] * num[i], axis=-1, keepdims=True) + EPS * (d * d))
        yn = num[i] * row_scale * ng_ref[0:1, h * dv:(h + 1) * dv]
        o_ref[bb, :, h * dv:(h + 1) * dv] = (
            so_ref[bb, :, h * dv:(h + 1) * dv].astype(F32) * yn).astype(o_ref.dtype)


def _mlstm(qk, vproj, v_col, gates, so, so_col, conv_w, conv_b, norm_g, bsz, seq, nb):
    nc = seq // CHUNK
    L = CHUNK
    assert bsz % nb == 0
    by_batch = lambda a: a.reshape(bsz, seq, a.shape[-1])
    out = pl.pallas_call(
        _mlstm_kernel,
        grid=(bsz // nb, nc),
        in_specs=[
            pl.BlockSpec((nb, L, 2 * M_QK_W), lambda b, c: (b, c, 0)),
            pl.BlockSpec((nb, L, M_V_W), lambda b, c: (b, c, v_col)),
            pl.BlockSpec((nb, L, LANES), lambda b, c: (b, c, 0)),
            pl.BlockSpec((nb, L, M_V_W), lambda b, c: (b, c, so_col)),
            pl.BlockSpec((CONV_K, 2 * M_QK_W), lambda b, c: (0, 0)),
            pl.BlockSpec((8, 2 * M_QK_W), lambda b, c: (0, 0)),
            pl.BlockSpec((8, M_V_W), lambda b, c: (0, 0)),
        ],
        out_specs=pl.BlockSpec((nb, L, M_V_W), lambda b, c: (b, c, 0)),
        out_shape=jax.ShapeDtypeStruct((bsz, seq, M_V_W), BF16),
        scratch_shapes=[
            pltpu.VMEM((nb, L, 2 * M_QK_W), BF16),
            pltpu.VMEM((nb * M_HEADS, M_DQK, M_DV), F32),
            pltpu.VMEM((nb * M_HEADS, 1, M_DQK), F32),
            pltpu.VMEM((nb * M_HEADS, 1, 1), F32),
        ],
        compiler_params=_cparams(("parallel", "arbitrary")),
        name="mlstm",
    )(by_batch(qk), by_batch(vproj), by_batch(gates), by_batch(so), conv_w.astype(F32), _aux_rows(conv_b),
      _aux_rows(norm_g))
    return out.reshape(bsz * seq, M_V_W)


def _dattn_load_tile(q_ref, v_ref, vt_ref, seq):
    tv = 256

    @pl.when(pl.program_id(2) == 0)
    def _():
        for c in range(seq // tv):
            blk = v_ref[c * tv:(c + 1) * tv, :].astype(F32)
            vt_ref[:, c * tv:(c + 1) * tv] = blk.T.astype(BF16)

    qt = q_ref[...].astype(F32).T
    row = lax.broadcasted_iota(jnp.int32, qt.shape, 0)
    return (jnp.where(row < D_HD, qt, 0.0).astype(BF16), jnp.where(row >= D_HD, qt, 0.0).astype(BF16))


def _dattn_finish(lam_ref, sg_ref, o_ref, acc_ref, l1, l2, out_scale):
    ot = acc_ref[0] * (1.0 / l1) - acc_ref[1] * (lam_ref[0] / l2)
    yt = ot * lax.rsqrt(jnp.mean(ot * ot, axis=0, keepdims=True) + EPS)
    o_ref[...] = (yt.T * (sg_ref[0:1, :] * out_scale)).astype(o_ref.dtype)


def _causal_mask(s):
    kr = lax.broadcasted_iota(jnp.int32, s.shape, 0)
    qc = lax.broadcasted_iota(jnp.int32, s.shape, 1)
    return jnp.where(kr <= qc, s, NEG_BIG)


def _dattn_kernel(lam_ref, q_ref, k_ref, v_ref, sg_ref, o_ref, vt_ref, acc_ref, qz_ref, l_ref, p_ref, *,
                  seq, tq, tk, out_scale):
    n_diag = tq // tk
    assert n_diag * tk == tq and n_diag % 2 == 0
    qi = pl.program_id(2)
    qz = _dattn_load_tile(q_ref, v_ref, vt_ref, seq)
    qz_ref[0] = qz[0]
    qz_ref[1] = qz[1]
    acc_ref[...] = jnp.zeros_like(acc_ref)
    l_ref[...] = jnp.zeros_like(l_ref)

    def scores(j, lo, slot):
        q0 = lo or 0
        kj = k_ref[pl.ds(pl.multiple_of(j * tk, tk), tk), :]
        for c in range(2):
            s = jnp.dot(kj, qz_ref[c, :, q0:], preferred_element_type=F32)
            if lo is not None:
                s = _causal_mask(s)
            p = jnp.exp2(s)
            l_ref[c, :, q0:] += jnp.sum(p, axis=0, keepdims=True)
            p_ref[slot, c, :, q0:] = p.astype(BF16)

    def accumulate(j, lo, slot):
        q0 = lo or 0
        vtj = vt_ref[:, pl.ds(pl.multiple_of(j * tk, tk), tk)]
        for c in range(2):
            acc_ref[c, :, q0:] += jnp.dot(vtj, p_ref[slot, c, :, q0:], preferred_element_type=F32)

    n = n_diag * qi

    @pl.when(qi > 0)
    def _():
        scores(0, None, 0)

    def pair(i, carry):
        j = 2 * i
        accumulate(j, None, 0)
        scores(j + 1, None, 1)
        accumulate(j + 1, None, 1)
        scores(j + 2, None, 0)
        return carry

    lax.fori_loop(0, n // 2 - 1, pair, 0)

    @pl.when(qi > 0)
    def _():
        accumulate(n - 2, None, 0)
        scores(n - 1, None, 1)
        accumulate(n - 1, None, 1)
        scores(n, 0, 0)

    @pl.when(qi == 0)
    def _():
        scores(n, 0, 0)

    for d in range(n_diag):
        accumulate(n + d, d * tk, d % 2)
        if d + 1 < n_diag:
            scores(n + d + 1, (d + 1) * tk, (d + 1) % 2)
    _dattn_finish(lam_ref, sg_ref, o_ref, acc_ref, l_ref[0], l_ref[1], out_scale)


def _dattn_stabilised_kernel(lam_ref, q_ref, k_ref, v_ref, sg_ref, o_ref, vt_ref, acc_ref, *,
                             seq, tq, tk, out_scale):
    qi = pl.program_id(2)
    qz = _dattn_load_tile(q_ref, v_ref, vt_ref, seq)
    acc_ref[...] = jnp.zeros_like(acc_ref)

    def chunk(j, carry, lo):
        start = pl.multiple_of(j * tk, tk)
        kj = k_ref[pl.ds(start, tk), :]
        vtj = vt_ref[:, pl.ds(start, tk)]
        q0 = lo or 0
        out = []
        for c in range(2):
            m, l = carry[2 * c][:, q0:], carry[2 * c + 1][:, q0:]
            s = jnp.dot(kj, qz[c][:, q0:], preferred_element_type=F32)
            if lo is not None:
                s = _causal_mask(s)
            m_new = jnp.maximum(m, jnp.max(s, axis=0, keepdims=True))
            p = jnp.exp2(s - m_new)
            alpha = jnp.exp2(m - m_new)
            l_new = alpha * l + jnp.sum(p, axis=0, keepdims=True)
            acc_ref[c, :, q0:] = alpha * acc_ref[c, :, q0:] + jnp.dot(
                vtj, p.astype(BF16), preferred_element_type=F32)
            if q0:
                m_new = jnp.concatenate([carry[2 * c][:, :q0], m_new], axis=1)
                l_new = jnp.concatenate([carry[2 * c + 1][:, :q0], l_new], axis=1)
            out += [m_new, l_new]
        return tuple(out)

    init = (jnp.full((1, tq), NEG_BIG, F32), jnp.zeros((1, tq), F32)) * 2
    n_full = qi * (tq // tk)
    carry = lax.fori_loop(0, n_full, lambda j, c: chunk(j, c, None), init)
    for d in range(tq // tk):
        carry = chunk(n_full + d, carry, d * tk)

    _dattn_finish(lam_ref, sg_ref, o_ref, acc_ref, carry[1], carry[3], out_scale)


def _diff_attention(lam, qk, v, v_col0, subln_g, bsz, seq, out_scale, stabilised, tq, tk):
    n = bsz * seq
    tq = min(tq, seq)
    tk = min(tk, tq)
    nq = seq // tq
    hw = 2 * D_HD
    scratch = [pltpu.VMEM((D_DV, seq), BF16), pltpu.VMEM((2, D_DV, tq), F32)]
    if stabilised:
        body = _dattn_stabilised_kernel
    else:
        body = _dattn_kernel
        scratch += [pltpu.VMEM((2, hw, tq), BF16), pltpu.VMEM((2, 1, tq), F32), pltpu.VMEM((2, 2, tk, tq), BF16)]
    return pl.pallas_call(
        functools.partial(body, seq=seq, tq=tq, tk=tk, out_scale=out_scale),
        grid=(bsz, D_HEADS, nq),
        in_specs=[
            pl.BlockSpec(memory_space=pltpu.SMEM),
            pl.BlockSpec((tq, hw), lambda b, h, i: (b * nq + i, h)),
            pl.BlockSpec((seq, hw), lambda b, h, i: (b, D_HEADS + h)),
            pl.BlockSpec((seq, D_DV), lambda b, h, i: (b, v_col0 + h)),
            pl.BlockSpec((8, D_DV), lambda b, h, i: (0, 0)),
        ],
        out_specs=pl.BlockSpec((tq, D_DV), lambda b, h, i: (b * nq + i, h)),
        out_shape=jax.ShapeDtypeStruct((n, D_V_W), BF16),
        scratch_shapes=scratch,
        compiler_params=_cparams(("parallel", "parallel", "arbitrary")),
        name="diff_attn_stabilised" if stabilised else "diff_attn",
    )(lam.reshape(1).astype(F32), qk, qk, v, _aux_rows(subln_g))


def _xattn_kernel(q_ref, mk_ref, mv_ref, o_ref):
    for h in range(C_HEADS):
        q = q_ref[:, h * C_DQK:(h + 1) * C_DQK]
        k = mk_ref[:, h * C_DQK:(h + 1) * C_DQK]
        s = lax.dot_general(q, k, (((1,), (1,)), ((), ())), preferred_element_type=F32)
        m = jnp.max(s, axis=-1, keepdims=True)
        p = jnp.exp2(s - m)
        l = jnp.sum(p, axis=-1, keepdims=True)
        o = jnp.dot(p.astype(BF16), mv_ref[:, h * C_DV:(h + 1) * C_DV], preferred_element_type=F32)
        o_ref[:, h * C_DV:(h + 1) * C_DV] = (o / l).astype(o_ref.dtype)


def _cross_attention(q, mk, mv, bsz, seq, mem_len, tq=2048):
    n = bsz * seq
    tq = min(tq, seq)
    nq = seq // tq
    return pl.pallas_call(
        _xattn_kernel,
        grid=(bsz, nq),
        in_specs=[
            pl.BlockSpec((tq, C_Q_W), lambda b, i: (b * nq + i, 0)),
            pl.BlockSpec((mem_len, C_Q_W), lambda b, i: (b, 0)),
            pl.BlockSpec((mem_len, C_V_W), lambda b, i: (b, 0)),
        ],
        out_specs=pl.BlockSpec((tq, C_V_W), lambda b, i: (b * nq + i, 0)),
        out_shape=jax.ShapeDtypeStruct((n, C_V_W), BF16),
        compiler_params=_cparams(("parallel", "parallel")),
        name="cross_attn",
    )(q, mk, mv)


def _merge_kernel(x_ref, ym_ref, yd_ref, yc_ref, gm_ref, gd_ref, gc_ref,
                  wm_ref, wd_ref, wc_ref, wo_ref, g_ref, o_ref, h_ref):
    merged = gm_ref[...].astype(F32) * jnp.dot(ym_ref[...], wm_ref[...], preferred_element_type=F32)
    merged = merged + gd_ref[...].astype(F32) * jnp.dot(yd_ref[...], wd_ref[...], preferred_element_type=F32)
    merged = merged + gc_ref[...].astype(F32) * jnp.dot(yc_ref[...], wc_ref[...], preferred_element_type=F32)
    x1 = x_ref[...] + jnp.dot(merged.astype(BF16), wo_ref[...], preferred_element_type=F32)
    o_ref[...] = x1
    y = x1 * lax.rsqrt(jnp.mean(x1 * x1, axis=-1, keepdims=True) + EPS)
    h_ref[...] = (y * g_ref[...]).astype(BF16)


def _merge(x, ym, yd, yc, gates, g_col0, wm, wd, wc, wo, g_next, tm):
    n, d = x.shape
    tm = min(tm, n)
    row = lambda i: (i, 0)
    wspec = pl.BlockSpec((d, d), lambda i: (0, 0), pipeline_mode=pl.Buffered(1))
    return pl.pallas_call(
        _merge_kernel,
        grid=(n // tm,),
        in_specs=[
            pl.BlockSpec((tm, d), row),
            pl.BlockSpec((tm, d), row),
            pl.BlockSpec((tm, d), row),
            pl.BlockSpec((tm, d), row),
            pl.BlockSpec((tm, d), lambda i: (i, g_col0)),
            pl.BlockSpec((tm, d), lambda i: (i, g_col0 + 1)),
            pl.BlockSpec((tm, d), lambda i: (i, g_col0 + 2)),
            wspec, wspec, wspec, wspec,
            pl.BlockSpec((1, d), lambda i: (0, 0)),
        ],
        out_specs=[pl.BlockSpec((tm, d), row), pl.BlockSpec((tm, d), row)],
        out_shape=[jax.ShapeDtypeStruct((n, d), F32), jax.ShapeDtypeStruct((n, d), BF16)],
        compiler_params=_cparams(("parallel",)),
        name="merge",
    )(x, ym, yd, yc, gates, gates, gates, wm, wd, wc, wo, g_next.reshape(1, d).astype(F32))


def _mlp_kernel(x_ref, h_ref, wu_ref, wd_ref, o_ref, *, tf):
    h = h_ref[...]
    acc = x_ref[...]
    for f in range(wu_ref.shape[1] // tf):
        u = jnp.maximum(jnp.dot(h, wu_ref[:, f * tf:(f + 1) * tf], preferred_element_type=F32), 0.0)
        acc = acc + jnp.dot((u * u).astype(BF16), wd_ref[f * tf:(f + 1) * tf, :], preferred_element_type=F32)
    o_ref[...] = acc


def _mlp(x, h, wu, wd, tm, tf):
    n, d = x.shape
    dff = wu.shape[1]
    tm = min(tm, n)
    return pl.pallas_call(
        functools.partial(_mlp_kernel, tf=tf),
        grid=(n // tm,),
        in_specs=[
            pl.BlockSpec((tm, d), lambda i: (i, 0)),
            pl.BlockSpec((tm, d), lambda i: (i, 0)),
            pl.BlockSpec((d, dff), lambda i: (0, 0), pipeline_mode=pl.Buffered(1)),
            pl.BlockSpec((dff, d), lambda i: (0, 0), pipeline_mode=pl.Buffered(1)),
        ],
        out_specs=pl.BlockSpec((tm, d), lambda i: (i, 0)),
        out_shape=jax.ShapeDtypeStruct((n, d), F32),
        compiler_params=_cparams(("parallel",)),
        name="mlp",
    )(x, h, wu, wd)


def _lambda_init(layer):
    return 0.8 - 0.6 * math.exp(-0.3 * layer)


def _layer(l, x2, mem2, bsz, seq, mem_len, p):
    split = [0]
    for w in (M_QK_W, M_QK_W, M_V_W, M_HEADS, M_HEADS, M_V_W, D_Q_W, D_Q_W, D_V_W, C_Q_W):
        split.append(split[-1] + w)
    w_in = p['w_in'][l]
    g_mix = p['norm_mix_g'][l]

    def head_major(a):
        lead = a.shape[:-1]
        return a.reshape(lead + (2, M_HEADS, M_DQK)).swapaxes(-3, -2).reshape(lead + (2 * M_QK_W,))

    conv_w_half = 0.5 * head_major(p['conv_w'][l].astype(F32))
    conv_b_half = 0.5 * head_major(p['conv_b'][l].astype(F32))

    q_gain = jnp.tile(p['dq_norm_g'][l].astype(F32), D_Q_W // D_HD) * (D_HD ** -0.5 * LOG2E)
    k_gain = jnp.tile(p['dk_norm_g'][l].astype(F32), D_Q_W // D_HD)
    cq_gain = jnp.tile(p['cq_norm_g'][l].astype(F32), C_HEADS) * (C_DQK ** -0.5 * LOG2E)
    if_pad = LANES - 2 * M_HEADS
    w_all = _pack_w_in(w_in, p['w_gate'][l], tuple(split))
    aux_all = _aux_rows(jnp.concatenate(
        [jnp.zeros((M_V_W,), F32), 0.5 * p['b_gate'][l].astype(F32), q_gain, k_gain,
         jnp.zeros((M_V_W + D_V_W + 2 * M_QK_W,), F32), cq_gain, p['b_igate'][l].astype(F32),
         p['b_fgate'][l].astype(F32), jnp.zeros((if_pad,), F32)]))
    win = {}
    c0 = 0
    for name, width in (("gates", M_V_W + N_BRANCH * D_MODEL), ("dqk", 2 * D_Q_W), ("v", M_V_W + D_V_W),
                        ("mqk", 2 * M_QK_W), ("cq", C_Q_W), ("if", LANES)):
        win[name] = (c0, width)
        c0 += width

    mqk, h = _norm_matmul(x2, g_mix, w_all, aux_all, win["mqk"], _ep_identity, BF16, 1024, PROJ_TN, "proj_mqk")
    gates_if = _matmul(h, w_all, aux_all, win["if"], _ep_bias, F32, PROJ_TM, LANES, "proj_if")
    vproj = _matmul(h, w_all, aux_all, win["v"], _ep_identity, BF16, PROJ_TM, PROJ_TN, "proj_v")
    sgates = _matmul(h, w_all, aux_all, win["gates"], _ep_sigmoid_of_double, BF16, PROJ_WIDE_TM, PROJ_TN,
                     "proj_gates")
    dqk = _matmul(h, w_all, aux_all, win["dqk"], functools.partial(_ep_group_norm, D_HD), BF16,
                  PROJ_NORM_TM, PROJ_TN, "proj_dqk")
    cq = _matmul(h, w_all, aux_all, win["cq"], functools.partial(_ep_group_norm, C_DQK), BF16,
                 PROJ_TM, C_Q_W, "proj_cq")
    w_kv = p['w_mem_kv'][l]
    w_kv = jnp.concatenate([w_kv[:, C_Q_W:], w_kv[:, :C_Q_W]], axis=1).astype(BF16)
    aux_kv = _aux_rows(jnp.concatenate([jnp.zeros((C_V_W,), F32), jnp.tile(p['ck_norm_g'][l].astype(F32), C_HEADS)]))
    mk, hmem = _norm_matmul(mem2, p['mem_norm_g'][l], w_kv, aux_kv, (C_V_W, C_Q_W),
                            functools.partial(_ep_group_norm, C_DQK), BF16, 1024, C_Q_W, "proj_mk")
    mv = _matmul(hmem, w_kv, aux_kv, (0, C_V_W), _ep_identity, BF16, 1024, C_Q_W, "proj_mv")

    y_m = _mlstm(mqk, vproj, 0, gates_if, sgates, 0, conv_w_half, conv_b_half, p['m_norm_g'][l], bsz, seq,
                 MLSTM_NB if bsz % MLSTM_NB == 0 else 1)

    lam_i = _lambda_init(l)
    lam = (jnp.exp(jnp.sum(p['lam_q1'][l].astype(F32) * p['lam_k1'][l].astype(F32)))
           - jnp.exp(jnp.sum(p['lam_q2'][l].astype(F32) * p['lam_k2'][l].astype(F32))) + lam_i)
    score_bound = 1.02 * D_HD * jnp.max(jnp.abs(q_gain)) * jnp.max(jnp.abs(k_gain))
    dattn = functools.partial(_diff_attention, lam, dqk, vproj, M_V_W // D_DV, p['subln_g'][l], bsz, seq,
                              1.0 - lam_i)
    y_d = lax.cond(score_bound <= MAX_UNSTABILISED_SCORE,
                   lambda: dattn(False, DATTN_TQ_FAST, DATTN_TK_FAST),
                   lambda: dattn(True, DATTN_TQ_STABILISED, DATTN_TK_STABILISED))

    y_c = _cross_attention(cq, mk, mv, bsz, seq, mem_len)

    x2, h2 = _merge(x2, y_m, y_d, y_c, sgates, 1,
                    p['w_proj_m'][l].astype(BF16), p['w_proj_d'][l].astype(BF16),
                    p['w_proj_c'][l].astype(BF16), p['w_out'][l].astype(BF16), p['norm_mlp_g'][l], MERGE_TM)
    return _mlp(x2, h2, p['w_up'][l].astype(BF16), p['w_down'][l].astype(BF16), MLP_TM, MLP_TF)


def kernel(x, mem, norm_mix_g, w_in, b_igate, b_fgate, conv_w, conv_b, m_norm_g, dq_norm_g, dk_norm_g, lam_q1, lam_k1, lam_q2, lam_k2, subln_g, cq_norm_g, ck_norm_g, mem_norm_g, w_mem_kv, w_gate, b_gate, w_proj_m, w_proj_d, w_proj_c, w_out, norm_mlp_g, w_up, w_down):
    p = dict(norm_mix_g=norm_mix_g, w_in=w_in, b_igate=b_igate, b_fgate=b_fgate, conv_w=conv_w, conv_b=conv_b,
             m_norm_g=m_norm_g, dq_norm_g=dq_norm_g, dk_norm_g=dk_norm_g, lam_q1=lam_q1, lam_k1=lam_k1,
             lam_q2=lam_q2, lam_k2=lam_k2, subln_g=subln_g, cq_norm_g=cq_norm_g, ck_norm_g=ck_norm_g,
             mem_norm_g=mem_norm_g, w_mem_kv=w_mem_kv, w_gate=w_gate, b_gate=b_gate, w_proj_m=w_proj_m,
             w_proj_d=w_proj_d, w_proj_c=w_proj_c, w_out=w_out, norm_mlp_g=norm_mlp_g, w_up=w_up, w_down=w_down)
    bsz, seq, d = x.shape
    mem_len = mem.shape[1]
    x2 = x.reshape(bsz * seq, d)
    mem2 = mem.reshape(bsz * mem_len, d)
    for l in range(w_in.shape[0]):
        x2 = _layer(l, x2, mem2, bsz, seq, mem_len, p)
    return x2.reshape(bsz, seq, d)
```

```python
import functools
import math

import jax
import jax.numpy as jnp
from jax import lax
from jax.experimental import pallas as pl
from jax.experimental.pallas import tpu as pltpu

F32 = jnp.float32
BF16 = jnp.bfloat16

EPS = 1e-6
LOG2E = math.log2(math.e)
NEG_BIG = -1e30

D_MODEL = 1024
MEM_LEN = 256
M_HEADS, M_DQK, M_DV = 4, 128, 256
CHUNK = 128
CONV_K = 4
D_HEADS, D_HD = 8, 64
D_DV = 2 * D_HD
C_HEADS, C_DQK, C_DV = 4, 128, 256
D_FF = 4 * D_MODEL
N_BRANCH = 3
LANES = 128
CQ_IF_W = 1024
MXU_DIM = 256

M_QK_W = M_HEADS * M_DQK
M_V_W = M_HEADS * M_DV
D_Q_W = D_HEADS * 2 * D_HD
D_V_W = D_HEADS * D_DV
C_Q_W = C_HEADS * C_DQK
C_V_W = C_HEADS * C_DV

VMEM_LIMIT = 56 * 1024 * 1024

PROJ_TM = 2048
PROJ_WIDE_TM = 1024
PROJ_NORM_TM = 1024
PROJ_TN = 1024
MERGE_TM = 512
MLP_TM = 1024
MLP_TF = 1024
MLSTM_NB = 4
DATTN_TQ_FAST = 4096
DATTN_TQ_STABILISED = 1024
DATTN_TK_FAST = 512
DATTN_TK_STABILISED = 256
MAX_UNSTABILISED_SCORE = 60.0


def _cparams(sem):
    return pltpu.CompilerParams(dimension_semantics=sem, vmem_limit_bytes=VMEM_LIMIT)


def _ep_identity(acc, aux_ref):
    return acc


def _ep_sigmoid_of_double(acc, aux_ref):
    return 0.5 * jnp.tanh(acc + aux_ref[0:1, :]) + 0.5


def _ep_bias(acc, aux_ref):
    return acc + aux_ref[0:1, :]


def _ep_group_norm(group, acc, aux_ref):
    tn = acc.shape[1]
    cw = MXU_DIM
    r = lax.broadcasted_iota(jnp.int32, (cw, cw), 0) // group
    c = lax.broadcasted_iota(jnp.int32, (cw, cw), 1) // group
    gmat = (r == c).astype(BF16)
    outs = []
    for s in range(tn // cw):
        a = acc[:, s * cw:(s + 1) * cw]
        ss = jnp.dot((a * a).astype(BF16), gmat, preferred_element_type=F32)
        outs.append(a * lax.rsqrt(ss * (1.0 / group) + EPS))
    y = jnp.concatenate(outs, axis=1) if len(outs) > 1 else outs[0]
    return y * aux_ref[0:1, :]


def _column_tiles(h, w_ref, aux_ref, o_ref, epilogue, tn):
    for j in range(w_ref.shape[1] // tn):
        cs = slice(j * tn, (j + 1) * tn)
        acc = jnp.dot(h, w_ref[:, cs], preferred_element_type=F32)
        o_ref[:, cs] = epilogue(acc, aux_ref.at[:, cs]).astype(o_ref.dtype)


def _norm_matmul_kernel(x_ref, g_ref, w_ref, aux_ref, o_ref, h_ref, *, epilogue, tn):
    x = x_ref[...].astype(F32)
    y = x * lax.rsqrt(jnp.mean(x * x, axis=-1, keepdims=True) + EPS)
    h = (y * g_ref[...]).astype(BF16)
    h_ref[...] = h
    _column_tiles(h, w_ref, aux_ref, o_ref, epilogue, tn)


def _matmul_kernel(h_ref, w_ref, aux_ref, o_ref, *, epilogue, tn):
    _column_tiles(h_ref[...], w_ref, aux_ref, o_ref, epilogue, tn)


def _window_specs(d, window, tn):
    c0, cols = window
    assert cols % tn == 0 and c0 % cols == 0
    jw = c0 // cols
    return [pl.BlockSpec((d, cols), lambda i: (0, jw), pipeline_mode=pl.Buffered(1)),
            pl.BlockSpec((8, cols), lambda i: (0, jw), pipeline_mode=pl.Buffered(1))]


def _norm_matmul(x, g, w, aux, window, epilogue, out_dtype, tm, tn, name):
    n, d = x.shape
    cols = window[1]
    tm = min(tm, n)
    tn = min(tn, cols)
    assert n % tm == 0
    return pl.pallas_call(
        functools.partial(_norm_matmul_kernel, epilogue=epilogue, tn=tn),
        grid=(n // tm,),
        in_specs=[pl.BlockSpec((tm, d), lambda i: (i, 0)), pl.BlockSpec((1, d), lambda i: (0, 0))]
        + _window_specs(d, window, tn),
        out_specs=[pl.BlockSpec((tm, cols), lambda i: (i, 0)), pl.BlockSpec((tm, d), lambda i: (i, 0))],
        out_shape=[jax.ShapeDtypeStruct((n, cols), out_dtype), jax.ShapeDtypeStruct((n, d), BF16)],
        compiler_params=_cparams(("parallel",)),
        name=name,
    )(x, g.reshape(1, d).astype(F32), w, aux)


def _matmul(h, w, aux, window, epilogue, out_dtype, tm, tn, name):
    n, d = h.shape
    cols = window[1]
    tm = min(tm, n)
    tn = min(tn, cols)
    assert n % tm == 0
    return pl.pallas_call(
        functools.partial(_matmul_kernel, epilogue=epilogue, tn=tn),
        grid=(n // tm,),
        in_specs=[pl.BlockSpec((tm, d), lambda i: (i, 0))] + _window_specs(d, window, tn),
        out_specs=pl.BlockSpec((tm, cols), lambda i: (i, 0)),
        out_shape=jax.ShapeDtypeStruct((n, cols), out_dtype),
        compiler_params=_cparams(("parallel",)),
        name=name,
    )(h, w, aux)


def _pack_w_in_kernel(w_ref, wg_ref, o_ref, *, split):
    w = w_ref[...]
    col = lambda a, b: w[:, split[a]:split[b]]
    parts = [0.5 * col(5, 6), 0.5 * wg_ref[...], col(6, 8), col(2, 3), col(8, 9)]
    for h in range(M_HEADS):
        parts += [col(0, 1)[:, h * M_DQK:(h + 1) * M_DQK], col(1, 2)[:, h * M_DQK:(h + 1) * M_DQK]]
    parts += [col(9, 10), col(3, 5), jnp.zeros((w.shape[0], CQ_IF_W - C_Q_W - 2 * M_HEADS), w.dtype)]
    o_ref[...] = jnp.concatenate(parts, axis=1).astype(BF16)


def _pack_w_in(w_in, w_gate, split, tr=256):
    d, cin = w_in.shape
    cout = cin - C_Q_W - 2 * M_HEADS + CQ_IF_W + w_gate.shape[1]
    return pl.pallas_call(
        functools.partial(_pack_w_in_kernel, split=split),
        grid=(d // tr,),
        in_specs=[pl.BlockSpec((tr, cin), lambda i: (i, 0)),
                  pl.BlockSpec((tr, w_gate.shape[1]), lambda i: (i, 0))],
        out_specs=pl.BlockSpec((tr, cout), lambda i: (i, 0)),
        out_shape=jax.ShapeDtypeStruct((d, cout), BF16),
        compiler_params=_cparams(("parallel",)),
        name="pack_w_in",
    )(w_in, w_gate)


def _cq_if_kernel(h_ref, w_ref, aux_ref, ocq_ref, oif_ref):
    h = h_ref[...]
    acc = jnp.dot(h, w_ref[:, :C_Q_W], preferred_element_type=F32)
    ocq_ref[...] = _ep_group_norm(C_DQK, acc, aux_ref.at[:, :C_Q_W]).astype(ocq_ref.dtype)
    gl = slice(C_Q_W, C_Q_W + LANES)
    oif_ref[...] = jnp.dot(h, w_ref[:, gl], preferred_element_type=F32) + aux_ref[0:1, gl]


def _proj_cq_if(h, w, aux, window, tm):
    n, d = h.shape
    tm = min(tm, n)
    assert window[1] == CQ_IF_W
    return pl.pallas_call(
        _cq_if_kernel,
        grid=(n // tm,),
        in_specs=[pl.BlockSpec((tm, d), lambda i: (i, 0))] + _window_specs(d, window, CQ_IF_W),
        out_specs=[pl.BlockSpec((tm, C_Q_W), lambda i: (i, 0)), pl.BlockSpec((tm, LANES), lambda i: (i, 0))],
        out_shape=[jax.ShapeDtypeStruct((n, C_Q_W), BF16), jax.ShapeDtypeStruct((n, LANES), F32)],
        compiler_params=_cparams(("parallel",)),
        name="proj_cq_if",
    )(h, w, aux)


def _aux_rows(row):
    return jnp.zeros((8, row.shape[0]), F32).at[0].set(row.astype(F32))


def _mlstm_kernel(qk_ref, v_ref, gate_ref, so_ref, cw_ref, cb_ref, ng_ref, o_ref,
                  xprev_ref, c_ref, n_ref, m_ref):
    L = CHUNK
    dk, dv, nh = M_DQK, M_DV, M_HEADS
    nb = qk_ref.shape[0]

    @pl.when(pl.program_id(1) == 0)
    def _():
        xprev_ref[...] = jnp.zeros_like(xprev_ref)
        c_ref[...] = jnp.zeros_like(c_ref)
        n_ref[...] = jnp.zeros_like(n_ref)
        m_ref[...] = jnp.zeros_like(m_ref)

    n_sh = CONV_K - 1
    srow = lax.broadcasted_iota(jnp.int32, (n_sh * L, 2 * L), 0)
    scol = lax.broadcasted_iota(jnp.int32, (n_sh * L, 2 * L), 1)
    sel = (scol == (srow % L) + L - 1 - srow // L).astype(BF16)
    rr = lax.broadcasted_iota(jnp.int32, (L, L), 0)
    cc = lax.broadcasted_iota(jnp.int32, (L, L), 1)
    causal = cc <= rr
    tri = causal.astype(BF16)

    act, g, g_t, bcum, bcum_t = [], [], [], [], []
    for bb in range(nb):
        xb = qk_ref[bb]
        xx = jnp.concatenate([xprev_ref[bb], xb], axis=0)
        shifted = jnp.dot(sel, xx, preferred_element_type=F32)
        xprev_ref[bb] = xb
        z = cb_ref[0:1, :] + cw_ref[CONV_K - 1:CONV_K, :] * xb.astype(F32)
        for s in range(n_sh):
            z = z + cw_ref[CONV_K - 2 - s:CONV_K - 1 - s, :] * shifted[s * L:(s + 1) * L, :]
        act.append(z + z * jnp.tanh(z))

        gg = gate_ref[bb]
        lf = jnp.minimum(gg, 0.0) - jnp.log1p(jnp.exp(-jnp.abs(gg)))
        lf_hi = lf.astype(BF16)
        lf_lo = (lf - lf_hi.astype(F32)).astype(BF16)
        bc = (jnp.dot(tri, lf_hi, preferred_element_type=F32)
              + jnp.dot(tri, lf_lo, preferred_element_type=F32))
        g.append(gg)
        g_t.append(gg.T)
        bcum.append(bc)
        bcum_t.append(bc.T)

    streams = [(bb, h) for bb in range(nb) for h in range(nh)]
    heads = range(len(streams))
    pick_r = lax.broadcasted_iota(jnp.int32, (2 * LANES, LANES), 0) % LANES

    def hi_lo(a):
        a_hi = a.astype(BF16)
        return jnp.concatenate([a_hi, (a - a_hi.astype(F32)).astype(BF16)], axis=1)

    def lane_column(a_hi_lo, lane):
        return jnp.dot(a_hi_lo, (pick_r == lane).astype(BF16), preferred_element_type=F32)

    wide = lambda col: jnp.concatenate([col, col], axis=1)
    bcum_hl = [hi_lo(a) for a in bcum]
    g_hl = [hi_lo(a) for a in g]
    bcol = [lane_column(bcum_hl[bb], nh + h) for bb, h in streams]
    brow = [bcum_t[bb][nh + h:nh + h + 1, :] for bb, h in streams]
    irow = [g_t[bb][h:h + 1, :] for bb, h in streams]
    icol = [lane_column(g_hl[bb], h) for bb, h in streams]
    m_old = [m_ref[i] for i in heads]
    c_old = [c_ref[i] for i in heads]
    n_old = [n_ref[i] for i in heads]
    qh = [act[bb][:, 2 * dk * h:2 * dk * h + dk] * (dk ** -0.5) for bb, h in streams]
    kh = [act[bb][:, 2 * dk * h + dk:2 * dk * (h + 1)] for bb, h in streams]
    vh = [v_ref[bb, :, h * dv:(h + 1) * dv] for bb, h in streams]
    qb = [q.astype(BF16) for q in qh]
    kb = [k.astype(BF16) for k in kh]

    qk_t = [lax.dot_general(qb[h], kb[h], (((1,), (1,)), ((), ())), preferred_element_type=F32) for h in heads]
    q_c = [jnp.dot(qb[h], c_old[h].astype(BF16), preferred_element_type=F32) for h in heads]
    logd = [jnp.where(causal, bcol[h] - brow[h] + irow[h], -jnp.inf) for h in heads]
    inter = [bcol[h] + m_old[h] for h in heads]
    m_loc = [jnp.maximum(inter[h], jnp.max(logd[h], axis=-1, keepdims=True)) for h in heads]
    w_inter = [jnp.exp(inter[h] - m_loc[h]) for h in heads]
    s = [qk_t[h] * jnp.exp(logd[h] - m_loc[h]) for h in heads]
    num = [jnp.dot(s[h].astype(BF16), vh[h], preferred_element_type=F32) + wide(w_inter[h]) * q_c[h] for h in heads]
    den = [jnp.sum(s[h], axis=-1, keepdims=True) + w_inter[h] * jnp.sum(qh[h] * n_old[h], axis=-1, keepdims=True)
           for h in heads]

    b_end = [bcol[h][L - 1:L, :] for h in heads]
    log_w = [b_end[h] - bcol[h] + icol[h] for h in heads]
    m_new = [jnp.maximum(b_end[h] + m_old[h], jnp.max(log_w[h], axis=0, keepdims=True)) for h in heads]
    kw = [kh[h] * jnp.exp(log_w[h] - m_new[h]) for h in heads]
    decay = [jnp.exp(b_end[h] + m_old[h] - m_new[h]) for h in heads]
    for h in heads:
        c_ref[h] = wide(decay[h]) * c_old[h] + lax.dot_general(
            kw[h].astype(BF16), vh[h], (((0,), (0,)), ((), ())), preferred_element_type=F32)
        n_ref[h] = decay[h] * n_old[h] + jnp.sum(kw[h], axis=0, keepdims=True)
        m_ref[h] = m_new[h][:, 0:1]

    for i, (bb, h) in enumerate(streams):
        d = jnp.maximum(jnp.abs(den[i]), jnp.exp(-m_loc[i]))
        row_scale = lax.rsqrt(jnp.mean(num[i] * num[i], axis=-1, keepdims=True) + EPS * (d * d))
        yn = num[i] * wide(row_scale) * ng_ref[0:1, h * dv:(h + 1) * dv]
        o_ref[bb, :, h * dv:(h + 1) * dv] = so_ref[bb, :, h * dv:(h + 1) * dv] * yn.astype(o_ref.dtype)


def _mlstm(qk, qk_col, vproj, v_col, gates, so, so_col, conv_w, conv_b, norm_g, bsz, seq, nb):
    nc = seq // CHUNK
    L = CHUNK
    assert bsz % nb == 0
    by_batch = lambda a: a.reshape(bsz, seq, a.shape[-1])
    out = pl.pallas_call(
        _mlstm_kernel,
        grid=(bsz // nb, nc),
        in_specs=[
            pl.BlockSpec((nb, L, 2 * M_QK_W), lambda b, c: (b, c, qk_col)),
            pl.BlockSpec((nb, L, M_V_W), lambda b, c: (b, c, v_col)),
            pl.BlockSpec((nb, L, LANES), lambda b, c: (b, c, 0)),
            pl.BlockSpec((nb, L, M_V_W), lambda b, c: (b, c, so_col)),
            pl.BlockSpec((CONV_K, 2 * M_QK_W), lambda b, c: (0, 0)),
            pl.BlockSpec((8, 2 * M_QK_W), lambda b, c: (0, 0)),
            pl.BlockSpec((8, M_V_W), lambda b, c: (0, 0)),
        ],
        out_specs=pl.BlockSpec((nb, L, M_V_W), lambda b, c: (b, c, 0)),
        out_shape=jax.ShapeDtypeStruct((bsz, seq, M_V_W), BF16),
        scratch_shapes=[
            pltpu.VMEM((nb, L, 2 * M_QK_W), BF16),
            pltpu.VMEM((nb * M_HEADS, M_DQK, M_DV), F32),
            pltpu.VMEM((nb * M_HEADS, 1, M_DQK), F32),
            pltpu.VMEM((nb * M_HEADS, 1, 1), F32),
        ],
        compiler_params=_cparams(("parallel", "arbitrary")),
        name="mlstm",
    )(by_batch(qk), by_batch(vproj), by_batch(gates), by_batch(so), conv_w.astype(F32), _aux_rows(conv_b),
      _aux_rows(norm_g))
    return out.reshape(bsz * seq, M_V_W)


def _dattn_load_tile(q_ref, v_ref, vt_ref, seq):
    tv = 256

    @pl.when(pl.program_id(2) == 0)
    def _():
        for c in range(seq // tv):
            blk = v_ref[c * tv:(c + 1) * tv, :].astype(F32)
            vt_ref[:, c * tv:(c + 1) * tv] = blk.T.astype(BF16)

    qt = q_ref[...].astype(F32).T
    row = lax.broadcasted_iota(jnp.int32, qt.shape, 0)
    return (jnp.where(row < D_HD, qt, 0.0).astype(BF16), jnp.where(row >= D_HD, qt, 0.0).astype(BF16))


def _dattn_finish(lam_ref, sg_ref, o_ref, acc_ref, l1, l2, out_scale):
    ot = acc_ref[0] * (1.0 / l1) - acc_ref[1] * (lam_ref[0] / l2)
    yt = ot * lax.rsqrt(jnp.mean(ot * ot, axis=0, keepdims=True) + EPS)
    o_ref[...] = (yt.T * (sg_ref[0:1, :] * out_scale)).astype(o_ref.dtype)


def _causal_mask(s):
    kr = lax.broadcasted_iota(jnp.int32, s.shape, 0)
    qc = lax.broadcasted_iota(jnp.int32, s.shape, 1)
    return jnp.where(kr <= qc, s, NEG_BIG)


def _dattn_kernel(lam_ref, q_ref, k_ref, v_ref, sg_ref, o_ref, vt_ref, acc_ref, qz_ref, l_ref, p_ref, *,
                  seq, tq, tk, out_scale):
    n_diag = tq // tk
    assert n_diag * tk == tq and n_diag % 2 == 0
    qi = pl.program_id(2)
    qz = _dattn_load_tile(q_ref, v_ref, vt_ref, seq)
    qz_ref[0] = qz[0]
    qz_ref[1] = qz[1]
    acc_ref[...] = jnp.zeros_like(acc_ref)
    l_ref[...] = jnp.zeros_like(l_ref)

    def scores(j, lo, slot):
        q0 = lo or 0
        kj = k_ref[pl.ds(pl.multiple_of(j * tk, tk), tk), :]
        for c in range(2):
            s = jnp.dot(kj, qz_ref[c, :, q0:], preferred_element_type=F32)
            if lo is not None:
                s = _causal_mask(s)
            p = jnp.exp2(s)
            l_ref[c, :, q0:] += jnp.sum(p, axis=0, keepdims=True)
            p_ref[slot, c, :, q0:] = p.astype(BF16)

    def accumulate(j, lo, slot):
        q0 = lo or 0
        vtj = vt_ref[:, pl.ds(pl.multiple_of(j * tk, tk), tk)]
        for c in range(2):
            acc_ref[c, :, q0:] += jnp.dot(vtj, p_ref[slot, c, :, q0:], preferred_element_type=F32)

    n = n_diag * qi

    @pl.when(qi > 0)
    def _():
        scores(0, None, 0)

    def pair(i, carry):
        j = 2 * i
        accumulate(j, None, 0)
        scores(j + 1, None, 1)
        accumulate(j + 1, None, 1)
        scores(j + 2, None, 0)
        return carry

    lax.fori_loop(0, n // 2 - 1, pair, 0)

    @pl.when(qi > 0)
    def _():
        accumulate(n - 2, None, 0)
        scores(n - 1, None, 1)
        accumulate(n - 1, None, 1)
        scores(n, 0, 0)

    @pl.when(qi == 0)
    def _():
        scores(n, 0, 0)

    for d in range(n_diag):
        accumulate(n + d, d * tk, d % 2)
        if d + 1 < n_diag:
            scores(n + d + 1, (d + 1) * tk, (d + 1) % 2)
    _dattn_finish(lam_ref, sg_ref, o_ref, acc_ref, l_ref[0], l_ref[1], out_scale)


def _dattn_stabilised_kernel(lam_ref, q_ref, k_ref, v_ref, sg_ref, o_ref, vt_ref, acc_ref, *,
                             seq, tq, tk, out_scale):
    qi = pl.program_id(2)
    qz = _dattn_load_tile(q_ref, v_ref, vt_ref, seq)
    acc_ref[...] = jnp.zeros_like(acc_ref)

    def chunk(j, carry, lo):
        start = pl.multiple_of(j * tk, tk)
        kj = k_ref[pl.ds(start, tk), :]
        vtj = vt_ref[:, pl.ds(start, tk)]
        q0 = lo or 0
        out = []
        for c in range(2):
            m, l = carry[2 * c][:, q0:], carry[2 * c + 1][:, q0:]
            s = jnp.dot(kj, qz[c][:, q0:], preferred_element_type=F32)
            if lo is not None:
                s = _causal_mask(s)
            m_new = jnp.maximum(m, jnp.max(s, axis=0, keepdims=True))
            p = jnp.exp2(s - m_new)
            alpha = jnp.exp2(m - m_new)
            l_new = alpha * l + jnp.sum(p, axis=0, keepdims=True)
            acc_ref[c, :, q0:] = alpha * acc_ref[c, :, q0:] + jnp.dot(
                vtj, p.astype(BF16), preferred_element_type=F32)
            if q0:
                m_new = jnp.concatenate([carry[2 * c][:, :q0], m_new], axis=1)
                l_new = jnp.concatenate([carry[2 * c + 1][:, :q0], l_new], axis=1)
            out += [m_new, l_new]
        return tuple(out)

    init = (jnp.full((1, tq), NEG_BIG, F32), jnp.zeros((1, tq), F32)) * 2
    n_full = qi * (tq // tk)
    carry = lax.fori_loop(0, n_full, lambda j, c: chunk(j, c, None), init)
    for d in range(tq // tk):
        carry = chunk(n_full + d, carry, d * tk)

    _dattn_finish(lam_ref, sg_ref, o_ref, acc_ref, carry[1], carry[3], out_scale)


def _diff_attention(lam, qk, v, v_col0, subln_g, bsz, seq, out_scale, stabilised, tq, tk):
    n = bsz * seq
    tq = min(tq, seq)
    tk = min(tk, tq)
    nq = seq // tq
    hw = 2 * D_HD
    scratch = [pltpu.VMEM((D_DV, seq), BF16), pltpu.VMEM((2, D_DV, tq), F32)]
    if stabilised:
        body = _dattn_stabilised_kernel
    else:
        body = _dattn_kernel
        scratch += [pltpu.VMEM((2, hw, tq), BF16), pltpu.VMEM((2, 1, tq), F32), pltpu.VMEM((2, 2, tk, tq), BF16)]
    return pl.pallas_call(
        functools.partial(body, seq=seq, tq=tq, tk=tk, out_scale=out_scale),
        grid=(bsz, D_HEADS, nq),
        in_specs=[
            pl.BlockSpec(memory_space=pltpu.SMEM),
            pl.BlockSpec((tq, hw), lambda b, h, i: (b * nq + i, h)),
            pl.BlockSpec((seq, hw), lambda b, h, i: (b, D_HEADS + h)),
            pl.BlockSpec((seq, D_DV), lambda b, h, i: (b, v_col0 + h)),
            pl.BlockSpec((8, D_DV), lambda b, h, i: (0, 0)),
        ],
        out_specs=pl.BlockSpec((tq, D_DV), lambda b, h, i: (b * nq + i, h)),
        out_shape=jax.ShapeDtypeStruct((n, D_V_W), BF16),
        scratch_shapes=scratch,
        compiler_params=_cparams(("parallel", "parallel", "arbitrary")),
        name="diff_attn_stabilised" if stabilised else "diff_attn",
    )(lam.reshape(1).astype(F32), qk, qk, v, _aux_rows(subln_g))


def _xattn_kernel(q_ref, mk_ref, mv_ref, o_ref):
    for h in range(C_HEADS):
        q = q_ref[:, h * C_DQK:(h + 1) * C_DQK]
        k = mk_ref[:, h * C_DQK:(h + 1) * C_DQK]
        s = lax.dot_general(q, k, (((1,), (1,)), ((), ())), preferred_element_type=F32)
        m = jnp.max(s, axis=-1, keepdims=True)
        p = jnp.exp2(s - m)
        l = jnp.sum(p, axis=-1, keepdims=True)
        o = jnp.dot(p.astype(BF16), mv_ref[:, h * C_DV:(h + 1) * C_DV], preferred_element_type=F32)
        o_ref[:, h * C_DV:(h + 1) * C_DV] = (o / l).astype(o_ref.dtype)


def _cross_attention(q, mk, mv, bsz, seq, mem_len, tq=2048):
    n = bsz * seq
    tq = min(tq, seq)
    nq = seq // tq
    return pl.pallas_call(
        _xattn_kernel,
        grid=(bsz, nq),
        in_specs=[
            pl.BlockSpec((tq, C_Q_W), lambda b, i: (b * nq + i, 0)),
            pl.BlockSpec((mem_len, C_Q_W), lambda b, i: (b, 0)),
            pl.BlockSpec((mem_len, C_V_W), lambda b, i: (b, 0)),
        ],
        out_specs=pl.BlockSpec((tq, C_V_W), lambda b, i: (b * nq + i, 0)),
        out_shape=jax.ShapeDtypeStruct((n, C_V_W), BF16),
        compiler_params=_cparams(("parallel", "parallel")),
        name="cross_attn",
    )(q, mk, mv)


def _merge_kernel(x_ref, ym_ref, yd_ref, yc_ref, gm_ref, gd_ref, gc_ref,
                  wm_ref, wd_ref, wc_ref, wo_ref, g_ref, o_ref, h_ref):
    merged = gm_ref[...].astype(F32) * jnp.dot(ym_ref[...], wm_ref[...], preferred_element_type=F32)
    merged = merged + gd_ref[...].astype(F32) * jnp.dot(yd_ref[...], wd_ref[...], preferred_element_type=F32)
    merged = merged + gc_ref[...].astype(F32) * jnp.dot(yc_ref[...], wc_ref[...], preferred_element_type=F32)
    x1 = x_ref[...] + jnp.dot(merged.astype(BF16), wo_ref[...], preferred_element_type=F32)
    o_ref[...] = x1
    y = x1 * lax.rsqrt(jnp.mean(x1 * x1, axis=-1, keepdims=True) + EPS)
    h_ref[...] = (y * g_ref[...]).astype(BF16)


def _merge(x, ym, yd, yc, gates, g_col0, wm, wd, wc, wo, g_next, tm):
    n, d = x.shape
    tm = min(tm, n)
    row = lambda i: (i, 0)
    wspec = pl.BlockSpec((d, d), lambda i: (0, 0), pipeline_mode=pl.Buffered(1))
    return pl.pallas_call(
        _merge_kernel,
        grid=(n // tm,),
        in_specs=[
            pl.BlockSpec((tm, d), row),
            pl.BlockSpec((tm, d), row),
            pl.BlockSpec((tm, d), row),
            pl.BlockSpec((tm, d), row),
            pl.BlockSpec((tm, d), lambda i: (i, g_col0)),
            pl.BlockSpec((tm, d), lambda i: (i, g_col0 + 1)),
            pl.BlockSpec((tm, d), lambda i: (i, g_col0 + 2)),
            wspec, wspec, wspec, wspec,
            pl.BlockSpec((1, d), lambda i: (0, 0)),
        ],
        out_specs=[pl.BlockSpec((tm, d), row), pl.BlockSpec((tm, d), row)],
        out_shape=[jax.ShapeDtypeStruct((n, d), F32), jax.ShapeDtypeStruct((n, d), BF16)],
        compiler_params=_cparams(("parallel",)),
        name="merge",
    )(x, ym, yd, yc, gates, gates, gates, wm, wd, wc, wo, g_next.reshape(1, d).astype(F32))


def _mlp_kernel(x_ref, h_ref, wu_ref, wd_ref, o_ref, *, tf):
    h = h_ref[...]
    acc = x_ref[...]
    for f in range(wu_ref.shape[1] // tf):
        u = jnp.maximum(jnp.dot(h, wu_ref[:, f * tf:(f + 1) * tf], preferred_element_type=F32), 0.0)
        acc = acc + jnp.dot((u * u).astype(BF16), wd_ref[f * tf:(f + 1) * tf, :], preferred_element_type=F32)
    o_ref[...] = acc


def _mlp(x, h, wu, wd, tm, tf):
    n, d = x.shape
    dff = wu.shape[1]
    tm = min(tm, n)
    return pl.pallas_call(
        functools.partial(_mlp_kernel, tf=tf),
        grid=(n // tm,),
        in_specs=[
            pl.BlockSpec((tm, d), lambda i: (i, 0)),
            pl.BlockSpec((tm, d), lambda i: (i, 0)),
            pl.BlockSpec((d, dff), lambda i: (0, 0), pipeline_mode=pl.Buffered(1)),
            pl.BlockSpec((dff, d), lambda i: (0, 0), pipeline_mode=pl.Buffered(1)),
        ],
        out_specs=pl.BlockSpec((tm, d), lambda i: (i, 0)),
        out_shape=jax.ShapeDtypeStruct((n, d), F32),
        compiler_params=_cparams(("parallel",)),
        name="mlp",
    )(x, h, wu, wd)


def _lambda_init(layer):
    return 0.8 - 0.6 * math.exp(-0.3 * layer)


def _layer(l, x2, mem2, bsz, seq, mem_len, p):
    split = [0]
    for w in (M_QK_W, M_QK_W, M_V_W, M_HEADS, M_HEADS, M_V_W, D_Q_W, D_Q_W, D_V_W, C_Q_W):
        split.append(split[-1] + w)
    w_in = p['w_in'][l]
    g_mix = p['norm_mix_g'][l]

    def head_major(a):
        lead = a.shape[:-1]
        return a.reshape(lead + (2, M_HEADS, M_DQK)).swapaxes(-3, -2).reshape(lead + (2 * M_QK_W,))

    conv_w_half = 0.5 * head_major(p['conv_w'][l].astype(F32))
    conv_b_half = 0.5 * head_major(p['conv_b'][l].astype(F32))

    q_gain = jnp.tile(p['dq_norm_g'][l].astype(F32), D_Q_W // D_HD) * (D_HD ** -0.5 * LOG2E)
    k_gain = jnp.tile(p['dk_norm_g'][l].astype(F32), D_Q_W // D_HD)
    cq_gain = jnp.tile(p['cq_norm_g'][l].astype(F32), C_HEADS) * (C_DQK ** -0.5 * LOG2E)
    if_pad = CQ_IF_W - C_Q_W - 2 * M_HEADS
    w_all = _pack_w_in(w_in, p['w_gate'][l], tuple(split))
    aux_all = _aux_rows(jnp.concatenate(
        [jnp.zeros((M_V_W,), F32), 0.5 * p['b_gate'][l].astype(F32), q_gain, k_gain,
         jnp.zeros((M_V_W + D_V_W + 2 * M_QK_W,), F32), cq_gain, p['b_igate'][l].astype(F32),
         p['b_fgate'][l].astype(F32), jnp.zeros((if_pad,), F32)]))
    win = {}
    c0 = 0
    for name, width in (("gates", M_V_W + N_BRANCH * D_MODEL), ("dqk", 2 * D_Q_W),
                        ("vqk", M_V_W + D_V_W + 2 * M_QK_W), ("cq_if", CQ_IF_W)):
        win[name] = (c0, width)
        c0 += width

    vqk, h = _norm_matmul(x2, g_mix, w_all, aux_all, win["vqk"], _ep_identity, BF16, 1024, PROJ_TN, "proj_vqk")
    sgates = _matmul(h, w_all, aux_all, win["gates"], _ep_sigmoid_of_double, BF16, PROJ_WIDE_TM, PROJ_TN,
                     "proj_gates")
    dqk = _matmul(h, w_all, aux_all, win["dqk"], functools.partial(_ep_group_norm, D_HD), BF16,
                  PROJ_NORM_TM, PROJ_TN, "proj_dqk")
    cq, gates_if = _proj_cq_if(h, w_all, aux_all, win["cq_if"], PROJ_TM)
    w_kv = p['w_mem_kv'][l]
    w_kv = jnp.concatenate([w_kv[:, C_Q_W:], w_kv[:, :C_Q_W]], axis=1).astype(BF16)
    aux_kv = _aux_rows(jnp.concatenate([jnp.zeros((C_V_W,), F32), jnp.tile(p['ck_norm_g'][l].astype(F32), C_HEADS)]))
    mk, hmem = _norm_matmul(mem2, p['mem_norm_g'][l], w_kv, aux_kv, (C_V_W, C_Q_W),
                            functools.partial(_ep_group_norm, C_DQK), BF16, 1024, C_Q_W, "proj_mk")
    mv = _matmul(hmem, w_kv, aux_kv, (0, C_V_W), _ep_identity, BF16, 1024, C_Q_W, "proj_mv")

    y_m = _mlstm(vqk, (M_V_W + D_V_W) // (2 * M_QK_W), vqk, 0, gates_if, sgates, 0, conv_w_half, conv_b_half,
                 p['m_norm_g'][l], bsz, seq, MLSTM_NB if bsz % MLSTM_NB == 0 else 1)

    lam_i = _lambda_init(l)
    lam = (jnp.exp(jnp.sum(p['lam_q1'][l].astype(F32) * p['lam_k1'][l].astype(F32)))
           - jnp.exp(jnp.sum(p['lam_q2'][l].astype(F32) * p['lam_k2'][l].astype(F32))) + lam_i)
    score_bound = 1.02 * D_HD * jnp.max(jnp.abs(q_gain)) * jnp.max(jnp.abs(k_gain))
    dattn = functools.partial(_diff_attention, lam, dqk, vqk, M_V_W // D_DV, p['subln_g'][l], bsz, seq,
                              1.0 - lam_i)
    y_d = lax.cond(score_bound <= MAX_UNSTABILISED_SCORE,
                   lambda: dattn(False, DATTN_TQ_FAST, DATTN_TK_FAST),
                   lambda: dattn(True, DATTN_TQ_STABILISED, DATTN_TK_STABILISED))

    y_c = _cross_attention(cq, mk, mv, bsz, seq, mem_len)

    x2, h2 = _merge(x2, y_m, y_d, y_c, sgates, 1,
                    p['w_proj_m'][l].astype(BF16), p['w_proj_d'][l].astype(BF16),
                    p['w_proj_c'][l].astype(BF16), p['w_out'][l].astype(BF16), p['norm_mlp_g'][l], MERGE_TM)
    return _mlp(x2, h2, p['w_up'][l].astype(BF16), p['w_down'][l].astype(BF16), MLP_TM, MLP_TF)


def kernel(x, mem, norm_mix_g, w_in, b_igate, b_fgate, conv_w, conv_b, m_norm_g, dq_norm_g, dk_norm_g, lam_q1, lam_k1, lam_q2, lam_k2, subln_g, cq_norm_g, ck_norm_g, mem_norm_g, w_mem_kv, w_gate, b_gate, w_proj_m, w_proj_d, w_proj_c, w_out, norm_mlp_g, w_up, w_down):
    p = dict(norm_mix_g=norm_mix_g, w_in=w_in, b_igate=b_igate, b_fgate=b_fgate, conv_w=conv_w, conv_b=conv_b,
             m_norm_g=m_norm_g, dq_norm_g=dq_norm_g, dk_norm_g=dk_norm_g, lam_q1=lam_q1, lam_k1=lam_k1,
             lam_q2=lam_q2, lam_k2=lam_k2, subln_g=subln_g, cq_norm_g=cq_norm_g, ck_norm_g=ck_norm_g,
             mem_norm_g=mem_norm_g, w_mem_kv=w_mem_kv, w_gate=w_gate, b_gate=b_gate, w_proj_m=w_proj_m,
             w_proj_d=w_proj_d, w_proj_c=w_proj_c, w_out=w_out, norm_mlp_g=norm_mlp_g, w_up=w_up, w_down=w_down)
    bsz, seq, d = x.shape
    mem_len = mem.shape[1]
    x2 = x.reshape(bsz * seq, d)
    mem2 = mem.reshape(bsz * mem_len, d)
    for l in range(w_in.shape[0]):
        x2 = _layer(l, x2, mem2, bsz, seq, mem_len, p)
    return x2.reshape(bsz, seq, d)
```

```python
import functools
import math

import jax
import jax.numpy as jnp
from jax import lax
from jax.experimental import pallas as pl
from jax.experimental.pallas import tpu as pltpu

F32 = jnp.float32
BF16 = jnp.bfloat16

EPS = 1e-6
LOG2E = math.log2(math.e)
NEG_BIG = -1e30

D_MODEL = 1024
MEM_LEN = 256
M_HEADS, M_DQK, M_DV = 4, 128, 256
CHUNK = 128
CONV_K = 4
D_HEADS, D_HD = 8, 64
D_DV = 2 * D_HD
C_HEADS, C_DQK, C_DV = 4, 128, 256
D_FF = 4 * D_MODEL
N_BRANCH = 3
LANES = 128
CQ_IF_W = 1024
MXU_DIM = 256

M_QK_W = M_HEADS * M_DQK
M_V_W = M_HEADS * M_DV
D_Q_W = D_HEADS * 2 * D_HD
D_V_W = D_HEADS * D_DV
C_Q_W = C_HEADS * C_DQK
C_V_W = C_HEADS * C_DV

VMEM_LIMIT = 56 * 1024 * 1024

PROJ_TM = 2048
PROJ_WIDE_TM = 1024
PROJ_NORM_TM = 1024
PROJ_TN = 1024
MERGE_TM = 512
MLP_TM = 1024
MLP_TF = 1024
MLSTM_NB = 4
DATTN_TQ_FAST = 4096
DATTN_TQ_STABILISED = 1024
DATTN_TK_FAST = 512
DATTN_TK_STABILISED = 256
MAX_UNSTABILISED_SCORE = 60.0


def _cparams(sem):
    return pltpu.CompilerParams(dimension_semantics=sem, vmem_limit_bytes=VMEM_LIMIT)


def _ep_identity(acc, aux_ref):
    return acc


def _ep_sigmoid_of_double(acc, aux_ref):
    return 0.5 * jnp.tanh(acc + aux_ref[0:1, :]) + 0.5


def _ep_bias(acc, aux_ref):
    return acc + aux_ref[0:1, :]


def _ep_group_norm(group, acc, aux_ref):
    tn = acc.shape[1]
    cw = MXU_DIM
    r = lax.broadcasted_iota(jnp.int32, (cw, cw), 0) // group
    c = lax.broadcasted_iota(jnp.int32, (cw, cw), 1) // group
    gmat = (r == c).astype(BF16)
    outs = []
    for s in range(tn // cw):
        a = acc[:, s * cw:(s + 1) * cw]
        ss = jnp.dot((a * a).astype(BF16), gmat, preferred_element_type=F32)
        outs.append(a * lax.rsqrt(ss * (1.0 / group) + EPS))
    y = jnp.concatenate(outs, axis=1) if len(outs) > 1 else outs[0]
    return y * aux_ref[0:1, :]


def _column_tiles(h, w_ref, aux_ref, o_ref, epilogue, tn):
    for j in range(w_ref.shape[1] // tn):
        cs = slice(j * tn, (j + 1) * tn)
        acc = jnp.dot(h, w_ref[:, cs], preferred_element_type=F32)
        o_ref[:, cs] = epilogue(acc, aux_ref.at[:, cs]).astype(o_ref.dtype)


def _norm_matmul_kernel(x_ref, g_ref, w_ref, aux_ref, o_ref, h_ref, *, epilogue, tn):
    x = x_ref[...].astype(F32)
    y = x * lax.rsqrt(jnp.mean(x * x, axis=-1, keepdims=True) + EPS)
    h = (y * g_ref[...]).astype(BF16)
    h_ref[...] = h
    _column_tiles(h, w_ref, aux_ref, o_ref, epilogue, tn)


def _matmul_kernel(h_ref, w_ref, aux_ref, o_ref, *, epilogue, tn):
    _column_tiles(h_ref[...], w_ref, aux_ref, o_ref, epilogue, tn)


def _window_specs(d, window, tn):
    c0, cols = window
    assert cols % tn == 0 and c0 % cols == 0
    jw = c0 // cols
    return [pl.BlockSpec((d, cols), lambda i: (0, jw), pipeline_mode=pl.Buffered(1)),
            pl.BlockSpec((8, cols), lambda i: (0, jw), pipeline_mode=pl.Buffered(1))]


def _norm_matmul(x, g, w, aux, window, epilogue, out_dtype, tm, tn, name):
    n, d = x.shape
    cols = window[1]
    tm = min(tm, n)
    tn = min(tn, cols)
    assert n % tm == 0
    return pl.pallas_call(
        functools.partial(_norm_matmul_kernel, epilogue=epilogue, tn=tn),
        grid=(n // tm,),
        in_specs=[pl.BlockSpec((tm, d), lambda i: (i, 0)), pl.BlockSpec((1, d), lambda i: (0, 0))]
        + _window_specs(d, window, tn),
        out_specs=[pl.BlockSpec((tm, cols), lambda i: (i, 0)), pl.BlockSpec((tm, d), lambda i: (i, 0))],
        out_shape=[jax.ShapeDtypeStruct((n, cols), out_dtype), jax.ShapeDtypeStruct((n, d), BF16)],
        compiler_params=_cparams(("parallel",)),
        name=name,
    )(x, g.reshape(1, d).astype(F32), w, aux)


def _matmul(h, w, aux, window, epilogue, out_dtype, tm, tn, name):
    n, d = h.shape
    cols = window[1]
    tm = min(tm, n)
    tn = min(tn, cols)
    assert n % tm == 0
    return pl.pallas_call(
        functools.partial(_matmul_kernel, epilogue=epilogue, tn=tn),
        grid=(n // tm,),
        in_specs=[pl.BlockSpec((tm, d), lambda i: (i, 0))] + _window_specs(d, window, tn),
        out_specs=pl.BlockSpec((tm, cols), lambda i: (i, 0)),
        out_shape=jax.ShapeDtypeStruct((n, cols), out_dtype),
        compiler_params=_cparams(("parallel",)),
        name=name,
    )(h, w, aux)


def _pack_w_in_kernel(w_ref, wg_ref, o_ref, *, split):
    w = w_ref[...]
    col = lambda a, b: w[:, split[a]:split[b]]
    parts = [0.5 * col(5, 6), 0.5 * wg_ref[...], col(6, 8), col(2, 3), col(8, 9)]
    for h in range(M_HEADS):
        parts += [col(0, 1)[:, h * M_DQK:(h + 1) * M_DQK], col(1, 2)[:, h * M_DQK:(h + 1) * M_DQK]]
    parts += [col(9, 10), col(3, 5), jnp.zeros((w.shape[0], CQ_IF_W - C_Q_W - 2 * M_HEADS), w.dtype)]
    o_ref[...] = jnp.concatenate(parts, axis=1).astype(BF16)


def _pack_w_in(w_in, w_gate, split, tr=256):
    d, cin = w_in.shape
    cout = cin - C_Q_W - 2 * M_HEADS + CQ_IF_W + w_gate.shape[1]
    return pl.pallas_call(
        functools.partial(_pack_w_in_kernel, split=split),
        grid=(d // tr,),
        in_specs=[pl.BlockSpec((tr, cin), lambda i: (i, 0)),
                  pl.BlockSpec((tr, w_gate.shape[1]), lambda i: (i, 0))],
        out_specs=pl.BlockSpec((tr, cout), lambda i: (i, 0)),
        out_shape=jax.ShapeDtypeStruct((d, cout), BF16),
        compiler_params=_cparams(("parallel",)),
        name="pack_w_in",
    )(w_in, w_gate)


def _cq_if_kernel(h_ref, w_ref, aux_ref, ocq_ref, oif_ref):
    h = h_ref[...]
    acc = jnp.dot(h, w_ref[:, :C_Q_W], preferred_element_type=F32)
    ocq_ref[...] = _ep_group_norm(C_DQK, acc, aux_ref.at[:, :C_Q_W]).astype(ocq_ref.dtype)
    gl = slice(C_Q_W, C_Q_W + LANES)
    oif_ref[...] = jnp.dot(h, w_ref[:, gl], preferred_element_type=F32) + aux_ref[0:1, gl]


def _proj_cq_if(h, w, aux, window, tm):
    n, d = h.shape
    tm = min(tm, n)
    assert window[1] == CQ_IF_W
    return pl.pallas_call(
        _cq_if_kernel,
        grid=(n // tm,),
        in_specs=[pl.BlockSpec((tm, d), lambda i: (i, 0))] + _window_specs(d, window, CQ_IF_W),
        out_specs=[pl.BlockSpec((tm, C_Q_W), lambda i: (i, 0)), pl.BlockSpec((tm, LANES), lambda i: (i, 0))],
        out_shape=[jax.ShapeDtypeStruct((n, C_Q_W), BF16), jax.ShapeDtypeStruct((n, LANES), F32)],
        compiler_params=_cparams(("parallel",)),
        name="proj_cq_if",
    )(h, w, aux)


def _aux_rows(row):
    return jnp.zeros((8, row.shape[0]), F32).at[0].set(row.astype(F32))


def _mlstm_kernel(qk_ref, v_ref, gate_ref, so_ref, cw_ref, cb_ref, ng_ref, o_ref,
                  xprev_ref, c_ref, n_ref, m_ref):
    L = CHUNK
    dk, dv, nh = M_DQK, M_DV, M_HEADS
    nb = qk_ref.shape[0]

    @pl.when(pl.program_id(1) == 0)
    def _():
        xprev_ref[...] = jnp.zeros_like(xprev_ref)
        c_ref[...] = jnp.zeros_like(c_ref)
        n_ref[...] = jnp.zeros_like(n_ref)
        m_ref[...] = jnp.zeros_like(m_ref)

    n_sh = CONV_K - 1
    srow = lax.broadcasted_iota(jnp.int32, (n_sh * L, 2 * L), 0)
    scol = lax.broadcasted_iota(jnp.int32, (n_sh * L, 2 * L), 1)
    sel = (scol == (srow % L) + L - 1 - srow // L).astype(BF16)
    rr = lax.broadcasted_iota(jnp.int32, (L, L), 0)
    cc = lax.broadcasted_iota(jnp.int32, (L, L), 1)
    causal = cc <= rr
    tri = causal.astype(BF16)

    act, g, g_t, bcum, bcum_t = [], [], [], [], []
    for bb in range(nb):
        xb = qk_ref[bb]
        xx = jnp.concatenate([xprev_ref[bb], xb], axis=0)
        shifted = jnp.dot(sel, xx, preferred_element_type=F32)
        xprev_ref[bb] = xb
        z = cb_ref[0:1, :] + cw_ref[CONV_K - 1:CONV_K, :] * xb.astype(F32)
        for s in range(n_sh):
            z = z + cw_ref[CONV_K - 2 - s:CONV_K - 1 - s, :] * shifted[s * L:(s + 1) * L, :]
        act.append(z + z * jnp.tanh(z))

        gg = gate_ref[bb]
        lf = jnp.minimum(gg, 0.0) - jnp.log1p(jnp.exp(-jnp.abs(gg)))
        lf_hi = lf.astype(BF16)
        lf_lo = (lf - lf_hi.astype(F32)).astype(BF16)
        bc = (jnp.dot(tri, lf_hi, preferred_element_type=F32)
              + jnp.dot(tri, lf_lo, preferred_element_type=F32))
        g.append(gg)
        g_t.append(gg.T)
        bcum.append(bc)
        bcum_t.append(bc.T)

    streams = [(bb, h) for bb in range(nb) for h in range(nh)]
    heads = range(len(streams))
    pick_r = lax.broadcasted_iota(jnp.int32, (2 * LANES, LANES), 0) % LANES

    def hi_lo(a):
        a_hi = a.astype(BF16)
        return jnp.concatenate([a_hi, (a - a_hi.astype(F32)).astype(BF16)], axis=1)

    def lane_column(a_hi_lo, lane):
        return jnp.dot(a_hi_lo, (pick_r == lane).astype(BF16), preferred_element_type=F32)

    wide = lambda col: jnp.concatenate([col, col], axis=1)
    bcum_hl = [hi_lo(a) for a in bcum]
    g_hl = [hi_lo(a) for a in g]
    bcol = [lane_column(bcum_hl[bb], nh + h) for bb, h in streams]
    brow = [bcum_t[bb][nh + h:nh + h + 1, :] for bb, h in streams]
    irow = [g_t[bb][h:h + 1, :] for bb, h in streams]
    icol = [lane_column(g_hl[bb], h) for bb, h in streams]
    m_old = [m_ref[i] for i in heads]
    c_old = [c_ref[i] for i in heads]
    n_old = [n_ref[i] for i in heads]
    qh = [act[bb][:, 2 * dk * h:2 * dk * h + dk] * (dk ** -0.5) for bb, h in streams]
    kh = [act[bb][:, 2 * dk * h + dk:2 * dk * (h + 1)] for bb, h in streams]
    vh = [v_ref[bb, :, h * dv:(h + 1) * dv] for bb, h in streams]
    qb = [q.astype(BF16) for q in qh]
    kb = [k.astype(BF16) for k in kh]

    qk_t = [lax.dot_general(qb[h], kb[h], (((1,), (1,)), ((), ())), preferred_element_type=F32) for h in heads]
    q_c = [jnp.dot(qb[h], c_old[h].astype(BF16), preferred_element_type=F32) for h in heads]
    logd = [jnp.where(causal, bcol[h] - brow[h] + irow[h], -jnp.inf) for h in heads]
    inter = [bcol[h] + m_old[h] for h in heads]
    m_loc = [jnp.maximum(inter[h], jnp.max(logd[h], axis=-1, keepdims=True)) for h in heads]
    w_inter = [jnp.exp(inter[h] - m_loc[h]) for h in heads]
    s = [qk_t[h] * jnp.exp(logd[h] - m_loc[h]) for h in heads]
    num = [jnp.dot(s[h].astype(BF16), vh[h], preferred_element_type=F32) + wide(w_inter[h]) * q_c[h] for h in heads]
    den = [jnp.sum(s[h], axis=-1, keepdims=True) + w_inter[h] * jnp.sum(qh[h] * n_old[h], axis=-1, keepdims=True)
           for h in heads]

    b_end = [bcol[h][L - 1:L, :] for h in heads]
    log_w = [b_end[h] - bcol[h] + icol[h] for h in heads]
    m_new = [jnp.maximum(b_end[h] + m_old[h], jnp.max(log_w[h], axis=0, keepdims=True)) for h in heads]
    kw = [kh[h] * jnp.exp(log_w[h] - m_new[h]) for h in heads]
    decay = [jnp.exp(b_end[h] + m_old[h] - m_new[h]) for h in heads]
    for h in heads:
        c_ref[h] = wide(decay[h]) * c_old[h] + lax.dot_general(
            kw[h].astype(BF16), vh[h], (((0,), (0,)), ((), ())), preferred_element_type=F32)
        n_ref[h] = decay[h] * n_old[h] + jnp.sum(kw[h], axis=0, keepdims=True)
        m_ref[h] = m_new[h][:, 0:1]

    for i, (bb, h) in enumerate(streams):
        d = jnp.maximum(jnp.abs(den[i]), jnp.exp(-m_loc[i]))
        row_scale = lax.rsqrt(jnp.mean(num[i] * num[i], axis=-1, keepdims=True) + EPS * (d * d))
        yn = num[i] * wide(row_scale) * ng_ref[0:1, h * dv:(h + 1) * dv]
        o_ref[bb, :, h * dv:(h + 1) * dv] = so_ref[bb, :, h * dv:(h + 1) * dv] * yn.astype(o_ref.dtype)


def _mlstm(qk, qk_col, vproj, v_col, gates, so, so_col, conv_w, conv_b, norm_g, bsz, seq, nb):
    nc = seq // CHUNK
    L = CHUNK
    assert bsz % nb == 0
    by_batch = lambda a: a.reshape(bsz, seq, a.shape[-1])
    out = pl.pallas_call(
        _mlstm_kernel,
        grid=(bsz // nb, nc),
        in_specs=[
            pl.BlockSpec((nb, L, 2 * M_QK_W), lambda b, c: (b, c, qk_col)),
            pl.BlockSpec((nb, L, M_V_W), lambda b, c: (b, c, v_col)),
            pl.BlockSpec((nb, L, LANES), lambda b, c: (b, c, 0)),
            pl.BlockSpec((nb, L, M_V_W), lambda b, c: (b, c, so_col)),
            pl.BlockSpec((CONV_K, 2 * M_QK_W), lambda b, c: (0, 0)),
            pl.BlockSpec((8, 2 * M_QK_W), lambda b, c: (0, 0)),
            pl.BlockSpec((8, M_V_W), lambda b, c: (0, 0)),
        ],
        out_specs=pl.BlockSpec((nb, L, M_V_W), lambda b, c: (b, c, 0)),
        out_shape=jax.ShapeDtypeStruct((bsz, seq, M_V_W), BF16),
        scratch_shapes=[
            pltpu.VMEM((nb, L, 2 * M_QK_W), BF16),
            pltpu.VMEM((nb * M_HEADS, M_DQK, M_DV), F32),
            pltpu.VMEM((nb * M_HEADS, 1, M_DQK), F32),
            pltpu.VMEM((nb * M_HEADS, 1, 1), F32),
        ],
        compiler_params=_cparams(("parallel", "arbitrary")),
        name="mlstm",
    )(by_batch(qk), by_batch(vproj), by_batch(gates), by_batch(so), conv_w.astype(F32), _aux_rows(conv_b),
      _aux_rows(norm_g))
    return out.reshape(bsz * seq, M_V_W)


def _dattn_load_tile(q_ref, v_ref, vt_ref, seq):
    tv = 256

    @pl.when(pl.program_id(2) == 0)
    def _():
        for c in range(seq // tv):
            blk = v_ref[c * tv:(c + 1) * tv, :].astype(F32)
            vt_ref[:, c * tv:(c + 1) * tv] = blk.T.astype(BF16)

    qt = q_ref[...].astype(F32).T
    row = lax.broadcasted_iota(jnp.int32, qt.shape, 0)
    return (jnp.where(row < D_HD, qt, 0.0).astype(BF16), jnp.where(row >= D_HD, qt, 0.0).astype(BF16))


def _dattn_finish(lam_ref, sg_ref, o_ref, acc_ref, l1, l2, out_scale):
    ot = acc_ref[0] * (1.0 / l1) - acc_ref[1] * (lam_ref[0] / l2)
    yt = ot * lax.rsqrt(jnp.mean(ot * ot, axis=0, keepdims=True) + EPS)
    o_ref[...] = (yt.T * (sg_ref[0:1, :] * out_scale)).astype(o_ref.dtype)


def _causal_mask(s):
    kr = lax.broadcasted_iota(jnp.int32, s.shape, 0)
    qc = lax.broadcasted_iota(jnp.int32, s.shape, 1)
    return jnp.where(kr <= qc, s, NEG_BIG)


def _dattn_kernel(lam_ref, q_ref, k_ref, v_ref, sg_ref, o_ref, vt_ref, acc_ref, qz_ref, l_ref, p_ref, *,
                  seq, tq, tk, out_scale):
    n_diag = tq // tk
    assert n_diag * tk == tq and n_diag % 2 == 0
    qi = pl.program_id(2)
    qz = _dattn_load_tile(q_ref, v_ref, vt_ref, seq)
    qz_ref[0] = qz[0]
    qz_ref[1] = qz[1]
    acc_ref[...] = jnp.zeros_like(acc_ref)
    l_ref[...] = jnp.zeros_like(l_ref)

    half = tk // 2

    def block(kj, c, slot, k0, k1, c0, c1, masked):
        s = jnp.dot(kj[k0:k1], qz_ref[c, :, c0:c1], preferred_element_type=F32)
        if masked:
            s = _causal_mask(s)
        p = jnp.exp2(s)
        l_ref[c, :, c0:c1] += jnp.sum(p, axis=0, keepdims=True)
        p_ref[slot, c, k0:k1, c0:c1] = p.astype(BF16)

    def scores(j, lo, slot):
        kj = k_ref[pl.ds(pl.multiple_of(j * tk, tk), tk), :]
        for c in range(2):
            if lo is None:
                block(kj, c, slot, 0, tk, 0, tq, False)
            else:
                block(kj, c, slot, 0, half, lo, lo + tk, True)
                block(kj, c, slot, half, tk, lo + half, lo + tk, True)
                if lo + tk < tq:
                    block(kj, c, slot, 0, tk, lo + tk, tq, False)

    def accumulate(j, lo, slot):
        vtj = vt_ref[:, pl.ds(pl.multiple_of(j * tk, tk), tk)]
        for c in range(2):
            if lo is None:
                acc_ref[c] += jnp.dot(vtj, p_ref[slot, c], preferred_element_type=F32)
            else:
                acc_ref[c, :, lo:lo + half] += jnp.dot(vtj[:, :half], p_ref[slot, c, :half, lo:lo + half],
                                                       preferred_element_type=F32)
                acc_ref[c, :, lo + half:] += jnp.dot(vtj, p_ref[slot, c, :, lo + half:],
                                                     preferred_element_type=F32)

    n = n_diag * qi

    @pl.when(qi > 0)
    def _():
        scores(0, None, 0)

    def pair(i, carry):
        j = 2 * i
        accumulate(j, None, 0)
        scores(j + 1, None, 1)
        accumulate(j + 1, None, 1)
        scores(j + 2, None, 0)
        return carry

    lax.fori_loop(0, n // 2 - 1, pair, 0)

    @pl.when(qi > 0)
    def _():
        accumulate(n - 2, None, 0)
        scores(n - 1, None, 1)
        accumulate(n - 1, None, 1)
        scores(n, 0, 0)

    @pl.when(qi == 0)
    def _():
        scores(n, 0, 0)

    for d in range(n_diag):
        accumulate(n + d, d * tk, d % 2)
        if d + 1 < n_diag:
            scores(n + d + 1, (d + 1) * tk, (d + 1) % 2)
    _dattn_finish(lam_ref, sg_ref, o_ref, acc_ref, l_ref[0], l_ref[1], out_scale)


def _dattn_stabilised_kernel(lam_ref, q_ref, k_ref, v_ref, sg_ref, o_ref, vt_ref, acc_ref, *,
                             seq, tq, tk, out_scale):
    qi = pl.program_id(2)
    qz = _dattn_load_tile(q_ref, v_ref, vt_ref, seq)
    acc_ref[...] = jnp.zeros_like(acc_ref)

    def chunk(j, carry, lo):
        start = pl.multiple_of(j * tk, tk)
        kj = k_ref[pl.ds(start, tk), :]
        vtj = vt_ref[:, pl.ds(start, tk)]
        q0 = lo or 0
        out = []
        for c in range(2):
            m, l = carry[2 * c][:, q0:], carry[2 * c + 1][:, q0:]
            s = jnp.dot(kj, qz[c][:, q0:], preferred_element_type=F32)
            if lo is not None:
                s = _causal_mask(s)
            m_new = jnp.maximum(m, jnp.max(s, axis=0, keepdims=True))
            p = jnp.exp2(s - m_new)
            alpha = jnp.exp2(m - m_new)
            l_new = alpha * l + jnp.sum(p, axis=0, keepdims=True)
            acc_ref[c, :, q0:] = alpha * acc_ref[c, :, q0:] + jnp.dot(
                vtj, p.astype(BF16), preferred_element_type=F32)
            if q0:
                m_new = jnp.concatenate([carry[2 * c][:, :q0], m_new], axis=1)
                l_new = jnp.concatenate([carry[2 * c + 1][:, :q0], l_new], axis=1)
            out += [m_new, l_new]
        return tuple(out)

    init = (jnp.full((1, tq), NEG_BIG, F32), jnp.zeros((1, tq), F32)) * 2
    n_full = qi * (tq // tk)
    carry = lax.fori_loop(0, n_full, lambda j, c: chunk(j, c, None), init)
    for d in range(tq // tk):
        carry = chunk(n_full + d, carry, d * tk)

    _dattn_finish(lam_ref, sg_ref, o_ref, acc_ref, carry[1], carry[3], out_scale)


def _diff_attention(lam, qk, v, v_col0, subln_g, bsz, seq, out_scale, stabilised, tq, tk):
    n = bsz * seq
    tq = min(tq, seq)
    tk = min(tk, tq)
    nq = seq // tq
    hw = 2 * D_HD
    scratch = [pltpu.VMEM((D_DV, seq), BF16), pltpu.VMEM((2, D_DV, tq), F32)]
    if stabilised:
        body = _dattn_stabilised_kernel
    else:
        body = _dattn_kernel
        scratch += [pltpu.VMEM((2, hw, tq), BF16), pltpu.VMEM((2, 1, tq), F32), pltpu.VMEM((2, 2, tk, tq), BF16)]
    return pl.pallas_call(
        functools.partial(body, seq=seq, tq=tq, tk=tk, out_scale=out_scale),
        grid=(bsz, D_HEADS, nq),
        in_specs=[
            pl.BlockSpec(memory_space=pltpu.SMEM),
            pl.BlockSpec((tq, hw), lambda b, h, i: (b * nq + i, h)),
            pl.BlockSpec((seq, hw), lambda b, h, i: (b, D_HEADS + h)),
            pl.BlockSpec((seq, D_DV), lambda b, h, i: (b, v_col0 + h)),
            pl.BlockSpec((8, D_DV), lambda b, h, i: (0, 0)),
        ],
        out_specs=pl.BlockSpec((tq, D_DV), lambda b, h, i: (b * nq + i, h)),
        out_shape=jax.ShapeDtypeStruct((n, D_V_W), BF16),
        scratch_shapes=scratch,
        compiler_params=_cparams(("parallel", "parallel", "arbitrary")),
        name="diff_attn_stabilised" if stabilised else "diff_attn",
    )(lam.reshape(1).astype(F32), qk, qk, v, _aux_rows(subln_g))


def _xattn_kernel(q_ref, mk_ref, mv_ref, o_ref):
    for h in range(C_HEADS):
        q = q_ref[:, h * C_DQK:(h + 1) * C_DQK]
        k = mk_ref[:, h * C_DQK:(h + 1) * C_DQK]
        s = lax.dot_general(q, k, (((1,), (1,)), ((), ())), preferred_element_type=F32)
        m = jnp.max(s, axis=-1, keepdims=True)
        p = jnp.exp2(s - m)
        l = jnp.sum(p, axis=-1, keepdims=True)
        o = jnp.dot(p.astype(BF16), mv_ref[:, h * C_DV:(h + 1) * C_DV], preferred_element_type=F32)
        o_ref[:, h * C_DV:(h + 1) * C_DV] = (o / l).astype(o_ref.dtype)


def _cross_attention(q, mk, mv, bsz, seq, mem_len, tq=2048):
    n = bsz * seq
    tq = min(tq, seq)
    nq = seq // tq
    return pl.pallas_call(
        _xattn_kernel,
        grid=(bsz, nq),
        in_specs=[
            pl.BlockSpec((tq, C_Q_W), lambda b, i: (b * nq + i, 0)),
            pl.BlockSpec((mem_len, C_Q_W), lambda b, i: (b, 0)),
            pl.BlockSpec((mem_len, C_V_W), lambda b, i: (b, 0)),
        ],
        out_specs=pl.BlockSpec((tq, C_V_W), lambda b, i: (b * nq + i, 0)),
        out_shape=jax.ShapeDtypeStruct((n, C_V_W), BF16),
        compiler_params=_cparams(("parallel", "parallel")),
        name="cross_attn",
    )(q, mk, mv)


def _merge_kernel(x_ref, ym_ref, yd_ref, yc_ref, gm_ref, gd_ref, gc_ref,
                  wm_ref, wd_ref, wc_ref, wo_ref, g_ref, o_ref, h_ref):
    merged = gm_ref[...].astype(F32) * jnp.dot(ym_ref[...], wm_ref[...], preferred_element_type=F32)
    merged = merged + gd_ref[...].astype(F32) * jnp.dot(yd_ref[...], wd_ref[...], preferred_element_type=F32)
    merged = merged + gc_ref[...].astype(F32) * jnp.dot(yc_ref[...], wc_ref[...], preferred_element_type=F32)
    x1 = x_ref[...] + jnp.dot(merged.astype(BF16), wo_ref[...], preferred_element_type=F32)
    o_ref[...] = x1
    y = x1 * lax.rsqrt(jnp.mean(x1 * x1, axis=-1, keepdims=True) + EPS)
    h_ref[...] = (y * g_ref[...]).astype(BF16)


def _merge(x, ym, yd, yc, gates, g_col0, wm, wd, wc, wo, g_next, tm):
    n, d = x.shape
    tm = min(tm, n)
    row = lambda i: (i, 0)
    wspec = pl.BlockSpec((d, d), lambda i: (0, 0), pipeline_mode=pl.Buffered(1))
    return pl.pallas_call(
        _merge_kernel,
        grid=(n // tm,),
        in_specs=[
            pl.BlockSpec((tm, d), row),
            pl.BlockSpec((tm, d), row),
            pl.BlockSpec((tm, d), row),
            pl.BlockSpec((tm, d), row),
            pl.BlockSpec((tm, d), lambda i: (i, g_col0)),
            pl.BlockSpec((tm, d), lambda i: (i, g_col0 + 1)),
            pl.BlockSpec((tm, d), lambda i: (i, g_col0 + 2)),
            wspec, wspec, wspec, wspec,
            pl.BlockSpec((1, d), lambda i: (0, 0)),
        ],
        out_specs=[pl.BlockSpec((tm, d), row), pl.BlockSpec((tm, d), row)],
        out_shape=[jax.ShapeDtypeStruct((n, d), F32), jax.ShapeDtypeStruct((n, d), BF16)],
        compiler_params=_cparams(("parallel",)),
        name="merge",
    )(x, ym, yd, yc, gates, gates, gates, wm, wd, wc, wo, g_next.reshape(1, d).astype(F32))


def _mlp_kernel(x_ref, h_ref, wu_ref, wd_ref, o_ref, *, tf):
    h = h_ref[...]
    acc = x_ref[...]
    for f in range(wu_ref.shape[1] // tf):
        u = jnp.maximum(jnp.dot(h, wu_ref[:, f * tf:(f + 1) * tf], preferred_element_type=F32), 0.0)
        acc = acc + jnp.dot((u * u).astype(BF16), wd_ref[f * tf:(f + 1) * tf, :], preferred_element_type=F32)
    o_ref[...] = acc


def _mlp(x, h, wu, wd, tm, tf):
    n, d = x.shape
    dff = wu.shape[1]
    tm = min(tm, n)
    return pl.pallas_call(
        functools.partial(_mlp_kernel, tf=tf),
        grid=(n // tm,),
        in_specs=[
            pl.BlockSpec((tm, d), lambda i: (i, 0)),
            pl.BlockSpec((tm, d), lambda i: (i, 0)),
            pl.BlockSpec((d, dff), lambda i: (0, 0), pipeline_mode=pl.Buffered(1)),
            pl.BlockSpec((dff, d), lambda i: (0, 0), pipeline_mode=pl.Buffered(1)),
        ],
        out_specs=pl.BlockSpec((tm, d), lambda i: (i, 0)),
        out_shape=jax.ShapeDtypeStruct((n, d), F32),
        compiler_params=_cparams(("parallel",)),
        name="mlp",
    )(x, h, wu, wd)


def _lambda_init(layer):
    return 0.8 - 0.6 * math.exp(-0.3 * layer)


def _layer(l, x2, mem2, bsz, seq, mem_len, p):
    split = [0]
    for w in (M_QK_W, M_QK_W, M_V_W, M_HEADS, M_HEADS, M_V_W, D_Q_W, D_Q_W, D_V_W, C_Q_W):
        split.append(split[-1] + w)
    w_in = p['w_in'][l]
    g_mix = p['norm_mix_g'][l]

    def head_major(a):
        lead = a.shape[:-1]
        return a.reshape(lead + (2, M_HEADS, M_DQK)).swapaxes(-3, -2).reshape(lead + (2 * M_QK_W,))

    conv_w_half = 0.5 * head_major(p['conv_w'][l].astype(F32))
    conv_b_half = 0.5 * head_major(p['conv_b'][l].astype(F32))

    q_gain = jnp.tile(p['dq_norm_g'][l].astype(F32), D_Q_W // D_HD) * (D_HD ** -0.5 * LOG2E)
    k_gain = jnp.tile(p['dk_norm_g'][l].astype(F32), D_Q_W // D_HD)
    cq_gain = jnp.tile(p['cq_norm_g'][l].astype(F32), C_HEADS) * (C_DQK ** -0.5 * LOG2E)
    if_pad = CQ_IF_W - C_Q_W - 2 * M_HEADS
    w_all = _pack_w_in(w_in, p['w_gate'][l], tuple(split))
    aux_all = _aux_rows(jnp.concatenate(
        [jnp.zeros((M_V_W,), F32), 0.5 * p['b_gate'][l].astype(F32), q_gain, k_gain,
         jnp.zeros((M_V_W + D_V_W + 2 * M_QK_W,), F32), cq_gain, p['b_igate'][l].astype(F32),
         p['b_fgate'][l].astype(F32), jnp.zeros((if_pad,), F32)]))
    win = {}
    c0 = 0
    for name, width in (("gates", M_V_W + N_BRANCH * D_MODEL), ("dqk", 2 * D_Q_W),
                        ("vqk", M_V_W + D_V_W + 2 * M_QK_W), ("cq_if", CQ_IF_W)):
        win[name] = (c0, width)
        c0 += width

    vqk, h = _norm_matmul(x2, g_mix, w_all, aux_all, win["vqk"], _ep_identity, BF16, 1024, PROJ_TN, "proj_vqk")
    sgates = _matmul(h, w_all, aux_all, win["gates"], _ep_sigmoid_of_double, BF16, PROJ_WIDE_TM, PROJ_TN,
                     "proj_gates")
    dqk = _matmul(h, w_all, aux_all, win["dqk"], functools.partial(_ep_group_norm, D_HD), BF16,
                  PROJ_NORM_TM, PROJ_TN, "proj_dqk")
    cq, gates_if = _proj_cq_if(h, w_all, aux_all, win["cq_if"], PROJ_TM)
    w_kv = p['w_mem_kv'][l]
    w_kv = jnp.concatenate([w_kv[:, C_Q_W:], w_kv[:, :C_Q_W]], axis=1).astype(BF16)
    aux_kv = _aux_rows(jnp.concatenate([jnp.zeros((C_V_W,), F32), jnp.tile(p['ck_norm_g'][l].astype(F32), C_HEADS)]))
    mk, hmem = _norm_matmul(mem2, p['mem_norm_g'][l], w_kv, aux_kv, (C_V_W, C_Q_W),
                            functools.partial(_ep_group_norm, C_DQK), BF16, 1024, C_Q_W, "proj_mk")
    mv = _matmul(hmem, w_kv, aux_kv, (0, C_V_W), _ep_identity, BF16, 1024, C_Q_W, "proj_mv")

    y_m = _mlstm(vqk, (M_V_W + D_V_W) // (2 * M_QK_W), vqk, 0, gates_if, sgates, 0, conv_w_half, conv_b_half,
                 p['m_norm_g'][l], bsz, seq, MLSTM_NB if bsz % MLSTM_NB == 0 else 1)

    lam_i = _lambda_init(l)
    lam = (jnp.exp(jnp.sum(p['lam_q1'][l].astype(F32) * p['lam_k1'][l].astype(F32)))
           - jnp.exp(jnp.sum(p['lam_q2'][l].astype(F32) * p['lam_k2'][l].astype(F32))) + lam_i)
    score_bound = 1.02 * D_HD * jnp.max(jnp.abs(q_gain)) * jnp.max(jnp.abs(k_gain))
    dattn = functools.partial(_diff_attention, lam, dqk, vqk, M_V_W // D_DV, p['subln_g'][l], bsz, seq,
                              1.0 - lam_i)
    y_d = lax.cond(score_bound <= MAX_UNSTABILISED_SCORE,
                   lambda: dattn(False, DATTN_TQ_FAST, DATTN_TK_FAST),
                   lambda: dattn(True, DATTN_TQ_STABILISED, DATTN_TK_STABILISED))

    y_c = _cross_attention(cq, mk, mv, bsz, seq, mem_len)

    x2, h2 = _merge(x2, y_m, y_d, y_c, sgates, 1,
                    p['w_proj_m'][l].astype(BF16), p['w_proj_d'][l].astype(BF16),
                    p['w_proj_c'][l].astype(BF16), p['w_out'][l].astype(BF16), p['norm_mlp_g'][l], MERGE_TM)
    return _mlp(x2, h2, p['w_up'][l].astype(BF16), p['w_down'][l].astype(BF16), MLP_TM, MLP_TF)


def kernel(x, mem, norm_mix_g, w_in, b_igate, b_fgate, conv_w, conv_b, m_norm_g, dq_norm_g, dk_norm_g, lam_q1, lam_k1, lam_q2, lam_k2, subln_g, cq_norm_g, ck_norm_g, mem_norm_g, w_mem_kv, w_gate, b_gate, w_proj_m, w_proj_d, w_proj_c, w_out, norm_mlp_g, w_up, w_down):
    p = dict(norm_mix_g=norm_mix_g, w_in=w_in, b_igate=b_igate, b_fgate=b_fgate, conv_w=conv_w, conv_b=conv_b,
             m_norm_g=m_norm_g, dq_norm_g=dq_norm_g, dk_norm_g=dk_norm_g, lam_q1=lam_q1, lam_k1=lam_k1,
             lam_q2=lam_q2, lam_k2=lam_k2, subln_g=subln_g, cq_norm_g=cq_norm_g, ck_norm_g=ck_norm_g,
             mem_norm_g=mem_norm_g, w_mem_kv=w_mem_kv, w_gate=w_gate, b_gate=b_gate, w_proj_m=w_proj_m,
             w_proj_d=w_proj_d, w_proj_c=w_proj_c, w_out=w_out, norm_mlp_g=norm_mlp_g, w_up=w_up, w_down=w_down)
    bsz, seq, d = x.shape
    mem_len = mem.shape[1]
    x2 = x.reshape(bsz * seq, d)
    mem2 = mem.reshape(bsz * mem_len, d)
    for l in range(w_in.shape[0]):
        x2 = _layer(l, x2, mem2, bsz, seq, mem_len, p)
    return x2.reshape(bsz, seq, d)
```

```python
import functools
import math

import jax
import jax.numpy as jnp
from jax import lax
from jax.experimental import pallas as pl
from jax.experimental.pallas import tpu as pltpu

F32 = jnp.float32
BF16 = jnp.bfloat16

EPS = 1e-6
LOG2E = math.log2(math.e)
NEG_BIG = -1e30

D_MODEL = 1024
MEM_LEN = 256
M_HEADS, M_DQK, M_DV = 4, 128, 256
CHUNK = 128
CONV_K = 4
D_HEADS, D_HD = 8, 64
D_DV = 2 * D_HD
C_HEADS, C_DQK, C_DV = 4, 128, 256
D_FF = 4 * D_MODEL
N_BRANCH = 3
LANES = 128
CQ_IF_W = 1024
MXU_DIM = 256

M_QK_W = M_HEADS * M_DQK
M_V_W = M_HEADS * M_DV
D_Q_W = D_HEADS * 2 * D_HD
D_V_W = D_HEADS * D_DV
C_Q_W = C_HEADS * C_DQK
C_V_W = C_HEADS * C_DV

VMEM_LIMIT = 56 * 1024 * 1024

PROJ_TM = 2048
PROJ_WIDE_TM = 1024
PROJ_NORM_TM = 1024
PROJ_TN = 1024
MERGE_TM = 512
MERGE_ROW_GROUPS = 2
MLP_TM = 1024
MLP_TF = 1024
MLSTM_NB = 4
DATTN_TQ_FAST = 4096
DATTN_TQ_STABILISED = 1024
DATTN_TK_FAST = 512
DATTN_TK_STABILISED = 256
MAX_UNSTABILISED_SCORE = 60.0


def _cparams(sem):
    return pltpu.CompilerParams(dimension_semantics=sem, vmem_limit_bytes=VMEM_LIMIT)


def _ep_identity(acc, aux_ref):
    return acc


def _ep_sigmoid_of_double(acc, aux_ref):
    return 0.5 * jnp.tanh(acc + aux_ref[0:1, :]) + 0.5


def _ep_bias(acc, aux_ref):
    return acc + aux_ref[0:1, :]


def _ep_group_norm(group, acc, aux_ref):
    tn = acc.shape[1]
    cw = MXU_DIM
    r = lax.broadcasted_iota(jnp.int32, (cw, cw), 0) // group
    c = lax.broadcasted_iota(jnp.int32, (cw, cw), 1) // group
    gmat = (r == c).astype(BF16)
    outs = []
    for s in range(tn // cw):
        a = acc[:, s * cw:(s + 1) * cw]
        ss = jnp.dot((a * a).astype(BF16), gmat, preferred_element_type=F32)
        outs.append(a * lax.rsqrt(ss * (1.0 / group) + EPS))
    y = jnp.concatenate(outs, axis=1) if len(outs) > 1 else outs[0]
    return y * aux_ref[0:1, :]


def _wdot(h, w_ref, cs, wt):
    if wt:
        return lax.dot_general(h, w_ref[cs, :], (((1,), (1,)), ((), ())), preferred_element_type=F32)
    return jnp.dot(h, w_ref[:, cs], preferred_element_type=F32)


def _column_tiles(h, w_ref, aux_ref, o_ref, epilogue, tn, wt):
    for j in range(o_ref.shape[1] // tn):
        cs = slice(j * tn, (j + 1) * tn)
        o_ref[:, cs] = epilogue(_wdot(h, w_ref, cs, wt), aux_ref.at[:, cs]).astype(o_ref.dtype)


def _norm_matmul_kernel(x_ref, g_ref, w_ref, aux_ref, o_ref, h_ref, *, epilogue, tn, wt):
    x = x_ref[...].astype(F32)
    y = x * lax.rsqrt(jnp.mean(x * x, axis=-1, keepdims=True) + EPS)
    h = (y * g_ref[...]).astype(BF16)
    h_ref[...] = h
    _column_tiles(h, w_ref, aux_ref, o_ref, epilogue, tn, wt)


def _matmul_kernel(h_ref, w_ref, aux_ref, o_ref, *, epilogue, tn, wt):
    _column_tiles(h_ref[...], w_ref, aux_ref, o_ref, epilogue, tn, wt)


def _window_specs(d, window, tn, wt):
    c0, cols = window
    assert cols % tn == 0 and c0 % cols == 0
    jw = c0 // cols
    w_spec = (pl.BlockSpec((cols, d), lambda i: (jw, 0), pipeline_mode=pl.Buffered(1)) if wt else
              pl.BlockSpec((d, cols), lambda i: (0, jw), pipeline_mode=pl.Buffered(1)))
    return [w_spec, pl.BlockSpec((8, cols), lambda i: (0, jw), pipeline_mode=pl.Buffered(1))]


def _norm_matmul(x, g, w, aux, window, epilogue, out_dtype, tm, tn, name, wt=False):
    n, d = x.shape
    cols = window[1]
    tm = min(tm, n)
    tn = min(tn, cols)
    assert n % tm == 0
    return pl.pallas_call(
        functools.partial(_norm_matmul_kernel, epilogue=epilogue, tn=tn, wt=wt),
        grid=(n // tm,),
        in_specs=[pl.BlockSpec((tm, d), lambda i: (i, 0)), pl.BlockSpec((1, d), lambda i: (0, 0))]
        + _window_specs(d, window, tn, wt),
        out_specs=[pl.BlockSpec((tm, cols), lambda i: (i, 0)), pl.BlockSpec((tm, d), lambda i: (i, 0))],
        out_shape=[jax.ShapeDtypeStruct((n, cols), out_dtype), jax.ShapeDtypeStruct((n, d), BF16)],
        compiler_params=_cparams(("parallel",)),
        name=name,
    )(x, g.reshape(1, d).astype(F32), w, aux)


def _matmul(h, w, aux, window, epilogue, out_dtype, tm, tn, name, wt=False):
    n, d = h.shape
    cols = window[1]
    tm = min(tm, n)
    tn = min(tn, cols)
    assert n % tm == 0
    return pl.pallas_call(
        functools.partial(_matmul_kernel, epilogue=epilogue, tn=tn, wt=wt),
        grid=(n // tm,),
        in_specs=[pl.BlockSpec((tm, d), lambda i: (i, 0))] + _window_specs(d, window, tn, wt),
        out_specs=pl.BlockSpec((tm, cols), lambda i: (i, 0)),
        out_shape=jax.ShapeDtypeStruct((n, cols), out_dtype),
        compiler_params=_cparams(("parallel",)),
        name=name,
    )(h, w, aux)


def _pack_w_in_kernel(wt_ref, wg_ref, o_ref, *, split):
    wt = wt_ref[...]
    row = lambda a, b: wt[split[a]:split[b], :]
    parts = [0.5 * row(5, 6), 0.5 * wg_ref[...].T, row(6, 8), row(2, 3), row(8, 9)]
    for h in range(M_HEADS):
        parts += [row(0, 1)[h * M_DQK:(h + 1) * M_DQK, :], row(1, 2)[h * M_DQK:(h + 1) * M_DQK, :]]
    parts += [row(9, 10), row(3, 5), jnp.zeros((CQ_IF_W - C_Q_W - 2 * M_HEADS, wt.shape[1]), wt.dtype)]
    o_ref[...] = jnp.concatenate(parts, axis=0).astype(BF16)


def _pack_w_in(w_in_t, w_gate, split, tc=256):
    cin, d = w_in_t.shape
    cout = cin - C_Q_W - 2 * M_HEADS + CQ_IF_W + w_gate.shape[1]
    return pl.pallas_call(
        functools.partial(_pack_w_in_kernel, split=split),
        grid=(d // tc,),
        in_specs=[pl.BlockSpec((cin, tc), lambda i: (0, i)),
                  pl.BlockSpec((tc, w_gate.shape[1]), lambda i: (i, 0))],
        out_specs=pl.BlockSpec((cout, tc), lambda i: (0, i)),
        out_shape=jax.ShapeDtypeStruct((cout, d), BF16),
        compiler_params=_cparams(("parallel",)),
        name="pack_w_in",
    )(w_in_t, w_gate)


def _cq_if_kernel(h_ref, w_ref, aux_ref, ocq_ref, oif_ref):
    h = h_ref[...]
    acc = _wdot(h, w_ref, slice(0, C_Q_W), True)
    ocq_ref[...] = _ep_group_norm(C_DQK, acc, aux_ref.at[:, :C_Q_W]).astype(ocq_ref.dtype)
    gl = slice(C_Q_W, C_Q_W + LANES)
    oif_ref[...] = _wdot(h, w_ref, gl, True) + aux_ref[0:1, gl]


def _proj_cq_if(h, w, aux, window, tm):
    n, d = h.shape
    tm = min(tm, n)
    assert window[1] == CQ_IF_W
    return pl.pallas_call(
        _cq_if_kernel,
        grid=(n // tm,),
        in_specs=[pl.BlockSpec((tm, d), lambda i: (i, 0))] + _window_specs(d, window, CQ_IF_W, True),
        out_specs=[pl.BlockSpec((tm, C_Q_W), lambda i: (i, 0)), pl.BlockSpec((tm, LANES), lambda i: (i, 0))],
        out_shape=[jax.ShapeDtypeStruct((n, C_Q_W), BF16), jax.ShapeDtypeStruct((n, LANES), F32)],
        compiler_params=_cparams(("parallel",)),
        name="proj_cq_if",
    )(h, w, aux)


def _aux_rows(row):
    return jnp.zeros((8, row.shape[0]), F32).at[0].set(row.astype(F32))


def _mlstm_kernel(qk_ref, v_ref, gate_ref, so_ref, cw_ref, cb_ref, ng_ref, o_ref,
                  xprev_ref, c_ref, n_ref, m_ref):
    L = CHUNK
    dk, dv, nh = M_DQK, M_DV, M_HEADS
    nb = qk_ref.shape[0]

    @pl.when(pl.program_id(1) == 0)
    def _():
        xprev_ref[...] = jnp.zeros_like(xprev_ref)
        c_ref[...] = jnp.zeros_like(c_ref)
        n_ref[...] = jnp.zeros_like(n_ref)
        m_ref[...] = jnp.zeros_like(m_ref)

    n_sh = CONV_K - 1
    srow = lax.broadcasted_iota(jnp.int32, (n_sh * L, 2 * L), 0)
    scol = lax.broadcasted_iota(jnp.int32, (n_sh * L, 2 * L), 1)
    sel = (scol == (srow % L) + L - 1 - srow // L).astype(BF16)
    rr = lax.broadcasted_iota(jnp.int32, (L, L), 0)
    cc = lax.broadcasted_iota(jnp.int32, (L, L), 1)
    causal = cc <= rr
    tri = causal.astype(BF16)

    act, g, g_t, bcum, bcum_t = [], [], [], [], []
    for bb in range(nb):
        xb = qk_ref[bb]
        xx = jnp.concatenate([xprev_ref[bb], xb], axis=0)
        shifted = jnp.dot(sel, xx, preferred_element_type=F32)
        xprev_ref[bb] = xb
        z = cb_ref[0:1, :] + cw_ref[CONV_K - 1:CONV_K, :] * xb.astype(F32)
        for s in range(n_sh):
            z = z + cw_ref[CONV_K - 2 - s:CONV_K - 1 - s, :] * shifted[s * L:(s + 1) * L, :]
        act.append(z + z * jnp.tanh(z))

        gg = gate_ref[bb]
        lf = jnp.minimum(gg, 0.0) - jnp.log1p(jnp.exp(-jnp.abs(gg)))
        lf_hi = lf.astype(BF16)
        lf_lo = (lf - lf_hi.astype(F32)).astype(BF16)
        bc = (jnp.dot(tri, lf_hi, preferred_element_type=F32)
              + jnp.dot(tri, lf_lo, preferred_element_type=F32))
        g.append(gg)
        g_t.append(gg.T)
        bcum.append(bc)
        bcum_t.append(bc.T)

    streams = [(bb, h) for bb in range(nb) for h in range(nh)]
    heads = range(len(streams))
    pick_r = lax.broadcasted_iota(jnp.int32, (2 * LANES, LANES), 0) % LANES

    def hi_lo(a):
        a_hi = a.astype(BF16)
        return jnp.concatenate([a_hi, (a - a_hi.astype(F32)).astype(BF16)], axis=1)

    def lane_column(a_hi_lo, lane):
        return jnp.dot(a_hi_lo, (pick_r == lane).astype(BF16), preferred_element_type=F32)

    wide = lambda col: jnp.concatenate([col, col], axis=1)
    bcum_hl = [hi_lo(a) for a in bcum]
    g_hl = [hi_lo(a) for a in g]
    bcol = [lane_column(bcum_hl[bb], nh + h) for bb, h in streams]
    brow = [bcum_t[bb][nh + h:nh + h + 1, :] for bb, h in streams]
    irow = [g_t[bb][h:h + 1, :] for bb, h in streams]
    icol = [lane_column(g_hl[bb], h) for bb, h in streams]
    m_old = [m_ref[i] for i in heads]
    c_old = [c_ref[i] for i in heads]
    n_old = [n_ref[i] for i in heads]
    qh = [act[bb][:, 2 * dk * h:2 * dk * h + dk] * (dk ** -0.5) for bb, h in streams]
    kh = [act[bb][:, 2 * dk * h + dk:2 * dk * (h + 1)] for bb, h in streams]
    vh = [v_ref[bb, :, h * dv:(h + 1) * dv] for bb, h in streams]
    qb = [q.astype(BF16) for q in qh]
    kb = [k.astype(BF16) for k in kh]

    qk_t = [lax.dot_general(qb[h], kb[h], (((1,), (1,)), ((), ())), preferred_element_type=F32) for h in heads]
    q_c = [jnp.dot(qb[h], c_old[h].astype(BF16), preferred_element_type=F32) for h in heads]
    logd = [jnp.where(causal, bcol[h] - brow[h] + irow[h], -jnp.inf) for h in heads]
    inter = [bcol[h] + m_old[h] for h in heads]
    m_loc = [jnp.maximum(inter[h], jnp.max(logd[h], axis=-1, keepdims=True)) for h in heads]
    w_inter = [jnp.exp(inter[h] - m_loc[h]) for h in heads]
    s = [qk_t[h] * jnp.exp(logd[h] - m_loc[h]) for h in heads]
    num = [jnp.dot(s[h].astype(BF16), vh[h], preferred_element_type=F32) + wide(w_inter[h]) * q_c[h] for h in heads]
    den = [jnp.sum(s[h], axis=-1, keepdims=True) + w_inter[h] * jnp.sum(qh[h] * n_old[h], axis=-1, keepdims=True)
           for h in heads]

    b_end = [bcol[h][L - 1:L, :] for h in heads]
    log_w = [b_end[h] - bcol[h] + icol[h] for h in heads]
    m_new = [jnp.maximum(b_end[h] + m_old[h], jnp.max(log_w[h], axis=0, keepdims=True)) for h in heads]
    kw = [kh[h] * jnp.exp(log_w[h] - m_new[h]) for h in heads]
    decay = [jnp.exp(b_end[h] + m_old[h] - m_new[h]) for h in heads]
    for h in heads:
        c_ref[h] = wide(decay[h]) * c_old[h] + lax.dot_general(
            kw[h].astype(BF16), vh[h], (((0,), (0,)), ((), ())), preferred_element_type=F32)
        n_ref[h] = decay[h] * n_old[h] + jnp.sum(kw[h], axis=0, keepdims=True)
        m_ref[h] = m_new[h][:, 0:1]

    for i, (bb, h) in enumerate(streams):
        d = jnp.maximum(jnp.abs(den[i]), jnp.exp(-m_loc[i]))
        row_scale = lax.rsqrt(jnp.mean(num[i] * num[i], axis=-1, keepdims=True) + EPS * (d * d))
        yn = num[i] * wide(row_scale) * ng_ref[0:1, h * dv:(h + 1) * dv]
        o_ref[bb, :, h * dv:(h + 1) * dv] = so_ref[bb, :, h * dv:(h + 1) * dv] * yn.astype(o_ref.dtype)


def _mlstm(qk, qk_col, vproj, v_col, gates, so, so_col, conv_w, conv_b, norm_g, bsz, seq, nb):
    nc = seq // CHUNK
    L = CHUNK
    assert bsz % nb == 0
    by_batch = lambda a: a.reshape(bsz, seq, a.shape[-1])
    out = pl.pallas_call(
        _mlstm_kernel,
        grid=(bsz // nb, nc),
        in_specs=[
            pl.BlockSpec((nb, L, 2 * M_QK_W), lambda b, c: (b, c, qk_col)),
            pl.BlockSpec((nb, L, M_V_W), lambda b, c: (b, c, v_col)),
            pl.BlockSpec((nb, L, LANES), lambda b, c: (b, c, 0)),
            pl.BlockSpec((nb, L, M_V_W), lambda b, c: (b, c, so_col)),
            pl.BlockSpec((CONV_K, 2 * M_QK_W), lambda b, c: (0, 0)),
            pl.BlockSpec((8, 2 * M_QK_W), lambda b, c: (0, 0)),
            pl.BlockSpec((8, M_V_W), lambda b, c: (0, 0)),
        ],
        out_specs=pl.BlockSpec((nb, L, M_V_W), lambda b, c: (b, c, 0)),
        out_shape=jax.ShapeDtypeStruct((bsz, seq, M_V_W), BF16),
        scratch_shapes=[
            pltpu.VMEM((nb, L, 2 * M_QK_W), BF16),
            pltpu.VMEM((nb * M_HEADS, M_DQK, M_DV), F32),
            pltpu.VMEM((nb * M_HEADS, 1, M_DQK), F32),
            pltpu.VMEM((nb * M_HEADS, 1, 1), F32),
        ],
        compiler_params=_cparams(("parallel", "arbitrary")),
        name="mlstm",
    )(by_batch(qk), by_batch(vproj), by_batch(gates), by_batch(so), conv_w.astype(F32), _aux_rows(conv_b),
      _aux_rows(norm_g))
    return out.reshape(bsz * seq, M_V_W)


def _dattn_load_tile(q_ref, v_ref, vt_ref, seq):
    tv = 256

    @pl.when(pl.program_id(2) == 0)
    def _():
        for c in range(seq // tv):
            blk = v_ref[c * tv:(c + 1) * tv, :].astype(F32)
            vt_ref[:, c * tv:(c + 1) * tv] = blk.T.astype(BF16)

    qt = q_ref[...].astype(F32).T
    row = lax.broadcasted_iota(jnp.int32, qt.shape, 0)
    return (jnp.where(row < D_HD, qt, 0.0).astype(BF16), jnp.where(row >= D_HD, qt, 0.0).astype(BF16))


def _dattn_finish(lam_ref, sg_ref, o_ref, acc_ref, l1, l2, out_scale):
    ot = acc_ref[0] * (1.0 / l1) - acc_ref[1] * (lam_ref[0] / l2)
    yt = ot * lax.rsqrt(jnp.mean(ot * ot, axis=0, keepdims=True) + EPS)
    o_ref[...] = (yt.T * (sg_ref[0:1, :] * out_scale)).astype(o_ref.dtype)


def _causal_mask(s):
    kr = lax.broadcasted_iota(jnp.int32, s.shape, 0)
    qc = lax.broadcasted_iota(jnp.int32, s.shape, 1)
    return jnp.where(kr <= qc, s, NEG_BIG)


def _dattn_kernel(lam_ref, q_ref, k_ref, v_ref, sg_ref, o_ref, vt_ref, acc_ref, qz_ref, l_ref, p_ref, *,
                  seq, tq, tk, out_scale):
    n_diag = tq // tk
    assert n_diag * tk == tq and n_diag % 2 == 0
    qi = pl.program_id(2)
    qz = _dattn_load_tile(q_ref, v_ref, vt_ref, seq)
    qz_ref[0] = qz[0]
    qz_ref[1] = qz[1]
    acc_ref[...] = jnp.zeros_like(acc_ref)
    l_ref[...] = jnp.zeros_like(l_ref)

    half = tk // 2

    def block(kj, c, slot, k0, k1, c0, c1, masked):
        s = jnp.dot(kj[k0:k1], qz_ref[c, :, c0:c1], preferred_element_type=F32)
        if masked:
            s = _causal_mask(s)
        p = jnp.exp2(s)
        l_ref[c, :, c0:c1] += jnp.sum(p, axis=0, keepdims=True)
        p_ref[slot, c, k0:k1, c0:c1] = p.astype(BF16)

    def scores(j, lo, slot):
        kj = k_ref[pl.ds(pl.multiple_of(j * tk, tk), tk), :]
        for c in range(2):
            if lo is None:
                block(kj, c, slot, 0, tk, 0, tq, False)
            else:
                block(kj, c, slot, 0, half, lo, lo + tk, True)
                block(kj, c, slot, half, tk, lo + half, lo + tk, True)
                if lo + tk < tq:
                    block(kj, c, slot, 0, tk, lo + tk, tq, False)

    def accumulate(j, lo, slot):
        vtj = vt_ref[:, pl.ds(pl.multiple_of(j * tk, tk), tk)]
        for c in range(2):
            if lo is None:
                acc_ref[c] += jnp.dot(vtj, p_ref[slot, c], preferred_element_type=F32)
            else:
                acc_ref[c, :, lo:lo + half] += jnp.dot(vtj[:, :half], p_ref[slot, c, :half, lo:lo + half],
                                                       preferred_element_type=F32)
                acc_ref[c, :, lo + half:] += jnp.dot(vtj, p_ref[slot, c, :, lo + half:],
                                                     preferred_element_type=F32)

    n = n_diag * qi

    @pl.when(qi > 0)
    def _():
        scores(0, None, 0)

    def pair(i, carry):
        j = 2 * i
        accumulate(j, None, 0)
        scores(j + 1, None, 1)
        accumulate(j + 1, None, 1)
        scores(j + 2, None, 0)
        return carry

    lax.fori_loop(0, n // 2 - 1, pair, 0)

    @pl.when(qi > 0)
    def _():
        accumulate(n - 2, None, 0)
        scores(n - 1, None, 1)
        accumulate(n - 1, None, 1)
        scores(n, 0, 0)

    @pl.when(qi == 0)
    def _():
        scores(n, 0, 0)

    for d in range(n_diag):
        accumulate(n + d, d * tk, d % 2)
        if d + 1 < n_diag:
            scores(n + d + 1, (d + 1) * tk, (d + 1) % 2)
    _dattn_finish(lam_ref, sg_ref, o_ref, acc_ref, l_ref[0], l_ref[1], out_scale)


def _dattn_stabilised_kernel(lam_ref, q_ref, k_ref, v_ref, sg_ref, o_ref, vt_ref, acc_ref, *,
                             seq, tq, tk, out_scale):
    qi = pl.program_id(2)
    qz = _dattn_load_tile(q_ref, v_ref, vt_ref, seq)
    acc_ref[...] = jnp.zeros_like(acc_ref)

    def chunk(j, carry, lo):
        start = pl.multiple_of(j * tk, tk)
        kj = k_ref[pl.ds(start, tk), :]
        vtj = vt_ref[:, pl.ds(start, tk)]
        q0 = lo or 0
        out = []
        for c in range(2):
            m, l = carry[2 * c][:, q0:], carry[2 * c + 1][:, q0:]
            s = jnp.dot(kj, qz[c][:, q0:], preferred_element_type=F32)
            if lo is not None:
                s = _causal_mask(s)
            m_new = jnp.maximum(m, jnp.max(s, axis=0, keepdims=True))
            p = jnp.exp2(s - m_new)
            alpha = jnp.exp2(m - m_new)
            l_new = alpha * l + jnp.sum(p, axis=0, keepdims=True)
            acc_ref[c, :, q0:] = alpha * acc_ref[c, :, q0:] + jnp.dot(
                vtj, p.astype(BF16), preferred_element_type=F32)
            if q0:
                m_new = jnp.concatenate([carry[2 * c][:, :q0], m_new], axis=1)
                l_new = jnp.concatenate([carry[2 * c + 1][:, :q0], l_new], axis=1)
            out += [m_new, l_new]
        return tuple(out)

    init = (jnp.full((1, tq), NEG_BIG, F32), jnp.zeros((1, tq), F32)) * 2
    n_full = qi * (tq // tk)
    carry = lax.fori_loop(0, n_full, lambda j, c: chunk(j, c, None), init)
    for d in range(tq // tk):
        carry = chunk(n_full + d, carry, d * tk)

    _dattn_finish(lam_ref, sg_ref, o_ref, acc_ref, carry[1], carry[3], out_scale)


def _diff_attention(lam, qk, v, v_col0, subln_g, bsz, seq, out_scale, stabilised, tq, tk):
    n = bsz * seq
    tq = min(tq, seq)
    tk = min(tk, tq)
    nq = seq // tq
    hw = 2 * D_HD
    scratch = [pltpu.VMEM((D_DV, seq), BF16), pltpu.VMEM((2, D_DV, tq), F32)]
    if stabilised:
        body = _dattn_stabilised_kernel
    else:
        body = _dattn_kernel
        scratch += [pltpu.VMEM((2, hw, tq), BF16), pltpu.VMEM((2, 1, tq), F32), pltpu.VMEM((2, 2, tk, tq), BF16)]
    return pl.pallas_call(
        functools.partial(body, seq=seq, tq=tq, tk=tk, out_scale=out_scale),
        grid=(bsz, D_HEADS, nq),
        in_specs=[
            pl.BlockSpec(memory_space=pltpu.SMEM),
            pl.BlockSpec((tq, hw), lambda b, h, i: (b * nq + i, h)),
            pl.BlockSpec((seq, hw), lambda b, h, i: (b, D_HEADS + h)),
            pl.BlockSpec((seq, D_DV), lambda b, h, i: (b, v_col0 + h)),
            pl.BlockSpec((8, D_DV), lambda b, h, i: (0, 0)),
        ],
        out_specs=pl.BlockSpec((tq, D_DV), lambda b, h, i: (b * nq + i, h)),
        out_shape=jax.ShapeDtypeStruct((n, D_V_W), BF16),
        scratch_shapes=scratch,
        compiler_params=_cparams(("parallel", "parallel", "arbitrary")),
        name="diff_attn_stabilised" if stabilised else "diff_attn",
    )(lam.reshape(1).astype(F32), qk, qk, v, _aux_rows(subln_g))


def _xattn_kernel(q_ref, mk_ref, mv_ref, o_ref):
    for h in range(C_HEADS):
        q = q_ref[:, h * C_DQK:(h + 1) * C_DQK]
        k = mk_ref[:, h * C_DQK:(h + 1) * C_DQK]
        s = lax.dot_general(q, k, (((1,), (1,)), ((), ())), preferred_element_type=F32)
        m = jnp.max(s, axis=-1, keepdims=True)
        p = jnp.exp2(s - m)
        l = jnp.sum(p, axis=-1, keepdims=True)
        o = jnp.dot(p.astype(BF16), mv_ref[:, h * C_DV:(h + 1) * C_DV], preferred_element_type=F32)
        o_ref[:, h * C_DV:(h + 1) * C_DV] = (o / l).astype(o_ref.dtype)


def _cross_attention(q, mk, mv, bsz, seq, mem_len, tq=2048):
    n = bsz * seq
    tq = min(tq, seq)
    nq = seq // tq
    return pl.pallas_call(
        _xattn_kernel,
        grid=(bsz, nq),
        in_specs=[
            pl.BlockSpec((tq, C_Q_W), lambda b, i: (b * nq + i, 0)),
            pl.BlockSpec((mem_len, C_Q_W), lambda b, i: (b, 0)),
            pl.BlockSpec((mem_len, C_V_W), lambda b, i: (b, 0)),
        ],
        out_specs=pl.BlockSpec((tq, C_V_W), lambda b, i: (b * nq + i, 0)),
        out_shape=jax.ShapeDtypeStruct((n, C_V_W), BF16),
        compiler_params=_cparams(("parallel", "parallel")),
        name="cross_attn",
    )(q, mk, mv)


def _merge_kernel(x_ref, ym_ref, yd_ref, yc_ref, gm_ref, gd_ref, gc_ref,
                  wm_ref, wd_ref, wc_ref, wo_ref, g_ref, o_ref, h_ref):
    rows = x_ref.shape[0] // MERGE_ROW_GROUPS
    groups = [slice(r * rows, (r + 1) * rows) for r in range(MERGE_ROW_GROUPS)]
    branch = [[jnp.dot(y_ref[rs, :], w_ref[...], preferred_element_type=F32)
               for y_ref, w_ref in ((ym_ref, wm_ref), (yd_ref, wd_ref), (yc_ref, wc_ref))] for rs in groups]
    merged = [(gm_ref[rs, :].astype(F32) * b[0] + gd_ref[rs, :].astype(F32) * b[1]
               + gc_ref[rs, :].astype(F32) * b[2]).astype(BF16) for rs, b in zip(groups, branch)]
    x1 = [x_ref[rs, :] + jnp.dot(m, wo_ref[...], preferred_element_type=F32) for rs, m in zip(groups, merged)]
    for rs, v in zip(groups, x1):
        o_ref[rs, :] = v
        y = v * lax.rsqrt(jnp.mean(v * v, axis=-1, keepdims=True) + EPS)
        h_ref[rs, :] = (y * g_ref[...]).astype(BF16)


def _merge(x, ym, yd, yc, gates, g_col0, wm, wd, wc, wo, g_next, tm):
    n, d = x.shape
    tm = min(tm, n)
    row = lambda i: (i, 0)
    wspec = pl.BlockSpec((d, d), lambda i: (0, 0), pipeline_mode=pl.Buffered(1))
    return pl.pallas_call(
        _merge_kernel,
        grid=(n // tm,),
        in_specs=[
            pl.BlockSpec((tm, d), row),
            pl.BlockSpec((tm, d), row),
            pl.BlockSpec((tm, d), row),
            pl.BlockSpec((tm, d), row),
            pl.BlockSpec((tm, d), lambda i: (i, g_col0)),
            pl.BlockSpec((tm, d), lambda i: (i, g_col0 + 1)),
            pl.BlockSpec((tm, d), lambda i: (i, g_col0 + 2)),
            wspec, wspec, wspec, wspec,
            pl.BlockSpec((1, d), lambda i: (0, 0)),
        ],
        out_specs=[pl.BlockSpec((tm, d), row), pl.BlockSpec((tm, d), row)],
        out_shape=[jax.ShapeDtypeStruct((n, d), F32), jax.ShapeDtypeStruct((n, d), BF16)],
        compiler_params=_cparams(("parallel",)),
        name="merge",
    )(x, ym, yd, yc, gates, gates, gates, wm, wd, wc, wo, g_next.reshape(1, d).astype(F32))


def _mlp_kernel(x_ref, h_ref, wu_ref, wd_ref, o_ref, *, tf):
    h = h_ref[...]
    acc = x_ref[...]
    for f in range(wu_ref.shape[1] // tf):
        u = jnp.maximum(jnp.dot(h, wu_ref[:, f * tf:(f + 1) * tf], preferred_element_type=F32), 0.0)
        acc = acc + jnp.dot((u * u).astype(BF16), wd_ref[f * tf:(f + 1) * tf, :], preferred_element_type=F32)
    o_ref[...] = acc


def _mlp(x, h, wu, wd, tm, tf):
    n, d = x.shape
    dff = wu.shape[1]
    tm = min(tm, n)
    return pl.pallas_call(
        functools.partial(_mlp_kernel, tf=tf),
        grid=(n // tm,),
        in_specs=[
            pl.BlockSpec((tm, d), lambda i: (i, 0)),
            pl.BlockSpec((tm, d), lambda i: (i, 0)),
            pl.BlockSpec((d, dff), lambda i: (0, 0), pipeline_mode=pl.Buffered(1)),
            pl.BlockSpec((dff, d), lambda i: (0, 0), pipeline_mode=pl.Buffered(1)),
        ],
        out_specs=pl.BlockSpec((tm, d), lambda i: (i, 0)),
        out_shape=jax.ShapeDtypeStruct((n, d), F32),
        compiler_params=_cparams(("parallel",)),
        name="mlp",
    )(x, h, wu, wd)


def _lambda_init(layer):
    return 0.8 - 0.6 * math.exp(-0.3 * layer)


def _layer(l, x2, mem2, bsz, seq, mem_len, p):
    split = [0]
    for w in (M_QK_W, M_QK_W, M_V_W, M_HEADS, M_HEADS, M_V_W, D_Q_W, D_Q_W, D_V_W, C_Q_W):
        split.append(split[-1] + w)
    w_in = p['w_in'][l]
    g_mix = p['norm_mix_g'][l]

    def head_major(a):
        lead = a.shape[:-1]
        return a.reshape(lead + (2, M_HEADS, M_DQK)).swapaxes(-3, -2).reshape(lead + (2 * M_QK_W,))

    conv_w_half = 0.5 * head_major(p['conv_w'][l].astype(F32))
    conv_b_half = 0.5 * head_major(p['conv_b'][l].astype(F32))

    q_gain = jnp.tile(p['dq_norm_g'][l].astype(F32), D_Q_W // D_HD) * (D_HD ** -0.5 * LOG2E)
    k_gain = jnp.tile(p['dk_norm_g'][l].astype(F32), D_Q_W // D_HD)
    cq_gain = jnp.tile(p['cq_norm_g'][l].astype(F32), C_HEADS) * (C_DQK ** -0.5 * LOG2E)
    if_pad = CQ_IF_W - C_Q_W - 2 * M_HEADS
    w_all = _pack_w_in(w_in.T, p['w_gate'][l], tuple(split))
    aux_all = _aux_rows(jnp.concatenate(
        [jnp.zeros((M_V_W,), F32), 0.5 * p['b_gate'][l].astype(F32), q_gain, k_gain,
         jnp.zeros((M_V_W + D_V_W + 2 * M_QK_W,), F32), cq_gain, p['b_igate'][l].astype(F32),
         p['b_fgate'][l].astype(F32), jnp.zeros((if_pad,), F32)]))
    win = {}
    c0 = 0
    for name, width in (("gates", M_V_W + N_BRANCH * D_MODEL), ("dqk", 2 * D_Q_W),
                        ("vqk", M_V_W + D_V_W + 2 * M_QK_W), ("cq_if", CQ_IF_W)):
        win[name] = (c0, width)
        c0 += width

    vqk, h = _norm_matmul(x2, g_mix, w_all, aux_all, win["vqk"], _ep_identity, BF16, 1024, PROJ_TN, "proj_vqk",
                          wt=True)
    sgates = _matmul(h, w_all, aux_all, win["gates"], _ep_sigmoid_of_double, BF16, PROJ_WIDE_TM, PROJ_TN,
                     "proj_gates", wt=True)
    dqk = _matmul(h, w_all, aux_all, win["dqk"], functools.partial(_ep_group_norm, D_HD), BF16,
                  PROJ_NORM_TM, PROJ_TN, "proj_dqk", wt=True)
    cq, gates_if = _proj_cq_if(h, w_all, aux_all, win["cq_if"], PROJ_TM)
    w_kv = p['w_mem_kv'][l]
    w_kv = jnp.concatenate([w_kv[:, C_Q_W:], w_kv[:, :C_Q_W]], axis=1).astype(BF16)
    aux_kv = _aux_rows(jnp.concatenate([jnp.zeros((C_V_W,), F32), jnp.tile(p['ck_norm_g'][l].astype(F32), C_HEADS)]))
    mk, hmem = _norm_matmul(mem2, p['mem_norm_g'][l], w_kv, aux_kv, (C_V_W, C_Q_W),
                            functools.partial(_ep_group_norm, C_DQK), BF16, 1024, C_Q_W, "proj_mk")
    mv = _matmul(hmem, w_kv, aux_kv, (0, C_V_W), _ep_identity, BF16, 1024, C_Q_W, "proj_mv")

    y_m = _mlstm(vqk, (M_V_W + D_V_W) // (2 * M_QK_W), vqk, 0, gates_if, sgates, 0, conv_w_half, conv_b_half,
                 p['m_norm_g'][l], bsz, seq, MLSTM_NB if bsz % MLSTM_NB == 0 else 1)

    lam_i = _lambda_init(l)
    lam = (jnp.exp(jnp.sum(p['lam_q1'][l].astype(F32) * p['lam_k1'][l].astype(F32)))
           - jnp.exp(jnp.sum(p['lam_q2'][l].astype(F32) * p['lam_k2'][l].astype(F32))) + lam_i)
    score_bound = 1.02 * D_HD * jnp.max(jnp.abs(q_gain)) * jnp.max(jnp.abs(k_gain))
    dattn = functools.partial(_diff_attention, lam, dqk, vqk, M_V_W // D_DV, p['subln_g'][l], bsz, seq,
                              1.0 - lam_i)
    y_d = lax.cond(score_bound <= MAX_UNSTABILISED_SCORE,
                   lambda: dattn(False, DATTN_TQ_FAST, DATTN_TK_FAST),
                   lambda: dattn(True, DATTN_TQ_STABILISED, DATTN_TK_STABILISED))

    y_c = _cross_attention(cq, mk, mv, bsz, seq, mem_len)

    x2, h2 = _merge(x2, y_m, y_d, y_c, sgates, 1,
                    p['w_proj_m'][l].astype(BF16), p['w_proj_d'][l].astype(BF16),
                    p['w_proj_c'][l].astype(BF16), p['w_out'][l].astype(BF16), p['norm_mlp_g'][l], MERGE_TM)
    return _mlp(x2, h2, p['w_up'][l].astype(BF16), p['w_down'][l].astype(BF16), MLP_TM, MLP_TF)


def kernel(x, mem, norm_mix_g, w_in, b_igate, b_fgate, conv_w, conv_b, m_norm_g, dq_norm_g, dk_norm_g, lam_q1, lam_k1, lam_q2, lam_k2, subln_g, cq_norm_g, ck_norm_g, mem_norm_g, w_mem_kv, w_gate, b_gate, w_proj_m, w_proj_d, w_proj_c, w_out, norm_mlp_g, w_up, w_down):
    p = dict(norm_mix_g=norm_mix_g, w_in=w_in, b_igate=b_igate, b_fgate=b_fgate, conv_w=conv_w, conv_b=conv_b,
             m_norm_g=m_norm_g, dq_norm_g=dq_norm_g, dk_norm_g=dk_norm_g, lam_q1=lam_q1, lam_k1=lam_k1,
             lam_q2=lam_q2, lam_k2=lam_k2, subln_g=subln_g, cq_norm_g=cq_norm_g, ck_norm_g=ck_norm_g,
             mem_norm_g=mem_norm_g, w_mem_kv=w_mem_kv, w_gate=w_gate, b_gate=b_gate, w_proj_m=w_proj_m,
             w_proj_d=w_proj_d, w_proj_c=w_proj_c, w_out=w_out, norm_mlp_g=norm_mlp_g, w_up=w_up, w_down=w_down)
    bsz, seq, d = x.shape
    mem_len = mem.shape[1]
    x2 = x.reshape(bsz * seq, d)
    mem2 = mem.reshape(bsz * mem_len, d)
    for l in range(w_in.shape[0]):
        x2 = _layer(l, x2, mem2, bsz, seq, mem_len, p)
    return x2.reshape(bsz, seq, d)
```

```python
import functools
import math

import jax
import jax.numpy as jnp
from jax import lax
from jax.experimental import pallas as pl
from jax.experimental.pallas import tpu as pltpu

F32 = jnp.float32
BF16 = jnp.bfloat16

EPS = 1e-6
LOG2E = math.log2(math.e)
NEG_BIG = -1e30

D_MODEL = 1024
MEM_LEN = 256
M_HEADS, M_DQK, M_DV = 4, 128, 256
CHUNK = 128
CONV_K = 4
D_HEADS, D_HD = 8, 64
D_DV = 2 * D_HD
C_HEADS, C_DQK, C_DV = 4, 128, 256
D_FF = 4 * D_MODEL
N_BRANCH = 3
LANES = 128
CQ_IF_W = 1024
MXU_DIM = 256

M_QK_W = M_HEADS * M_DQK
M_V_W = M_HEADS * M_DV
D_Q_W = D_HEADS * 2 * D_HD
D_V_W = D_HEADS * D_DV
C_Q_W = C_HEADS * C_DQK
C_V_W = C_HEADS * C_DV

VMEM_LIMIT = 56 * 1024 * 1024

PROJ_TM = 2048
NORM_CALL_TM = 1024
PROJ_WIDE_TM = 1024
PROJ_NORM_TM = 1024
PROJ_TN = 1024
MERGE_TM = 512
MLP_TM = 1024
MLP_TF = 1024
MLSTM_NB = 4
DATTN_TQ_FAST = 4096
DATTN_TQ_STABILISED = 1024
DATTN_TK_FAST = 512
DATTN_TK_STABILISED = 256
MAX_UNSTABILISED_SCORE = 60.0


def _cparams(sem):
    return pltpu.CompilerParams(dimension_semantics=sem, vmem_limit_bytes=VMEM_LIMIT)


def _ep_identity(acc, aux_ref, o_ref):
    o_ref[...] = acc.astype(o_ref.dtype)


def _ep_sigmoid_of_double(acc, aux_ref, o_ref):
    o_ref[...] = (0.5 * jnp.tanh(acc + aux_ref[0:1, :]) + 0.5).astype(o_ref.dtype)


def _ep_group_norm(group, acc, aux_ref, o_ref):
    cw = MXU_DIM
    r = lax.broadcasted_iota(jnp.int32, (cw, cw), 0) // group
    c = lax.broadcasted_iota(jnp.int32, (cw, cw), 1) // group
    gmat = (r == c).astype(BF16)
    for s in range(acc.shape[1] // cw):
        cs = slice(s * cw, (s + 1) * cw)
        a = acc[:, cs]
        ss = jnp.dot((a * a).astype(BF16), gmat, preferred_element_type=F32)
        o_ref[:, cs] = (a * lax.rsqrt(ss * (1.0 / group) + EPS) * aux_ref[0:1, cs]).astype(o_ref.dtype)


def _wdot(h, w_ref, cs, wt):
    if wt:
        return lax.dot_general(h, w_ref[cs, :], (((1,), (1,)), ((), ())), preferred_element_type=F32)
    return jnp.dot(h, w_ref[:, cs], preferred_element_type=F32)


def _column_tiles(h, w_ref, aux_ref, o_ref, epilogue, tn, wt):
    for j in range(o_ref.shape[1] // tn):
        cs = slice(j * tn, (j + 1) * tn)
        epilogue(_wdot(h, w_ref, cs, wt), aux_ref.at[:, cs], o_ref.at[:, cs])


def _norm_matmul_kernel(x_ref, g_ref, w_ref, aux_ref, o_ref, h_ref, *, epilogue, tn, wt):
    x = x_ref[...].astype(F32)
    y = x * lax.rsqrt(jnp.mean(x * x, axis=-1, keepdims=True) + EPS)
    h = (y * g_ref[...]).astype(BF16)
    h_ref[...] = h
    _column_tiles(h, w_ref, aux_ref, o_ref, epilogue, tn, wt)


def _matmul_kernel(h_ref, w_ref, aux_ref, o_ref, *, epilogue, tn, wt):
    _column_tiles(h_ref[...], w_ref, aux_ref, o_ref, epilogue, tn, wt)


def _window_specs(d, window, tn, wt):
    c0, cols = window
    assert cols % tn == 0 and c0 % cols == 0
    jw = c0 // cols
    w_spec = (pl.BlockSpec((cols, d), lambda i: (jw, 0), pipeline_mode=pl.Buffered(1)) if wt else
              pl.BlockSpec((d, cols), lambda i: (0, jw), pipeline_mode=pl.Buffered(1)))
    return [w_spec, pl.BlockSpec((8, cols), lambda i: (0, jw), pipeline_mode=pl.Buffered(1))]


def _norm_matmul(x, g, w, aux, window, epilogue, out_dtype, tm, tn, name, wt=False):
    n, d = x.shape
    cols = window[1]
    tm = min(tm, n)
    tn = min(tn, cols)
    assert n % tm == 0
    return pl.pallas_call(
        functools.partial(_norm_matmul_kernel, epilogue=epilogue, tn=tn, wt=wt),
        grid=(n // tm,),
        in_specs=[pl.BlockSpec((tm, d), lambda i: (i, 0)), pl.BlockSpec((1, d), lambda i: (0, 0))]
        + _window_specs(d, window, tn, wt),
        out_specs=[pl.BlockSpec((tm, cols), lambda i: (i, 0)), pl.BlockSpec((tm, d), lambda i: (i, 0))],
        out_shape=[jax.ShapeDtypeStruct((n, cols), out_dtype), jax.ShapeDtypeStruct((n, d), BF16)],
        compiler_params=_cparams(("parallel",)),
        name=name,
    )(x, g.reshape(1, d).astype(F32), w, aux)


def _matmul(h, w, aux, window, epilogue, out_dtype, tm, tn, name, wt=False):
    n, d = h.shape
    cols = window[1]
    tm = min(tm, n)
    tn = min(tn, cols)
    assert n % tm == 0
    return pl.pallas_call(
        functools.partial(_matmul_kernel, epilogue=epilogue, tn=tn, wt=wt),
        grid=(n // tm,),
        in_specs=[pl.BlockSpec((tm, d), lambda i: (i, 0))] + _window_specs(d, window, tn, wt),
        out_specs=pl.BlockSpec((tm, cols), lambda i: (i, 0)),
        out_shape=jax.ShapeDtypeStruct((n, cols), out_dtype),
        compiler_params=_cparams(("parallel",)),
        name=name,
    )(h, w, aux)


def _pack_w_in_kernel(wt_ref, wg_ref, o_ref, *, split):
    wt = wt_ref[...]
    row = lambda a, b: wt[split[a]:split[b], :]
    parts = [0.5 * row(5, 6), 0.5 * wg_ref[...].T, row(6, 8), row(2, 3), row(8, 9)]
    for h in range(M_HEADS):
        parts += [row(0, 1)[h * M_DQK:(h + 1) * M_DQK, :], row(1, 2)[h * M_DQK:(h + 1) * M_DQK, :]]
    parts += [row(9, 10), row(3, 5), jnp.zeros((CQ_IF_W - C_Q_W - 2 * M_HEADS, wt.shape[1]), wt.dtype)]
    o_ref[...] = jnp.concatenate(parts, axis=0).astype(BF16)


def _pack_w_in(w_in_t, w_gate, split, tc=256):
    cin, d = w_in_t.shape
    cout = cin - C_Q_W - 2 * M_HEADS + CQ_IF_W + w_gate.shape[1]
    return pl.pallas_call(
        functools.partial(_pack_w_in_kernel, split=split),
        grid=(d // tc,),
        in_specs=[pl.BlockSpec((cin, tc), lambda i: (0, i)),
                  pl.BlockSpec((tc, w_gate.shape[1]), lambda i: (i, 0))],
        out_specs=pl.BlockSpec((cout, tc), lambda i: (0, i)),
        out_shape=jax.ShapeDtypeStruct((cout, d), BF16),
        compiler_params=_cparams(("parallel",)),
        name="pack_w_in",
    )(w_in_t, w_gate)


def _cq_if_kernel(h_ref, w_ref, aux_ref, ocq_ref, oif_ref):
    h = h_ref[...]
    acc = _wdot(h, w_ref, slice(0, C_Q_W), True)
    _ep_group_norm(C_DQK, acc, aux_ref.at[:, :C_Q_W], ocq_ref)
    gl = slice(C_Q_W, C_Q_W + LANES)
    oif_ref[...] = _wdot(h, w_ref, gl, True) + aux_ref[0:1, gl]


def _proj_cq_if(h, w, aux, window, tm):
    n, d = h.shape
    tm = min(tm, n)
    assert window[1] == CQ_IF_W
    return pl.pallas_call(
        _cq_if_kernel,
        grid=(n // tm,),
        in_specs=[pl.BlockSpec((tm, d), lambda i: (i, 0))] + _window_specs(d, window, CQ_IF_W, True),
        out_specs=[pl.BlockSpec((tm, C_Q_W), lambda i: (i, 0)), pl.BlockSpec((tm, LANES), lambda i: (i, 0))],
        out_shape=[jax.ShapeDtypeStruct((n, C_Q_W), BF16), jax.ShapeDtypeStruct((n, LANES), F32)],
        compiler_params=_cparams(("parallel",)),
        name="proj_cq_if",
    )(h, w, aux)


def _aux_rows(row):
    return jnp.zeros((8, row.shape[0]), F32).at[0].set(row.astype(F32))


def _mlstm_kernel(qk_ref, v_ref, gate_ref, so_ref, cw_ref, cb_ref, ng_ref, o_ref,
                  xprev_ref, c_ref, n_ref, m_ref):
    L = CHUNK
    dk, dv, nh = M_DQK, M_DV, M_HEADS
    nb = qk_ref.shape[0]

    @pl.when(pl.program_id(1) == 0)
    def _():
        xprev_ref[...] = jnp.zeros_like(xprev_ref)
        c_ref[...] = jnp.zeros_like(c_ref)
        n_ref[...] = jnp.zeros_like(n_ref)
        m_ref[...] = jnp.zeros_like(m_ref)

    n_sh = CONV_K - 1
    srow = lax.broadcasted_iota(jnp.int32, (n_sh * L, 2 * L), 0)
    scol = lax.broadcasted_iota(jnp.int32, (n_sh * L, 2 * L), 1)
    sel = (scol == (srow % L) + L - 1 - srow // L).astype(BF16)
    rr = lax.broadcasted_iota(jnp.int32, (L, L), 0)
    cc = lax.broadcasted_iota(jnp.int32, (L, L), 1)
    causal = cc <= rr
    tri = causal.astype(BF16)

    act, g, g_t, bcum, bcum_t = [], [], [], [], []
    for bb in range(nb):
        xb = qk_ref[bb]
        xx = jnp.concatenate([xprev_ref[bb], xb], axis=0)
        shifted = jnp.dot(sel, xx, preferred_element_type=F32)
        xprev_ref[bb] = xb
        z = cb_ref[0:1, :] + cw_ref[CONV_K - 1:CONV_K, :] * xb.astype(F32)
        for s in range(n_sh):
            z = z + cw_ref[CONV_K - 2 - s:CONV_K - 1 - s, :] * shifted[s * L:(s + 1) * L, :]
        act.append(z + z * jnp.tanh(z))

        gg = gate_ref[bb]
        lf = jnp.minimum(gg, 0.0) - jnp.log1p(jnp.exp(-jnp.abs(gg)))
        lf_hi = lf.astype(BF16)
        lf_lo = (lf - lf_hi.astype(F32)).astype(BF16)
        bc = (jnp.dot(tri, lf_hi, preferred_element_type=F32)
              + jnp.dot(tri, lf_lo, preferred_element_type=F32))
        g.append(gg)
        g_t.append(gg.T)
        bcum.append(bc)
        bcum_t.append(bc.T)

    streams = [(bb, h) for bb in range(nb) for h in range(nh)]
    heads = range(len(streams))
    pick_r = lax.broadcasted_iota(jnp.int32, (2 * LANES, LANES), 0) % LANES

    def hi_lo(a):
        a_hi = a.astype(BF16)
        return jnp.concatenate([a_hi, (a - a_hi.astype(F32)).astype(BF16)], axis=1)

    def lane_column(a_hi_lo, lane):
        return jnp.dot(a_hi_lo, (pick_r == lane).astype(BF16), preferred_element_type=F32)

    wide = lambda col: jnp.concatenate([col, col], axis=1)
    bcum_hl = [hi_lo(a) for a in bcum]
    g_hl = [hi_lo(a) for a in g]
    bcol = [lane_column(bcum_hl[bb], nh + h) for bb, h in streams]
    brow = [bcum_t[bb][nh + h:nh + h + 1, :] for bb, h in streams]
    irow = [g_t[bb][h:h + 1, :] for bb, h in streams]
    icol = [lane_column(g_hl[bb], h) for bb, h in streams]
    m_old = [m_ref[i] for i in heads]
    c_old = [c_ref[i] for i in heads]
    n_old = [n_ref[i] for i in heads]
    qh = [act[bb][:, 2 * dk * h:2 * dk * h + dk] * (dk ** -0.5) for bb, h in streams]
    kh = [act[bb][:, 2 * dk * h + dk:2 * dk * (h + 1)] for bb, h in streams]
    vh = [v_ref[bb, :, h * dv:(h + 1) * dv] for bb, h in streams]
    qb = [q.astype(BF16) for q in qh]
    kb = [k.astype(BF16) for k in kh]

    qk_t = [lax.dot_general(qb[h], kb[h], (((1,), (1,)), ((), ())), preferred_element_type=F32) for h in heads]
    q_c = [jnp.dot(qb[h], c_old[h].astype(BF16), preferred_element_type=F32) for h in heads]
    logd = [jnp.where(causal, bcol[h] - brow[h] + irow[h], -jnp.inf) for h in heads]
    inter = [bcol[h] + m_old[h] for h in heads]
    m_loc = [jnp.maximum(inter[h], jnp.max(logd[h], axis=-1, keepdims=True)) for h in heads]
    w_inter = [jnp.exp(inter[h] - m_loc[h]) for h in heads]
    s = [qk_t[h] * jnp.exp(logd[h] - m_loc[h]) for h in heads]
    num = [jnp.dot(s[h].astype(BF16), vh[h], preferred_element_type=F32) + wide(w_inter[h]) * q_c[h] for h in heads]
    den = [jnp.sum(s[h], axis=-1, keepdims=True) + w_inter[h] * jnp.sum(qh[h] * n_old[h], axis=-1, keepdims=True)
           for h in heads]

    b_end = [bcol[h][L - 1:L, :] for h in heads]
    log_w = [b_end[h] - bcol[h] + icol[h] for h in heads]
    m_new = [jnp.maximum(b_end[h] + m_old[h], jnp.max(log_w[h], axis=0, keepdims=True)) for h in heads]
    kw = [kh[h] * jnp.exp(log_w[h] - m_new[h]) for h in heads]
    decay = [jnp.exp(b_end[h] + m_old[h] - m_new[h]) for h in heads]
    for h in heads:
        c_ref[h] = wide(decay[h]) * c_old[h] + lax.dot_general(
            kw[h].astype(BF16), vh[h], (((0,), (0,)), ((), ())), preferred_element_type=F32)
        n_ref[h] = decay[h] * n_old[h] + jnp.sum(kw[h], axis=0, keepdims=True)
        m_ref[h] = m_new[h][:, 0:1]

    for i, (bb, h) in enumerate(streams):
        d = jnp.maximum(jnp.abs(den[i]), jnp.exp(-m_loc[i]))
        row_scale = lax.rsqrt(jnp.mean(num[i] * num[i], axis=-1, keepdims=True) + EPS * (d * d))
        yn = num[i] * wide(row_scale) * ng_ref[0:1, h * dv:(h + 1) * dv]
        o_ref[bb, :, h * dv:(h + 1) * dv] = so_ref[bb, :, h * dv:(h + 1) * dv] * yn.astype(o_ref.dtype)


def _mlstm(qk, qk_col, vproj, v_col, gates, so, so_col, conv_w, conv_b, norm_g, bsz, seq, nb):
    nc = seq // CHUNK
    L = CHUNK
    assert bsz % nb == 0
    by_batch = lambda a: a.reshape(bsz, seq, a.shape[-1])
    out = pl.pallas_call(
        _mlstm_kernel,
        grid=(bsz // nb, nc),
        in_specs=[
            pl.BlockSpec((nb, L, 2 * M_QK_W), lambda b, c: (b, c, qk_col)),
            pl.BlockSpec((nb, L, M_V_W), lambda b, c: (b, c, v_col)),
            pl.BlockSpec((nb, L, LANES), lambda b, c: (b, c, 0)),
            pl.BlockSpec((nb, L, M_V_W), lambda b, c: (b, c, so_col)),
            pl.BlockSpec((CONV_K, 2 * M_QK_W), lambda b, c: (0, 0)),
            pl.BlockSpec((8, 2 * M_QK_W), lambda b, c: (0, 0)),
            pl.BlockSpec((8, M_V_W), lambda b, c: (0, 0)),
        ],
        out_specs=pl.BlockSpec((nb, L, M_V_W), lambda b, c: (b, c, 0)),
        out_shape=jax.ShapeDtypeStruct((bsz, seq, M_V_W), BF16),
        scratch_shapes=[
            pltpu.VMEM((nb, L, 2 * M_QK_W), BF16),
            pltpu.VMEM((nb * M_HEADS, M_DQK, M_DV), F32),
            pltpu.VMEM((nb * M_HEADS, 1, M_DQK), F32),
            pltpu.VMEM((nb * M_HEADS, 1, 1), F32),
        ],
        compiler_params=_cparams(("parallel", "arbitrary")),
        name="mlstm",
    )(by_batch(qk), by_batch(vproj), by_batch(gates), by_batch(so), conv_w.astype(F32), _aux_rows(conv_b),
      _aux_rows(norm_g))
    return out.reshape(bsz * seq, M_V_W)


def _dattn_load_tile(q_ref, v_ref, vt_ref, seq):
    tv = 256

    @pl.when(pl.program_id(2) == 0)
    def _():
        for c in range(seq // tv):
            blk = v_ref[c * tv:(c + 1) * tv, :].astype(F32)
            vt_ref[:, c * tv:(c + 1) * tv] = blk.T.astype(BF16)

    qt = q_ref[...].astype(F32).T
    row = lax.broadcasted_iota(jnp.int32, qt.shape, 0)
    return (jnp.where(row < D_HD, qt, 0.0).astype(BF16), jnp.where(row >= D_HD, qt, 0.0).astype(BF16))


def _dattn_finish(lam_ref, sg_ref, o_ref, acc_ref, l1, l2, out_scale):
    ot = acc_ref[0] * (1.0 / l1) - acc_ref[1] * (lam_ref[0] / l2)
    yt = ot * lax.rsqrt(jnp.mean(ot * ot, axis=0, keepdims=True) + EPS)
    o_ref[...] = (yt.T * (sg_ref[0:1, :] * out_scale)).astype(o_ref.dtype)


def _causal_mask(s):
    kr = lax.broadcasted_iota(jnp.int32, s.shape, 0)
    qc = lax.broadcasted_iota(jnp.int32, s.shape, 1)
    return jnp.where(kr <= qc, s, NEG_BIG)


def _dattn_kernel(lam_ref, q_ref, k_ref, v_ref, sg_ref, o_ref, vt_ref, acc_ref, qz_ref, l_ref, p_ref, *,
                  seq, tq, tk, out_scale):
    n_diag = tq // tk
    assert n_diag * tk == tq and n_diag % 2 == 0
    qi = pl.program_id(2)
    qz = _dattn_load_tile(q_ref, v_ref, vt_ref, seq)
    qz_ref[0] = qz[0]
    qz_ref[1] = qz[1]
    acc_ref[...] = jnp.zeros_like(acc_ref)
    l_ref[...] = jnp.zeros_like(l_ref)

    half = tk // 2

    def block(kj, c, slot, k0, k1, c0, c1, masked):
        s = jnp.dot(kj[k0:k1], qz_ref[c, :, c0:c1], preferred_element_type=F32)
        if masked:
            s = _causal_mask(s)
        p = jnp.exp2(s)
        l_ref[c, :, c0:c1] += jnp.sum(p, axis=0, keepdims=True)
        p_ref[slot, c, k0:k1, c0:c1] = p.astype(BF16)

    def scores(j, lo, slot):
        kj = k_ref[pl.ds(pl.multiple_of(j * tk, tk), tk), :]
        for c in range(2):
            if lo is None:
                block(kj, c, slot, 0, tk, 0, tq, False)
            else:
                block(kj, c, slot, 0, half, lo, lo + tk, True)
                block(kj, c, slot, half, tk, lo + half, lo + tk, True)
                if lo + tk < tq:
                    block(kj, c, slot, 0, tk, lo + tk, tq, False)

    def accumulate(j, lo, slot):
        vtj = vt_ref[:, pl.ds(pl.multiple_of(j * tk, tk), tk)]
        for c in range(2):
            if lo is None:
                acc_ref[c] += jnp.dot(vtj, p_ref[slot, c], preferred_element_type=F32)
            else:
                acc_ref[c, :, lo:lo + half] += jnp.dot(vtj[:, :half], p_ref[slot, c, :half, lo:lo + half],
                                                       preferred_element_type=F32)
                acc_ref[c, :, lo + half:] += jnp.dot(vtj, p_ref[slot, c, :, lo + half:],
                                                     preferred_element_type=F32)

    n = n_diag * qi

    @pl.when(qi > 0)
    def _():
        scores(0, None, 0)

    def pair(i, carry):
        j = 2 * i
        accumulate(j, None, 0)
        scores(j + 1, None, 1)
        accumulate(j + 1, None, 1)
        scores(j + 2, None, 0)
        return carry

    lax.fori_loop(0, n // 2 - 1, pair, 0)

    @pl.when(qi > 0)
    def _():
        accumulate(n - 2, None, 0)
        scores(n - 1, None, 1)
        accumulate(n - 1, None, 1)
        scores(n, 0, 0)

    @pl.when(qi == 0)
    def _():
        scores(n, 0, 0)

    for d in range(n_diag):
        accumulate(n + d, d * tk, d % 2)
        if d + 1 < n_diag:
            scores(n + d + 1, (d + 1) * tk, (d + 1) % 2)
    _dattn_finish(lam_ref, sg_ref, o_ref, acc_ref, l_ref[0], l_ref[1], out_scale)


def _dattn_stabilised_kernel(lam_ref, q_ref, k_ref, v_ref, sg_ref, o_ref, vt_ref, acc_ref, *,
                             seq, tq, tk, out_scale):
    qi = pl.program_id(2)
    qz = _dattn_load_tile(q_ref, v_ref, vt_ref, seq)
    acc_ref[...] = jnp.zeros_like(acc_ref)

    def chunk(j, carry, lo):
        start = pl.multiple_of(j * tk, tk)
        kj = k_ref[pl.ds(start, tk), :]
        vtj = vt_ref[:, pl.ds(start, tk)]
        q0 = lo or 0
        out = []
        for c in range(2):
            m, l = carry[2 * c][:, q0:], carry[2 * c + 1][:, q0:]
            s = jnp.dot(kj, qz[c][:, q0:], preferred_element_type=F32)
            if lo is not None:
                s = _causal_mask(s)
            m_new = jnp.maximum(m, jnp.max(s, axis=0, keepdims=True))
            p = jnp.exp2(s - m_new)
            alpha = jnp.exp2(m - m_new)
            l_new = alpha * l + jnp.sum(p, axis=0, keepdims=True)
            acc_ref[c, :, q0:] = alpha * acc_ref[c, :, q0:] + jnp.dot(
                vtj, p.astype(BF16), preferred_element_type=F32)
            if q0:
                m_new = jnp.concatenate([carry[2 * c][:, :q0], m_new], axis=1)
                l_new = jnp.concatenate([carry[2 * c + 1][:, :q0], l_new], axis=1)
            out += [m_new, l_new]
        return tuple(out)

    init = (jnp.full((1, tq), NEG_BIG, F32), jnp.zeros((1, tq), F32)) * 2
    n_full = qi * (tq // tk)
    carry = lax.fori_loop(0, n_full, lambda j, c: chunk(j, c, None), init)
    for d in range(tq // tk):
        carry = chunk(n_full + d, carry, d * tk)

    _dattn_finish(lam_ref, sg_ref, o_ref, acc_ref, carry[1], carry[3], out_scale)


def _diff_attention(lam, qk, v, v_col0, subln_g, bsz, seq, out_scale, stabilised, tq, tk):
    n = bsz * seq
    tq = min(tq, seq)
    tk = min(tk, tq)
    nq = seq // tq
    hw = 2 * D_HD
    scratch = [pltpu.VMEM((D_DV, seq), BF16), pltpu.VMEM((2, D_DV, tq), F32)]
    if stabilised:
        body = _dattn_stabilised_kernel
    else:
        body = _dattn_kernel
        scratch += [pltpu.VMEM((2, hw, tq), BF16), pltpu.VMEM((2, 1, tq), F32), pltpu.VMEM((2, 2, tk, tq), BF16)]
    return pl.pallas_call(
        functools.partial(body, seq=seq, tq=tq, tk=tk, out_scale=out_scale),
        grid=(bsz, D_HEADS, nq),
        in_specs=[
            pl.BlockSpec(memory_space=pltpu.SMEM),
            pl.BlockSpec((tq, hw), lambda b, h, i: (b * nq + i, h)),
            pl.BlockSpec((seq, hw), lambda b, h, i: (b, D_HEADS + h)),
            pl.BlockSpec((seq, D_DV), lambda b, h, i: (b, v_col0 + h)),
            pl.BlockSpec((8, D_DV), lambda b, h, i: (0, 0)),
        ],
        out_specs=pl.BlockSpec((tq, D_DV), lambda b, h, i: (b * nq + i, h)),
        out_shape=jax.ShapeDtypeStruct((n, D_V_W), BF16),
        scratch_shapes=scratch,
        compiler_params=_cparams(("parallel", "parallel", "arbitrary")),
        name="diff_attn_stabilised" if stabilised else "diff_attn",
    )(lam.reshape(1).astype(F32), qk, qk, v, _aux_rows(subln_g))


def _xattn_kernel(q_ref, mk_ref, mv_ref, o_ref):
    for h in range(C_HEADS):
        q = q_ref[:, h * C_DQK:(h + 1) * C_DQK]
        k = mk_ref[:, h * C_DQK:(h + 1) * C_DQK]
        s = lax.dot_general(q, k, (((1,), (1,)), ((), ())), preferred_element_type=F32)
        m = jnp.max(s, axis=-1, keepdims=True)
        p = jnp.exp2(s - m)
        l = jnp.sum(p, axis=-1, keepdims=True)
        o = jnp.dot(p.astype(BF16), mv_ref[:, h * C_DV:(h + 1) * C_DV], preferred_element_type=F32)
        o_ref[:, h * C_DV:(h + 1) * C_DV] = (o / l).astype(o_ref.dtype)


def _cross_attention(q, mk, mv, bsz, seq, mem_len, tq=PROJ_TM):
    n = bsz * seq
    tq = min(tq, seq)
    nq = seq // tq
    return pl.pallas_call(
        _xattn_kernel,
        grid=(bsz, nq),
        in_specs=[
            pl.BlockSpec((tq, C_Q_W), lambda b, i: (b * nq + i, 0)),
            pl.BlockSpec((mem_len, C_Q_W), lambda b, i: (b, 0)),
            pl.BlockSpec((mem_len, C_V_W), lambda b, i: (b, 0)),
        ],
        out_specs=pl.BlockSpec((tq, C_V_W), lambda b, i: (b * nq + i, 0)),
        out_shape=jax.ShapeDtypeStruct((n, C_V_W), BF16),
        compiler_params=_cparams(("parallel", "parallel")),
        name="cross_attn",
    )(q, mk, mv)


def _merge_kernel(x_ref, ym_ref, yd_ref, yc_ref, gm_ref, gd_ref, gc_ref,
                  wm_ref, wd_ref, wc_ref, wo_ref, g_ref, o_ref, h_ref):
    merged = gm_ref[...].astype(F32) * jnp.dot(ym_ref[...], wm_ref[...], preferred_element_type=F32)
    merged = merged + gd_ref[...].astype(F32) * jnp.dot(yd_ref[...], wd_ref[...], preferred_element_type=F32)
    merged = merged + gc_ref[...].astype(F32) * jnp.dot(yc_ref[...], wc_ref[...], preferred_element_type=F32)
    x1 = x_ref[...] + jnp.dot(merged.astype(BF16), wo_ref[...], preferred_element_type=F32)
    o_ref[...] = x1
    y = x1 * lax.rsqrt(jnp.mean(x1 * x1, axis=-1, keepdims=True) + EPS)
    h_ref[...] = (y * g_ref[...]).astype(BF16)


def _merge(x, ym, yd, yc, gates, g_col0, wm, wd, wc, wo, g_next, tm):
    n, d = x.shape
    tm = min(tm, n)
    row = lambda i: (i, 0)
    wspec = pl.BlockSpec((d, d), lambda i: (0, 0), pipeline_mode=pl.Buffered(1))
    return pl.pallas_call(
        _merge_kernel,
        grid=(n // tm,),
        in_specs=[
            pl.BlockSpec((tm, d), row),
            pl.BlockSpec((tm, d), row),
            pl.BlockSpec((tm, d), row),
            pl.BlockSpec((tm, d), row),
            pl.BlockSpec((tm, d), lambda i: (i, g_col0)),
            pl.BlockSpec((tm, d), lambda i: (i, g_col0 + 1)),
            pl.BlockSpec((tm, d), lambda i: (i, g_col0 + 2)),
            wspec, wspec, wspec, wspec,
            pl.BlockSpec((1, d), lambda i: (0, 0)),
        ],
        out_specs=[pl.BlockSpec((tm, d), row), pl.BlockSpec((tm, d), row)],
        out_shape=[jax.ShapeDtypeStruct((n, d), F32), jax.ShapeDtypeStruct((n, d), BF16)],
        compiler_params=_cparams(("parallel",)),
        name="merge",
    )(x, ym, yd, yc, gates, gates, gates, wm, wd, wc, wo, g_next.reshape(1, d).astype(F32))


def _mlp_kernel(x_ref, h_ref, wu_ref, wd_ref, o_ref, *, tf):
    h = h_ref[...]
    acc = x_ref[...]
    for f in range(wu_ref.shape[1] // tf):
        u = jnp.maximum(jnp.dot(h, wu_ref[:, f * tf:(f + 1) * tf], preferred_element_type=F32), 0.0)
        acc = acc + jnp.dot((u * u).astype(BF16), wd_ref[f * tf:(f + 1) * tf, :], preferred_element_type=F32)
    o_ref[...] = acc


def _mlp(x, h, wu, wd, tm, tf):
    n, d = x.shape
    dff = wu.shape[1]
    tm = min(tm, n)
    return pl.pallas_call(
        functools.partial(_mlp_kernel, tf=tf),
        grid=(n // tm,),
        in_specs=[
            pl.BlockSpec((tm, d), lambda i: (i, 0)),
            pl.BlockSpec((tm, d), lambda i: (i, 0)),
            pl.BlockSpec((d, dff), lambda i: (0, 0), pipeline_mode=pl.Buffered(1)),
            pl.BlockSpec((dff, d), lambda i: (0, 0), pipeline_mode=pl.Buffered(1)),
        ],
        out_specs=pl.BlockSpec((tm, d), lambda i: (i, 0)),
        out_shape=jax.ShapeDtypeStruct((n, d), F32),
        compiler_params=_cparams(("parallel",)),
        name="mlp",
    )(x, h, wu, wd)


def _lambda_init(layer):
    return 0.8 - 0.6 * math.exp(-0.3 * layer)


def _layer(l, x2, mem2, bsz, seq, mem_len, p):
    split = [0]
    for w in (M_QK_W, M_QK_W, M_V_W, M_HEADS, M_HEADS, M_V_W, D_Q_W, D_Q_W, D_V_W, C_Q_W):
        split.append(split[-1] + w)
    w_in = p['w_in'][l]
    g_mix = p['norm_mix_g'][l]

    def head_major(a):
        lead = a.shape[:-1]
        return a.reshape(lead + (2, M_HEADS, M_DQK)).swapaxes(-3, -2).reshape(lead + (2 * M_QK_W,))

    conv_w_half = 0.5 * head_major(p['conv_w'][l].astype(F32))
    conv_b_half = 0.5 * head_major(p['conv_b'][l].astype(F32))

    q_gain = jnp.tile(p['dq_norm_g'][l].astype(F32), D_Q_W // D_HD) * (D_HD ** -0.5 * LOG2E)
    k_gain = jnp.tile(p['dk_norm_g'][l].astype(F32), D_Q_W // D_HD)
    cq_gain = jnp.tile(p['cq_norm_g'][l].astype(F32), C_HEADS) * (C_DQK ** -0.5 * LOG2E)
    if_pad = CQ_IF_W - C_Q_W - 2 * M_HEADS
    w_all = _pack_w_in(w_in.T, p['w_gate'][l], tuple(split))
    aux_all = _aux_rows(jnp.concatenate(
        [jnp.zeros((M_V_W,), F32), 0.5 * p['b_gate'][l].astype(F32), q_gain, k_gain,
         jnp.zeros((M_V_W + D_V_W + 2 * M_QK_W,), F32), cq_gain, p['b_igate'][l].astype(F32),
         p['b_fgate'][l].astype(F32), jnp.zeros((if_pad,), F32)]))
    win = {}
    c0 = 0
    for name, width in (("gates", M_V_W + N_BRANCH * D_MODEL), ("dqk", 2 * D_Q_W),
                        ("vqk", M_V_W + D_V_W + 2 * M_QK_W), ("cq_if", CQ_IF_W)):
        win[name] = (c0, width)
        c0 += width

    vqk, h = _norm_matmul(x2, g_mix, w_all, aux_all, win["vqk"], _ep_identity, BF16, NORM_CALL_TM, PROJ_TN,
                          "proj_vqk", wt=True)
    sgates = _matmul(h, w_all, aux_all, win["gates"], _ep_sigmoid_of_double, BF16, PROJ_WIDE_TM, PROJ_TN,
                     "proj_gates", wt=True)
    dqk = _matmul(h, w_all, aux_all, win["dqk"], functools.partial(_ep_group_norm, D_HD), BF16,
                  PROJ_NORM_TM, PROJ_TN, "proj_dqk", wt=True)
    cq, gates_if = _proj_cq_if(h, w_all, aux_all, win["cq_if"], PROJ_TM)
    w_kv = p['w_mem_kv'][l]
    w_kv = jnp.concatenate([w_kv[:, C_Q_W:], w_kv[:, :C_Q_W]], axis=1).astype(BF16)
    aux_kv = _aux_rows(jnp.concatenate([jnp.zeros((C_V_W,), F32), jnp.tile(p['ck_norm_g'][l].astype(F32), C_HEADS)]))
    mk, hmem = _norm_matmul(mem2, p['mem_norm_g'][l], w_kv, aux_kv, (C_V_W, C_Q_W),
                            functools.partial(_ep_group_norm, C_DQK), BF16, NORM_CALL_TM, C_Q_W, "proj_mk")
    mv = _matmul(hmem, w_kv, aux_kv, (0, C_V_W), _ep_identity, BF16, NORM_CALL_TM, C_Q_W, "proj_mv")

    y_m = _mlstm(vqk, (M_V_W + D_V_W) // (2 * M_QK_W), vqk, 0, gates_if, sgates, 0, conv_w_half, conv_b_half,
                 p['m_norm_g'][l], bsz, seq, MLSTM_NB if bsz % MLSTM_NB == 0 else 1)

    lam_i = _lambda_init(l)
    lam = (jnp.exp(jnp.sum(p['lam_q1'][l].astype(F32) * p['lam_k1'][l].astype(F32)))
           - jnp.exp(jnp.sum(p['lam_q2'][l].astype(F32) * p['lam_k2'][l].astype(F32))) + lam_i)
    score_bound = 1.02 * D_HD * jnp.max(jnp.abs(q_gain)) * jnp.max(jnp.abs(k_gain))
    dattn = functools.partial(_diff_attention, lam, dqk, vqk, M_V_W // D_DV, p['subln_g'][l], bsz, seq,
                              1.0 - lam_i)
    y_d = lax.cond(score_bound <= MAX_UNSTABILISED_SCORE,
                   lambda: dattn(False, DATTN_TQ_FAST, DATTN_TK_FAST),
                   lambda: dattn(True, DATTN_TQ_STABILISED, DATTN_TK_STABILISED))

    y_c = _cross_attention(cq, mk, mv, bsz, seq, mem_len)

    x2, h2 = _merge(x2, y_m, y_d, y_c, sgates, 1,
                    p['w_proj_m'][l].astype(BF16), p['w_proj_d'][l].astype(BF16),
                    p['w_proj_c'][l].astype(BF16), p['w_out'][l].astype(BF16), p['norm_mlp_g'][l], MERGE_TM)
    return _mlp(x2, h2, p['w_up'][l].astype(BF16), p['w_down'][l].astype(BF16), MLP_TM, MLP_TF)


def kernel(x, mem, norm_mix_g, w_in, b_igate, b_fgate, conv_w, conv_b, m_norm_g, dq_norm_g, dk_norm_g, lam_q1, lam_k1, lam_q2, lam_k2, subln_g, cq_norm_g, ck_norm_g, mem_norm_g, w_mem_kv, w_gate, b_gate, w_proj_m, w_proj_d, w_proj_c, w_out, norm_mlp_g, w_up, w_down):
    p = dict(norm_mix_g=norm_mix_g, w_in=w_in, b_igate=b_igate, b_fgate=b_fgate, conv_w=conv_w, conv_b=conv_b,
             m_norm_g=m_norm_g, dq_norm_g=dq_norm_g, dk_norm_g=dk_norm_g, lam_q1=lam_q1, lam_k1=lam_k1,
             lam_q2=lam_q2, lam_k2=lam_k2, subln_g=subln_g, cq_norm_g=cq_norm_g, ck_norm_g=ck_norm_g,
             mem_norm_g=mem_norm_g, w_mem_kv=w_mem_kv, w_gate=w_gate, b_gate=b_gate, w_proj_m=w_proj_m,
             w_proj_d=w_proj_d, w_proj_c=w_proj_c, w_out=w_out, norm_mlp_g=norm_mlp_g, w_up=w_up, w_down=w_down)
    bsz, seq, d = x.shape
    mem_len = mem.shape[1]
    x2 = x.reshape(bsz * seq, d)
    mem2 = mem.reshape(bsz * mem_len, d)
    for l in range(w_in.shape[0]):
        x2 = _layer(l, x2, mem2, bsz, seq, mem_len, p)
    return x2.reshape(bsz, seq, d)
```

```python
import functools
import math

import jax
import jax.numpy as jnp
from jax import lax
from jax.experimental import pallas as pl
from jax.experimental.pallas import tpu as pltpu

F32 = jnp.float32
BF16 = jnp.bfloat16

EPS = 1e-6
LOG2E = math.log2(math.e)
NEG_BIG = -1e30

D_MODEL = 1024
MEM_LEN = 256
M_HEADS, M_DQK, M_DV = 4, 128, 256
CHUNK = 128
CONV_K = 4
D_HEADS, D_HD = 8, 64
D_DV = 2 * D_HD
C_HEADS, C_DQK, C_DV = 4, 128, 256
D_FF = 4 * D_MODEL
N_BRANCH = 3
LANES = 128
CQ_IF_W = 1024
MXU_DIM = 256

M_QK_W = M_HEADS * M_DQK
M_V_W = M_HEADS * M_DV
D_Q_W = D_HEADS * 2 * D_HD
D_V_W = D_HEADS * D_DV
C_Q_W = C_HEADS * C_DQK
C_V_W = C_HEADS * C_DV

VMEM_LIMIT = 56 * 1024 * 1024

PROJ_TM = 2048
SMALL_CALL_TM = 4096
NORM_CALL_TM = 1024
PROJ_WIDE_TM = 1024
PROJ_NORM_TM = 1024
PROJ_TN = 1024
MERGE_TM = 512
MLP_TM = 1024
MLP_TF = 1024
MLSTM_NB = 4
DATTN_TQ_FAST = 4096
DATTN_TQ_STABILISED = 1024
DATTN_TK_FAST = 512
DATTN_TK_STABILISED = 256
MAX_UNSTABILISED_SCORE = 60.0


def _cparams(sem):
    return pltpu.CompilerParams(dimension_semantics=sem, vmem_limit_bytes=VMEM_LIMIT)


def _ep_identity(acc, aux_ref, o_ref):
    o_ref[...] = acc.astype(o_ref.dtype)


def _ep_sigmoid_of_double(acc, aux_ref, o_ref):
    o_ref[...] = (0.5 * jnp.tanh(acc + aux_ref[0:1, :]) + 0.5).astype(o_ref.dtype)


def _ep_group_norm(group, acc, aux_ref, o_ref):
    cw = MXU_DIM
    r = lax.broadcasted_iota(jnp.int32, (cw, cw), 0) // group
    c = lax.broadcasted_iota(jnp.int32, (cw, cw), 1) // group
    gmat = (r == c).astype(BF16)
    for s in range(acc.shape[1] // cw):
        cs = slice(s * cw, (s + 1) * cw)
        a = acc[:, cs]
        ss = jnp.dot((a * a).astype(BF16), gmat, preferred_element_type=F32)
        o_ref[:, cs] = (a * lax.rsqrt(ss * (1.0 / group) + EPS) * aux_ref[0:1, cs]).astype(o_ref.dtype)


def _wdot(h, w_ref, cs, wt):
    if wt:
        return lax.dot_general(h, w_ref[cs, :], (((1,), (1,)), ((), ())), preferred_element_type=F32)
    return jnp.dot(h, w_ref[:, cs], preferred_element_type=F32)


def _column_tiles(h, w_ref, aux_ref, o_ref, epilogue, tn, wt):
    for j in range(o_ref.shape[1] // tn):
        cs = slice(j * tn, (j + 1) * tn)
        epilogue(_wdot(h, w_ref, cs, wt), aux_ref.at[:, cs], o_ref.at[:, cs])


def _norm_matmul_kernel(x_ref, g_ref, w_ref, aux_ref, o_ref, h_ref, *, epilogue, tn, wt):
    x = x_ref[...].astype(F32)
    y = x * lax.rsqrt(jnp.mean(x * x, axis=-1, keepdims=True) + EPS)
    h = (y * g_ref[...]).astype(BF16)
    h_ref[...] = h
    _column_tiles(h, w_ref, aux_ref, o_ref, epilogue, tn, wt)


def _matmul_kernel(h_ref, w_ref, aux_ref, o_ref, *, epilogue, tn, wt):
    _column_tiles(h_ref[...], w_ref, aux_ref, o_ref, epilogue, tn, wt)


def _window_specs(d, window, tn, wt):
    c0, cols = window
    assert cols % tn == 0 and c0 % cols == 0
    jw = c0 // cols
    w_spec = (pl.BlockSpec((cols, d), lambda i: (jw, 0), pipeline_mode=pl.Buffered(1)) if wt else
              pl.BlockSpec((d, cols), lambda i: (0, jw), pipeline_mode=pl.Buffered(1)))
    return [w_spec, pl.BlockSpec((8, cols), lambda i: (0, jw), pipeline_mode=pl.Buffered(1))]


def _norm_matmul(x, g, w, aux, window, epilogue, out_dtype, tm, tn, name, wt=False):
    n, d = x.shape
    cols = window[1]
    tm = min(tm, n)
    tn = min(tn, cols)
    assert n % tm == 0
    return pl.pallas_call(
        functools.partial(_norm_matmul_kernel, epilogue=epilogue, tn=tn, wt=wt),
        grid=(n // tm,),
        in_specs=[pl.BlockSpec((tm, d), lambda i: (i, 0)), pl.BlockSpec((1, d), lambda i: (0, 0))]
        + _window_specs(d, window, tn, wt),
        out_specs=[pl.BlockSpec((tm, cols), lambda i: (i, 0)), pl.BlockSpec((tm, d), lambda i: (i, 0))],
        out_shape=[jax.ShapeDtypeStruct((n, cols), out_dtype), jax.ShapeDtypeStruct((n, d), BF16)],
        compiler_params=_cparams(("parallel",)),
        name=name,
    )(x, g.reshape(1, d).astype(F32), w, aux)


def _matmul(h, w, aux, window, epilogue, out_dtype, tm, tn, name, wt=False):
    n, d = h.shape
    cols = window[1]
    tm = min(tm, n)
    tn = min(tn, cols)
    assert n % tm == 0
    return pl.pallas_call(
        functools.partial(_matmul_kernel, epilogue=epilogue, tn=tn, wt=wt),
        grid=(n // tm,),
        in_specs=[pl.BlockSpec((tm, d), lambda i: (i, 0))] + _window_specs(d, window, tn, wt),
        out_specs=pl.BlockSpec((tm, cols), lambda i: (i, 0)),
        out_shape=jax.ShapeDtypeStruct((n, cols), out_dtype),
        compiler_params=_cparams(("parallel",)),
        name=name,
    )(h, w, aux)


def _pack_w_in_kernel(wt_ref, wg_ref, o_ref, *, split):
    wt = wt_ref[...]
    row = lambda a, b: wt[split[a]:split[b], :]
    parts = [0.5 * row(5, 6), 0.5 * wg_ref[...].T, row(6, 8), row(2, 3), row(8, 9)]
    for h in range(M_HEADS):
        parts += [row(0, 1)[h * M_DQK:(h + 1) * M_DQK, :], row(1, 2)[h * M_DQK:(h + 1) * M_DQK, :]]
    parts += [row(9, 10), row(3, 5), jnp.zeros((CQ_IF_W - C_Q_W - 2 * M_HEADS, wt.shape[1]), wt.dtype)]
    o_ref[...] = jnp.concatenate(parts, axis=0).astype(BF16)


def _pack_w_in(w_in_t, w_gate, split, tc=256):
    cin, d = w_in_t.shape
    cout = cin - C_Q_W - 2 * M_HEADS + CQ_IF_W + w_gate.shape[1]
    return pl.pallas_call(
        functools.partial(_pack_w_in_kernel, split=split),
        grid=(d // tc,),
        in_specs=[pl.BlockSpec((cin, tc), lambda i: (0, i)),
                  pl.BlockSpec((tc, w_gate.shape[1]), lambda i: (i, 0))],
        out_specs=pl.BlockSpec((cout, tc), lambda i: (0, i)),
        out_shape=jax.ShapeDtypeStruct((cout, d), BF16),
        compiler_params=_cparams(("parallel",)),
        name="pack_w_in",
    )(w_in_t, w_gate)


def _cq_if_kernel(h_ref, w_ref, aux_ref, ocq_ref, oif_ref):
    h = h_ref[...]
    acc = _wdot(h, w_ref, slice(0, C_Q_W), True)
    _ep_group_norm(C_DQK, acc, aux_ref.at[:, :C_Q_W], ocq_ref)
    gl = slice(C_Q_W, C_Q_W + LANES)
    oif_ref[...] = _wdot(h, w_ref, gl, True) + aux_ref[0:1, gl]


def _proj_cq_if(h, w, aux, window, tm):
    n, d = h.shape
    tm = min(tm, n)
    assert window[1] == CQ_IF_W
    return pl.pallas_call(
        _cq_if_kernel,
        grid=(n // tm,),
        in_specs=[pl.BlockSpec((tm, d), lambda i: (i, 0))] + _window_specs(d, window, CQ_IF_W, True),
        out_specs=[pl.BlockSpec((tm, C_Q_W), lambda i: (i, 0)), pl.BlockSpec((tm, LANES), lambda i: (i, 0))],
        out_shape=[jax.ShapeDtypeStruct((n, C_Q_W), BF16), jax.ShapeDtypeStruct((n, LANES), F32)],
        compiler_params=_cparams(("parallel",)),
        name="proj_cq_if",
    )(h, w, aux)


def _aux_rows(row):
    return jnp.zeros((8, row.shape[0]), F32).at[0].set(row.astype(F32))


def _mlstm_kernel(qk_ref, v_ref, gate_ref, so_ref, cw_ref, cb_ref, ng_ref, o_ref,
                  xprev_ref, c_ref, n_ref, m_ref):
    L = CHUNK
    dk, dv, nh = M_DQK, M_DV, M_HEADS
    nb = qk_ref.shape[0]

    @pl.when(pl.program_id(1) == 0)
    def _():
        xprev_ref[...] = jnp.zeros_like(xprev_ref)
        c_ref[...] = jnp.zeros_like(c_ref)
        n_ref[...] = jnp.zeros_like(n_ref)
        m_ref[...] = jnp.zeros_like(m_ref)

    n_sh = CONV_K - 1
    srow = lax.broadcasted_iota(jnp.int32, (n_sh * L, 2 * L), 0)
    scol = lax.broadcasted_iota(jnp.int32, (n_sh * L, 2 * L), 1)
    sel = (scol == (srow % L) + L - 1 - srow // L).astype(BF16)
    rr = lax.broadcasted_iota(jnp.int32, (L, L), 0)
    cc = lax.broadcasted_iota(jnp.int32, (L, L), 1)
    causal = cc <= rr
    tri = causal.astype(BF16)

    act, g, g_t, bcum, bcum_t = [], [], [], [], []
    for bb in range(nb):
        xb = qk_ref[bb]
        xx = jnp.concatenate([xprev_ref[bb], xb], axis=0)
        shifted = jnp.dot(sel, xx, preferred_element_type=F32)
        xprev_ref[bb] = xb
        z = cb_ref[0:1, :] + cw_ref[CONV_K - 1:CONV_K, :] * xb.astype(F32)
        for s in range(n_sh):
            z = z + cw_ref[CONV_K - 2 - s:CONV_K - 1 - s, :] * shifted[s * L:(s + 1) * L, :]
        act.append(z + z * jnp.tanh(z))

        gg = gate_ref[bb]
        lf = jnp.minimum(gg, 0.0) - jnp.log1p(jnp.exp(-jnp.abs(gg)))
        lf_hi = lf.astype(BF16)
        lf_lo = (lf - lf_hi.astype(F32)).astype(BF16)
        bc = (jnp.dot(tri, lf_hi, preferred_element_type=F32)
              + jnp.dot(tri, lf_lo, preferred_element_type=F32))
        g.append(gg)
        g_t.append(gg.T)
        bcum.append(bc)
        bcum_t.append(bc.T)

    streams = [(bb, h) for bb in range(nb) for h in range(nh)]
    heads = range(len(streams))
    pick_r = lax.broadcasted_iota(jnp.int32, (2 * LANES, LANES), 0) % LANES

    def hi_lo(a):
        a_hi = a.astype(BF16)
        return jnp.concatenate([a_hi, (a - a_hi.astype(F32)).astype(BF16)], axis=1)

    def lane_column(a_hi_lo, lane):
        return jnp.dot(a_hi_lo, (pick_r == lane).astype(BF16), preferred_element_type=F32)

    wide = lambda col: jnp.concatenate([col, col], axis=1)
    bcum_hl = [hi_lo(a) for a in bcum]
    g_hl = [hi_lo(a) for a in g]
    bcol = [lane_column(bcum_hl[bb], nh + h) for bb, h in streams]
    brow = [bcum_t[bb][nh + h:nh + h + 1, :] for bb, h in streams]
    irow = [g_t[bb][h:h + 1, :] for bb, h in streams]
    icol = [lane_column(g_hl[bb], h) for bb, h in streams]
    m_old = [m_ref[i] for i in heads]
    c_old = [c_ref[i] for i in heads]
    n_old = [n_ref[i] for i in heads]
    qh = [act[bb][:, 2 * dk * h:2 * dk * h + dk] * (dk ** -0.5) for bb, h in streams]
    kh = [act[bb][:, 2 * dk * h + dk:2 * dk * (h + 1)] for bb, h in streams]
    vh = [v_ref[bb, :, h * dv:(h + 1) * dv] for bb, h in streams]
    qb = [q.astype(BF16) for q in qh]
    kb = [k.astype(BF16) for k in kh]

    qk_t = [lax.dot_general(qb[h], kb[h], (((1,), (1,)), ((), ())), preferred_element_type=F32) for h in heads]
    q_c = [jnp.dot(qb[h], c_old[h].astype(BF16), preferred_element_type=F32) for h in heads]
    logd = [jnp.where(causal, bcol[h] - brow[h] + irow[h], -jnp.inf) for h in heads]
    inter = [bcol[h] + m_old[h] for h in heads]
    m_loc = [jnp.maximum(inter[h], jnp.max(logd[h], axis=-1, keepdims=True)) for h in heads]
    w_inter = [jnp.exp(inter[h] - m_loc[h]) for h in heads]
    s = [qk_t[h] * jnp.exp(logd[h] - m_loc[h]) for h in heads]
    num = [jnp.dot(s[h].astype(BF16), vh[h], preferred_element_type=F32) + wide(w_inter[h]) * q_c[h] for h in heads]
    den = [jnp.sum(s[h], axis=-1, keepdims=True) + w_inter[h] * jnp.sum(qh[h] * n_old[h], axis=-1, keepdims=True)
           for h in heads]

    b_end = [bcol[h][L - 1:L, :] for h in heads]
    log_w = [b_end[h] - bcol[h] + icol[h] for h in heads]
    m_new = [jnp.maximum(b_end[h] + m_old[h], jnp.max(log_w[h], axis=0, keepdims=True)) for h in heads]
    kw = [kh[h] * jnp.exp(log_w[h] - m_new[h]) for h in heads]
    decay = [jnp.exp(b_end[h] + m_old[h] - m_new[h]) for h in heads]
    for h in heads:
        c_ref[h] = wide(decay[h]) * c_old[h] + lax.dot_general(
            kw[h].astype(BF16), vh[h], (((0,), (0,)), ((), ())), preferred_element_type=F32)
        n_ref[h] = decay[h] * n_old[h] + jnp.sum(kw[h], axis=0, keepdims=True)
        m_ref[h] = m_new[h][:, 0:1]

    for i, (bb, h) in enumerate(streams):
        d = jnp.maximum(jnp.abs(den[i]), jnp.exp(-m_loc[i]))
        row_scale = lax.rsqrt(jnp.mean(num[i] * num[i], axis=-1, keepdims=True) + EPS * (d * d))
        yn = num[i] * wide(row_scale) * ng_ref[0:1, h * dv:(h + 1) * dv]
        o_ref[bb, :, h * dv:(h + 1) * dv] = so_ref[bb, :, h * dv:(h + 1) * dv] * yn.astype(o_ref.dtype)


def _mlstm(qk, qk_col, vproj, v_col, gates, so, so_col, conv_w, conv_b, norm_g, bsz, seq, nb):
    nc = seq // CHUNK
    L = CHUNK
    assert bsz % nb == 0
    by_batch = lambda a: a.reshape(bsz, seq, a.shape[-1])
    out = pl.pallas_call(
        _mlstm_kernel,
        grid=(bsz // nb, nc),
        in_specs=[
            pl.BlockSpec((nb, L, 2 * M_QK_W), lambda b, c: (b, c, qk_col)),
            pl.BlockSpec((nb, L, M_V_W), lambda b, c: (b, c, v_col)),
            pl.BlockSpec((nb, L, LANES), lambda b, c: (b, c, 0)),
            pl.BlockSpec((nb, L, M_V_W), lambda b, c: (b, c, so_col)),
            pl.BlockSpec((CONV_K, 2 * M_QK_W), lambda b, c: (0, 0)),
            pl.BlockSpec((8, 2 * M_QK_W), lambda b, c: (0, 0)),
            pl.BlockSpec((8, M_V_W), lambda b, c: (0, 0)),
        ],
        out_specs=pl.BlockSpec((nb, L, M_V_W), lambda b, c: (b, c, 0)),
        out_shape=jax.ShapeDtypeStruct((bsz, seq, M_V_W), BF16),
        scratch_shapes=[
            pltpu.VMEM((nb, L, 2 * M_QK_W), BF16),
            pltpu.VMEM((nb * M_HEADS, M_DQK, M_DV), F32),
            pltpu.VMEM((nb * M_HEADS, 1, M_DQK), F32),
            pltpu.VMEM((nb * M_HEADS, 1, 1), F32),
        ],
        compiler_params=_cparams(("parallel", "arbitrary")),
        name="mlstm",
    )(by_batch(qk), by_batch(vproj), by_batch(gates), by_batch(so), conv_w.astype(F32), _aux_rows(conv_b),
      _aux_rows(norm_g))
    return out.reshape(bsz * seq, M_V_W)


def _dattn_load_tile(q_ref, v_ref, vt_ref, seq):
    tv = 256

    @pl.when(pl.program_id(2) == 0)
    def _():
        for c in range(seq // tv):
            blk = v_ref[c * tv:(c + 1) * tv, :].astype(F32)
            vt_ref[:, c * tv:(c + 1) * tv] = blk.T.astype(BF16)

    qt = q_ref[...].astype(F32).T
    row = lax.broadcasted_iota(jnp.int32, qt.shape, 0)
    return (jnp.where(row < D_HD, qt, 0.0).astype(BF16), jnp.where(row >= D_HD, qt, 0.0).astype(BF16))


def _dattn_finish(lam_ref, sg_ref, o_ref, acc_ref, l1, l2, out_scale):
    ot = acc_ref[0] * (1.0 / l1) - acc_ref[1] * (lam_ref[0] / l2)
    yt = ot * lax.rsqrt(jnp.mean(ot * ot, axis=0, keepdims=True) + EPS)
    o_ref[...] = (yt.T * (sg_ref[0:1, :] * out_scale)).astype(o_ref.dtype)


def _causal_mask(s):
    kr = lax.broadcasted_iota(jnp.int32, s.shape, 0)
    qc = lax.broadcasted_iota(jnp.int32, s.shape, 1)
    return jnp.where(kr <= qc, s, NEG_BIG)


def _dattn_kernel(lam_ref, q_ref, k_ref, v_ref, sg_ref, o_ref, vt_ref, acc_ref, qz_ref, l_ref, p_ref, *,
                  seq, tq, tk, out_scale):
    n_diag = tq // tk
    assert n_diag * tk == tq and n_diag % 2 == 0
    qi = pl.program_id(2)
    qz = _dattn_load_tile(q_ref, v_ref, vt_ref, seq)
    qz_ref[0] = qz[0]
    qz_ref[1] = qz[1]
    acc_ref[...] = jnp.zeros_like(acc_ref)
    l_ref[...] = jnp.zeros_like(l_ref)

    half = tk // 2

    def block(kj, c, slot, k0, k1, c0, c1, masked):
        s = jnp.dot(kj[k0:k1], qz_ref[c, :, c0:c1], preferred_element_type=F32)
        if masked:
            s = _causal_mask(s)
        p = jnp.exp2(s)
        l_ref[c, :, c0:c1] += jnp.sum(p, axis=0, keepdims=True)
        p_ref[slot, c, k0:k1, c0:c1] = p.astype(BF16)

    def scores(j, lo, slot):
        kj = k_ref[pl.ds(pl.multiple_of(j * tk, tk), tk), :]
        for c in range(2):
            if lo is None:
                block(kj, c, slot, 0, tk, 0, tq, False)
            else:
                block(kj, c, slot, 0, half, lo, lo + tk, True)
                block(kj, c, slot, half, tk, lo + half, lo + tk, True)
                if lo + tk < tq:
                    block(kj, c, slot, 0, tk, lo + tk, tq, False)

    def accumulate(j, lo, slot):
        vtj = vt_ref[:, pl.ds(pl.multiple_of(j * tk, tk), tk)]
        for c in range(2):
            if lo is None:
                acc_ref[c] += jnp.dot(vtj, p_ref[slot, c], preferred_element_type=F32)
            else:
                acc_ref[c, :, lo:lo + half] += jnp.dot(vtj[:, :half], p_ref[slot, c, :half, lo:lo + half],
                                                       preferred_element_type=F32)
                acc_ref[c, :, lo + half:] += jnp.dot(vtj, p_ref[slot, c, :, lo + half:],
                                                     preferred_element_type=F32)

    n = n_diag * qi

    @pl.when(qi > 0)
    def _():
        scores(0, None, 0)

    def pair(i, carry):
        j = 2 * i
        accumulate(j, None, 0)
        scores(j + 1, None, 1)
        accumulate(j + 1, None, 1)
        scores(j + 2, None, 0)
        return carry

    lax.fori_loop(0, n // 2 - 1, pair, 0)

    @pl.when(qi > 0)
    def _():
        accumulate(n - 2, None, 0)
        scores(n - 1, None, 1)
        accumulate(n - 1, None, 1)
        scores(n, 0, 0)

    @pl.when(qi == 0)
    def _():
        scores(n, 0, 0)

    for d in range(n_diag):
        accumulate(n + d, d * tk, d % 2)
        if d + 1 < n_diag:
            scores(n + d + 1, (d + 1) * tk, (d + 1) % 2)
    _dattn_finish(lam_ref, sg_ref, o_ref, acc_ref, l_ref[0], l_ref[1], out_scale)


def _dattn_stabilised_kernel(lam_ref, q_ref, k_ref, v_ref, sg_ref, o_ref, vt_ref, acc_ref, *,
                             seq, tq, tk, out_scale):
    qi = pl.program_id(2)
    qz = _dattn_load_tile(q_ref, v_ref, vt_ref, seq)
    acc_ref[...] = jnp.zeros_like(acc_ref)

    def chunk(j, carry, lo):
        start = pl.multiple_of(j * tk, tk)
        kj = k_ref[pl.ds(start, tk), :]
        vtj = vt_ref[:, pl.ds(start, tk)]
        q0 = lo or 0
        out = []
        for c in range(2):
            m, l = carry[2 * c][:, q0:], carry[2 * c + 1][:, q0:]
            s = jnp.dot(kj, qz[c][:, q0:], preferred_element_type=F32)
            if lo is not None:
                s = _causal_mask(s)
            m_new = jnp.maximum(m, jnp.max(s, axis=0, keepdims=True))
            p = jnp.exp2(s - m_new)
            alpha = jnp.exp2(m - m_new)
            l_new = alpha * l + jnp.sum(p, axis=0, keepdims=True)
            acc_ref[c, :, q0:] = alpha * acc_ref[c, :, q0:] + jnp.dot(
                vtj, p.astype(BF16), preferred_element_type=F32)
            if q0:
                m_new = jnp.concatenate([carry[2 * c][:, :q0], m_new], axis=1)
                l_new = jnp.concatenate([carry[2 * c + 1][:, :q0], l_new], axis=1)
            out += [m_new, l_new]
        return tuple(out)

    init = (jnp.full((1, tq), NEG_BIG, F32), jnp.zeros((1, tq), F32)) * 2
    n_full = qi * (tq // tk)
    carry = lax.fori_loop(0, n_full, lambda j, c: chunk(j, c, None), init)
    for d in range(tq // tk):
        carry = chunk(n_full + d, carry, d * tk)

    _dattn_finish(lam_ref, sg_ref, o_ref, acc_ref, carry[1], carry[3], out_scale)


def _diff_attention(lam, qk, v, v_col0, subln_g, bsz, seq, out_scale, stabilised, tq, tk):
    n = bsz * seq
    tq = min(tq, seq)
    tk = min(tk, tq)
    nq = seq // tq
    hw = 2 * D_HD
    scratch = [pltpu.VMEM((D_DV, seq), BF16), pltpu.VMEM((2, D_DV, tq), F32)]
    if stabilised:
        body = _dattn_stabilised_kernel
    else:
        body = _dattn_kernel
        scratch += [pltpu.VMEM((2, hw, tq), BF16), pltpu.VMEM((2, 1, tq), F32), pltpu.VMEM((2, 2, tk, tq), BF16)]
    return pl.pallas_call(
        functools.partial(body, seq=seq, tq=tq, tk=tk, out_scale=out_scale),
        grid=(bsz, D_HEADS, nq),
        in_specs=[
            pl.BlockSpec(memory_space=pltpu.SMEM),
            pl.BlockSpec((tq, hw), lambda b, h, i: (b * nq + i, h)),
            pl.BlockSpec((seq, hw), lambda b, h, i: (b, D_HEADS + h)),
            pl.BlockSpec((seq, D_DV), lambda b, h, i: (b, v_col0 + h)),
            pl.BlockSpec((8, D_DV), lambda b, h, i: (0, 0)),
        ],
        out_specs=pl.BlockSpec((tq, D_DV), lambda b, h, i: (b * nq + i, h)),
        out_shape=jax.ShapeDtypeStruct((n, D_V_W), BF16),
        scratch_shapes=scratch,
        compiler_params=_cparams(("parallel", "parallel", "arbitrary")),
        name="diff_attn_stabilised" if stabilised else "diff_attn",
    )(lam.reshape(1).astype(F32), qk, qk, v, _aux_rows(subln_g))


def _xattn_kernel(q_ref, mk_ref, mv_ref, o_ref):
    for h in range(C_HEADS):
        q = q_ref[:, h * C_DQK:(h + 1) * C_DQK]
        k = mk_ref[:, h * C_DQK:(h + 1) * C_DQK]
        s = lax.dot_general(q, k, (((1,), (1,)), ((), ())), preferred_element_type=F32)
        m = jnp.max(s, axis=-1, keepdims=True)
        p = jnp.exp2(s - m)
        l = jnp.sum(p, axis=-1, keepdims=True)
        o = jnp.dot(p.astype(BF16), mv_ref[:, h * C_DV:(h + 1) * C_DV], preferred_element_type=F32)
        o_ref[:, h * C_DV:(h + 1) * C_DV] = (o / l).astype(o_ref.dtype)


def _cross_attention(q, mk, mv, bsz, seq, mem_len, tq=SMALL_CALL_TM):
    n = bsz * seq
    tq = min(tq, seq)
    nq = seq // tq
    return pl.pallas_call(
        _xattn_kernel,
        grid=(bsz, nq),
        in_specs=[
            pl.BlockSpec((tq, C_Q_W), lambda b, i: (b * nq + i, 0)),
            pl.BlockSpec((mem_len, C_Q_W), lambda b, i: (b, 0)),
            pl.BlockSpec((mem_len, C_V_W), lambda b, i: (b, 0)),
        ],
        out_specs=pl.BlockSpec((tq, C_V_W), lambda b, i: (b * nq + i, 0)),
        out_shape=jax.ShapeDtypeStruct((n, C_V_W), BF16),
        compiler_params=_cparams(("parallel", "parallel")),
        name="cross_attn",
    )(q, mk, mv)


def _merge_kernel(x_ref, ym_ref, yd_ref, yc_ref, gm_ref, gd_ref, gc_ref,
                  wm_ref, wd_ref, wc_ref, wo_ref, g_ref, o_ref, h_ref):
    merged = gm_ref[...].astype(F32) * jnp.dot(ym_ref[...], wm_ref[...], preferred_element_type=F32)
    merged = merged + gd_ref[...].astype(F32) * jnp.dot(yd_ref[...], wd_ref[...], preferred_element_type=F32)
    merged = merged + gc_ref[...].astype(F32) * jnp.dot(yc_ref[...], wc_ref[...], preferred_element_type=F32)
    x1 = x_ref[...] + jnp.dot(merged.astype(BF16), wo_ref[...], preferred_element_type=F32)
    o_ref[...] = x1
    y = x1 * lax.rsqrt(jnp.mean(x1 * x1, axis=-1, keepdims=True) + EPS)
    h_ref[...] = (y * g_ref[...]).astype(BF16)


def _merge(x, ym, yd, yc, gates, g_col0, wm, wd, wc, wo, g_next, tm):
    n, d = x.shape
    tm = min(tm, n)
    row = lambda i: (i, 0)
    wspec = pl.BlockSpec((d, d), lambda i: (0, 0), pipeline_mode=pl.Buffered(1))
    return pl.pallas_call(
        _merge_kernel,
        grid=(n // tm,),
        in_specs=[
            pl.BlockSpec((tm, d), row),
            pl.BlockSpec((tm, d), row),
            pl.BlockSpec((tm, d), row),
            pl.BlockSpec((tm, d), row),
            pl.BlockSpec((tm, d), lambda i: (i, g_col0)),
            pl.BlockSpec((tm, d), lambda i: (i, g_col0 + 1)),
            pl.BlockSpec((tm, d), lambda i: (i, g_col0 + 2)),
            wspec, wspec, wspec, wspec,
            pl.BlockSpec((1, d), lambda i: (0, 0)),
        ],
        out_specs=[pl.BlockSpec((tm, d), row), pl.BlockSpec((tm, d), row)],
        out_shape=[jax.ShapeDtypeStruct((n, d), F32), jax.ShapeDtypeStruct((n, d), BF16)],
        compiler_params=_cparams(("parallel",)),
        name="merge",
    )(x, ym, yd, yc, gates, gates, gates, wm, wd, wc, wo, g_next.reshape(1, d).astype(F32))


def _mlp_kernel(x_ref, h_ref, wu_ref, wd_ref, o_ref, *, tf):
    h = h_ref[...]
    acc = x_ref[...]
    for f in range(wu_ref.shape[1] // tf):
        u = jnp.maximum(jnp.dot(h, wu_ref[:, f * tf:(f + 1) * tf], preferred_element_type=F32), 0.0)
        acc = acc + jnp.dot((u * u).astype(BF16), wd_ref[f * tf:(f + 1) * tf, :], preferred_element_type=F32)
    o_ref[...] = acc


def _mlp(x, h, wu, wd, tm, tf):
    n, d = x.shape
    dff = wu.shape[1]
    tm = min(tm, n)
    return pl.pallas_call(
        functools.partial(_mlp_kernel, tf=tf),
        grid=(n // tm,),
        in_specs=[
            pl.BlockSpec((tm, d), lambda i: (i, 0)),
            pl.BlockSpec((tm, d), lambda i: (i, 0)),
            pl.BlockSpec((d, dff), lambda i: (0, 0), pipeline_mode=pl.Buffered(1)),
            pl.BlockSpec((dff, d), lambda i: (0, 0), pipeline_mode=pl.Buffered(1)),
        ],
        out_specs=pl.BlockSpec((tm, d), lambda i: (i, 0)),
        out_shape=jax.ShapeDtypeStruct((n, d), F32),
        compiler_params=_cparams(("parallel",)),
        name="mlp",
    )(x, h, wu, wd)


def _lambda_init(layer):
    return 0.8 - 0.6 * math.exp(-0.3 * layer)


def _layer(l, x2, mem2, bsz, seq, mem_len, p):
    split = [0]
    for w in (M_QK_W, M_QK_W, M_V_W, M_HEADS, M_HEADS, M_V_W, D_Q_W, D_Q_W, D_V_W, C_Q_W):
        split.append(split[-1] + w)
    w_in = p['w_in'][l]
    g_mix = p['norm_mix_g'][l]

    def head_major(a):
        lead = a.shape[:-1]
        return a.reshape(lead + (2, M_HEADS, M_DQK)).swapaxes(-3, -2).reshape(lead + (2 * M_QK_W,))

    conv_w_half = 0.5 * head_major(p['conv_w'][l].astype(F32))
    conv_b_half = 0.5 * head_major(p['conv_b'][l].astype(F32))

    q_gain = jnp.tile(p['dq_norm_g'][l].astype(F32), D_Q_W // D_HD) * (D_HD ** -0.5 * LOG2E)
    k_gain = jnp.tile(p['dk_norm_g'][l].astype(F32), D_Q_W // D_HD)
    cq_gain = jnp.tile(p['cq_norm_g'][l].astype(F32), C_HEADS) * (C_DQK ** -0.5 * LOG2E)
    if_pad = CQ_IF_W - C_Q_W - 2 * M_HEADS
    w_all = _pack_w_in(w_in.T, p['w_gate'][l], tuple(split))
    aux_all = _aux_rows(jnp.concatenate(
        [jnp.zeros((M_V_W,), F32), 0.5 * p['b_gate'][l].astype(F32), q_gain, k_gain,
         jnp.zeros((M_V_W + D_V_W + 2 * M_QK_W,), F32), cq_gain, p['b_igate'][l].astype(F32),
         p['b_fgate'][l].astype(F32), jnp.zeros((if_pad,), F32)]))
    win = {}
    c0 = 0
    for name, width in (("gates", M_V_W + N_BRANCH * D_MODEL), ("dqk", 2 * D_Q_W),
                        ("vqk", M_V_W + D_V_W + 2 * M_QK_W), ("cq_if", CQ_IF_W)):
        win[name] = (c0, width)
        c0 += width

    vqk, h = _norm_matmul(x2, g_mix, w_all, aux_all, win["vqk"], _ep_identity, BF16, NORM_CALL_TM, PROJ_TN,
                          "proj_vqk", wt=True)
    sgates = _matmul(h, w_all, aux_all, win["gates"], _ep_sigmoid_of_double, BF16, PROJ_WIDE_TM, PROJ_TN,
                     "proj_gates", wt=True)
    dqk = _matmul(h, w_all, aux_all, win["dqk"], functools.partial(_ep_group_norm, D_HD), BF16,
                  PROJ_NORM_TM, PROJ_TN, "proj_dqk", wt=True)
    cq, gates_if = _proj_cq_if(h, w_all, aux_all, win["cq_if"], SMALL_CALL_TM)
    w_kv = p['w_mem_kv'][l]
    w_kv = jnp.concatenate([w_kv[:, C_Q_W:], w_kv[:, :C_Q_W]], axis=1).astype(BF16)
    aux_kv = _aux_rows(jnp.concatenate([jnp.zeros((C_V_W,), F32), jnp.tile(p['ck_norm_g'][l].astype(F32), C_HEADS)]))
    mk, hmem = _norm_matmul(mem2, p['mem_norm_g'][l], w_kv, aux_kv, (C_V_W, C_Q_W),
                            functools.partial(_ep_group_norm, C_DQK), BF16, NORM_CALL_TM, C_Q_W, "proj_mk")
    mv = _matmul(hmem, w_kv, aux_kv, (0, C_V_W), _ep_identity, BF16, NORM_CALL_TM, C_Q_W, "proj_mv")

    y_m = _mlstm(vqk, (M_V_W + D_V_W) // (2 * M_QK_W), vqk, 0, gates_if, sgates, 0, conv_w_half, conv_b_half,
                 p['m_norm_g'][l], bsz, seq, MLSTM_NB if bsz % MLSTM_NB == 0 else 1)

    lam_i = _lambda_init(l)
    lam = (jnp.exp(jnp.sum(p['lam_q1'][l].astype(F32) * p['lam_k1'][l].astype(F32)))
           - jnp.exp(jnp.sum(p['lam_q2'][l].astype(F32) * p['lam_k2'][l].astype(F32))) + lam_i)
    score_bound = 1.02 * D_HD * jnp.max(jnp.abs(q_gain)) * jnp.max(jnp.abs(k_gain))
    dattn = functools.partial(_diff_attention, lam, dqk, vqk, M_V_W // D_DV, p['subln_g'][l], bsz, seq,
                              1.0 - lam_i)
    y_d = lax.cond(score_bound <= MAX_UNSTABILISED_SCORE,
                   lambda: dattn(False, DATTN_TQ_FAST, DATTN_TK_FAST),
                   lambda: dattn(True, DATTN_TQ_STABILISED, DATTN_TK_STABILISED))

    y_c = _cross_attention(cq, mk, mv, bsz, seq, mem_len)

    x2, h2 = _merge(x2, y_m, y_d, y_c, sgates, 1,
                    p['w_proj_m'][l].astype(BF16), p['w_proj_d'][l].astype(BF16),
                    p['w_proj_c'][l].astype(BF16), p['w_out'][l].astype(BF16), p['norm_mlp_g'][l], MERGE_TM)
    return _mlp(x2, h2, p['w_up'][l].astype(BF16), p['w_down'][l].astype(BF16), MLP_TM, MLP_TF)


def kernel(x, mem, norm_mix_g, w_in, b_igate, b_fgate, conv_w, conv_b, m_norm_g, dq_norm_g, dk_norm_g, lam_q1, lam_k1, lam_q2, lam_k2, subln_g, cq_norm_g, ck_norm_g, mem_norm_g, w_mem_kv, w_gate, b_gate, w_proj_m, w_proj_d, w_proj_c, w_out, norm_mlp_g, w_up, w_down):
    p = dict(norm_mix_g=norm_mix_g, w_in=w_in, b_igate=b_igate, b_fgate=b_fgate, conv_w=conv_w, conv_b=conv_b,
             m_norm_g=m_norm_g, dq_norm_g=dq_norm_g, dk_norm_g=dk_norm_g, lam_q1=lam_q1, lam_k1=lam_k1,
             lam_q2=lam_q2, lam_k2=lam_k2, subln_g=subln_g, cq_norm_g=cq_norm_g, ck_norm_g=ck_norm_g,
             mem_norm_g=mem_norm_g, w_mem_kv=w_mem_kv, w_gate=w_gate, b_gate=b_gate, w_proj_m=w_proj_m,
             w_proj_d=w_proj_d, w_proj_c=w_proj_c, w_out=w_out, norm_mlp_g=norm_mlp_g, w_up=w_up, w_down=w_down)
    bsz, seq, d = x.shape
    mem_len = mem.shape[1]
    x2 = x.reshape(bsz * seq, d)
    mem2 = mem.reshape(bsz * mem_len, d)
    for l in range(w_in.shape[0]):
        x2 = _layer(l, x2, mem2, bsz, seq, mem_len, p)
    return x2.reshape(bsz, seq, d)
```
